```python
import math
import jax, jax.numpy as jnp
from jax import lax
import numpy as np

D_MODEL = 1024
BATCH = 8
SEQ = 2048
DEPTH = 2

CHUNK = 64
Q_BLOCK = 128
HEAD_DIM = 64
D_MIX = D_MODEL
RET_HEADS = 6
RET_DIM = HEAD_DIM
DSA_HEADS = 5
DSA_DIM = HEAD_DIM
IDX_HEADS = 4
IDX_DIM = 64
DSA_TOPK_MAX = 256
DIFF_HEADS = 5
DIFF_QK_DIM = HEAD_DIM // 2
DIFF_V_DIM = 2 * DIFF_QK_DIM
D_RET = RET_HEADS * RET_DIM
D_DSA = DSA_HEADS * DSA_DIM
D_DIFF = DIFF_HEADS * DIFF_V_DIM
D_FF = 2816
N_BUCKETS = 32
MAX_DISTANCE = 128
N_BIAS_HEADS = DSA_HEADS + DIFF_HEADS
ROPE_BASE = 10000.0
LN_EPS = 1e-5
HEAD_NORM_EPS = 1e-6
DEEPNORM_ALPHA = (2 * DEPTH) ** 0.25
DEEPNORM_BETA = (8 * DEPTH) ** -0.25
IN_SPLITS = (D_RET, D_RET, D_RET, D_RET,
             D_DSA, D_DSA, D_DSA,
             IDX_HEADS * IDX_DIM, IDX_DIM, IDX_HEADS,
             2 * DIFF_HEADS * DIFF_QK_DIM, 2 * DIFF_HEADS * DIFF_QK_DIM, D_DIFF)
V_SEGMENTS = (2, 6, 12)
N_IN = sum(IN_SPLITS)

kernel_name = "hymba_style_retention_dsa_diffattn_macaron_deepnorm"


def layer_norm(x, g, b):
    xf = x.astype(jnp.float32)
    mu = xf.mean(-1, keepdims=True)
    var = jnp.square(xf - mu).mean(-1, keepdims=True)
    y = (xf - mu) * lax.rsqrt(var + LN_EPS)
    return (y * g + b).astype(x.dtype)


def swiglu(x, wg, wu, wd):
    return (jax.nn.silu(x @ wg) * (x @ wu)) @ wd


def rel_bucket(rel):
    half = N_BUCKETS // 2
    max_exact = half // 2
    direction = jnp.where(rel > 0, half, 0)
    n = jnp.abs(rel)
    nf = jnp.maximum(n, 1).astype(jnp.float32)
    large = max_exact + (jnp.log(nf / max_exact) / math.log(MAX_DISTANCE / max_exact)
                         * (half - max_exact)).astype(jnp.int32)
    large = jnp.minimum(large, half - 1)
    return (direction + jnp.where(n < max_exact, n, large)).astype(jnp.int32)


def rotary(x, pos):
    d = x.shape[-1]
    inv = ROPE_BASE ** (-jnp.arange(0, d, 2, dtype=jnp.float32) / d)
    ang = pos.astype(jnp.float32)[:, None] * inv[None, :]
    cos = jnp.cos(ang)[:, None, :]
    sin = jnp.sin(ang)[:, None, :]
    xf = x.astype(jnp.float32)
    x1, x2 = xf[..., : d // 2], xf[..., d // 2:]
    return jnp.concatenate([x1 * cos - x2 * sin, x1 * sin + x2 * cos], -1).astype(x.dtype)


def head_norm(o, eps):
    of = o.astype(jnp.float32)
    mu = of.mean(-1, keepdims=True)
    var = jnp.square(of - mu).mean(-1, keepdims=True)
    return (of - mu) * lax.rsqrt(var + eps)


def to_query_blocks(a):
    b, s = a.shape[:2]
    return jnp.moveaxis(a.reshape(b, s // Q_BLOCK, Q_BLOCK, *a.shape[2:]), 1, 0)


def from_query_blocks(a):
    a = jnp.moveaxis(a, 0, 1)
    return a.reshape(a.shape[0], -1, *a.shape[3:])


def retention(q, k, v, g):
    B, S, H, d = q.shape
    n_chunks = S // CHUNK
    pos = jnp.arange(S)
    q = rotary(q, pos) * (d ** -0.5)
    k = rotary(k, pos)
    gamma = 1.0 - 2.0 ** (-5.0 - jnp.arange(H, dtype=jnp.float32))
    log_g = jnp.log(gamma)
    i = jnp.arange(CHUNK, dtype=jnp.float32)
    intra_decay = jnp.exp(jnp.abs(i[:, None] - i[None, :])[None] * log_g[:, None, None])
    q_decay = jnp.exp((i[None, :] + 1.0) * log_g[:, None])
    k_decay = jnp.exp((CHUNK - 1.0 - i)[None, :] * log_g[:, None])
    chunk_decay = jnp.exp(CHUNK * log_g)
    qc = q.reshape(B, n_chunks, CHUNK, H, d)
    kc = k.reshape(B, n_chunks, CHUNK, H, d)
    vc = v.reshape(B, n_chunks, CHUNK, H, d)
    scores = jnp.einsum('bncha,bnsha->bnhcs', qc, kc) * intra_decay
    intra = jnp.einsum('bnhcs,bnshe->bnche', scores, vc)
    chunk_states = jnp.einsum('bnsha,hs,bnshe->bnhae', kc, k_decay, vc)

    def step(state, s_n):
        return chunk_decay[None, :, None, None] * state + s_n, state

    state0 = jnp.zeros((B, H, d, d), chunk_states.dtype)
    _, state_in = lax.scan(step, state0, jnp.moveaxis(chunk_states, 1, 0))
    state_in = jnp.moveaxis(state_in, 0, 1)
    cross = jnp.einsum('bncha,hc,bnhae->bnche', qc, q_decay, state_in)
    o = (intra + cross).reshape(B, S, H, d)
    o = head_norm(o, HEAD_NORM_EPS)
    return (jax.nn.silu(g.astype(jnp.float32)) * o).reshape(B, S, H * d).astype(v.dtype)


def dsa_attention(q, k, v, qi, ki, wi, bias_tab):
    B, S, H, d = q.shape
    top_k = min(DSA_TOPK_MAX, S // 4)
    key_pos = jnp.arange(S)

    def block(args):
        qb, qib, wib, qpos = args
        limit = (qpos // CHUNK + 1) * CHUNK
        visible = key_pos[None, :] < limit[:, None]
        rel = jax.nn.relu(jnp.einsum('bqhi,bsi->bqhs', qib, ki).astype(jnp.float32) * (IDX_DIM ** -0.5))
        score = jnp.einsum('bqhs,bqh->bqs', rel, wib.astype(jnp.float32) * (IDX_HEADS ** -0.5))
        score = jnp.where(visible[None], score, -jnp.inf)
        _, sel = lax.top_k(score, top_k)
        sel_ok = sel < limit[None, :, None]
        k_sel = jax.vmap(lambda kb, ib: kb[ib])(k, sel)
        v_sel = jax.vmap(lambda vb, ib: vb[ib])(v, sel)
        logits = jnp.einsum('bqhd,bqkhd->bhqk', qb, k_sel).astype(jnp.float32) * (d ** -0.5)
        bias = bias_tab[rel_bucket(sel - qpos[None, :, None])]
        logits = logits + jnp.moveaxis(bias, -1, 1).astype(jnp.float32)
        logits = jnp.where(sel_ok[:, None], logits, -jnp.inf)
        p = jax.nn.softmax(logits, axis=-1).astype(v.dtype)
        return jnp.einsum('bhqk,bqkhd->bqhd', p, v_sel)

    out = lax.map(block, (to_query_blocks(q), to_query_blocks(qi), to_query_blocks(wi),
                          key_pos.reshape(S // Q_BLOCK, Q_BLOCK)))
    return from_query_blocks(out).reshape(B, S, H * d)


def diff_attention(q, k, v, lam_vecs, subln_g, bias_tab, layer_idx):
    B, S, H, _, dq = q.shape
    key_pos = jnp.arange(S)
    lambda_init = 0.8 - 0.6 * math.exp(-0.3 * layer_idx)
    lv = lam_vecs.astype(jnp.float32)
    lam = jnp.exp(jnp.sum(lv[0] * lv[1])) - jnp.exp(jnp.sum(lv[2] * lv[3])) + lambda_init

    def block(args):
        qb, qpos = args
        limit = (qpos // CHUNK + 1) * CHUNK
        visible = key_pos[None, :] < limit[:, None]
        logits = jnp.einsum('bqhmd,bshmd->bhmqs', qb, k).astype(jnp.float32) * (dq ** -0.5)
        bias = bias_tab[rel_bucket(key_pos[None, :] - qpos[:, None])]
        logits = logits + jnp.transpose(bias, (2, 0, 1))[None, :, None].astype(jnp.float32)
        logits = jnp.where(visible[None, None, None], logits, -jnp.inf)
        p = jax.nn.softmax(logits, axis=-1)
        attn = p[:, :, 0] - lam * p[:, :, 1]
        return jnp.einsum('bhqs,bshe->bqhe', attn.astype(v.dtype), v)

    out = from_query_blocks(lax.map(block, (to_query_blocks(q), key_pos.reshape(S // Q_BLOCK, Q_BLOCK))))
    of = out.astype(jnp.float32)
    of = of * lax.rsqrt(jnp.mean(of * of, -1, keepdims=True) + LN_EPS) * subln_g
    return (of * (1.0 - lambda_init)).reshape(B, S, H * DIFF_V_DIM).astype(v.dtype)


def hybrid_mixer(x, w_in, w_out, lam_vecs, subln_g, rel_bias, layer_idx):
    B, S, _ = x.shape
    h = x @ w_in
    offsets = np.cumsum(IN_SPLITS)[:-1].tolist()
    (rq, rk, rv, rg, aq, ak, av, iq, ik, iw, dq, dk, dv) = jnp.split(h, offsets, axis=-1)
    r4 = lambda a: a.reshape(B, S, RET_HEADS, RET_DIM)
    ret_out = retention(r4(rq), r4(rk), r4(rv), r4(rg))
    a4 = lambda a: a.reshape(B, S, DSA_HEADS, DSA_DIM)
    dsa_out = dsa_attention(a4(aq), a4(ak), a4(av), iq.reshape(B, S, IDX_HEADS, IDX_DIM), ik, iw,
                            rel_bias[:, :DSA_HEADS])
    d5 = lambda a: a.reshape(B, S, DIFF_HEADS, 2, DIFF_QK_DIM)
    diff_out = diff_attention(d5(dq), d5(dk), dv.reshape(B, S, DIFF_HEADS, DIFF_V_DIM), lam_vecs, subln_g,
                              rel_bias[:, DSA_HEADS:], layer_idx)
    mixed = jnp.concatenate([ret_out.astype(x.dtype), dsa_out.astype(x.dtype), diff_out.astype(x.dtype)], -1)
    return mixed @ w_out


def setup_inputs(seed: int = 0) -> dict:
    key = jax.random.key(seed)
    ks = jax.random.split(key, 14)
    f32 = jnp.float32
    nrm = lambda k, shape: jax.random.normal(k, shape, f32)
    col_scale = jnp.concatenate([
        jnp.full((n,), DEEPNORM_BETA if i in V_SEGMENTS else 1.0, f32) for i, n in enumerate(IN_SPLITS)])
    return {
        "x": nrm(ks[0], (BATCH, SEQ, D_MODEL)),
        "w_in": nrm(ks[1], (DEPTH, D_MODEL, N_IN)) * (D_MODEL ** -0.5) * col_scale,
        "w_out": nrm(ks[2], (DEPTH, D_MIX, D_MODEL)) * (D_MIX ** -0.5) * DEEPNORM_BETA,
        "ffn1_wg": nrm(ks[3], (DEPTH, D_MODEL, D_FF)) * (D_MODEL ** -0.5),
        "ffn1_wu": nrm(ks[4], (DEPTH, D_MODEL, D_FF)) * (D_MODEL ** -0.5) * DEEPNORM_BETA,
        "ffn1_wd": nrm(ks[5], (DEPTH, D_FF, D_MODEL)) * (D_FF ** -0.5) * DEEPNORM_BETA,
        "ffn2_wg": nrm(ks[6], (DEPTH, D_MODEL, D_FF)) * (D_MODEL ** -0.5),
        "ffn2_wu": nrm(ks[7], (DEPTH, D_MODEL, D_FF)) * (D_MODEL ** -0.5) * DEEPNORM_BETA,
        "ffn2_wd": nrm(ks[8], (DEPTH, D_FF, D_MODEL)) * (D_FF ** -0.5) * DEEPNORM_BETA,
        "ln_g": 1.0 + 0.02 * nrm(ks[9], (DEPTH, 3, D_MODEL)),
        "ln_b": 0.02 * nrm(ks[10], (DEPTH, 3, D_MODEL)),
        "diff_lambda": 0.1 * nrm(ks[11], (DEPTH, 4, DIFF_QK_DIM)),
        "diff_subln_g": 1.0 + 0.02 * nrm(ks[12], (DEPTH, DIFF_V_DIM)),
        "rel_bias": 0.2 * nrm(ks[13], (N_BUCKETS, N_BIAS_HEADS)),
    }


def reference(x, w_in, w_out, ffn1_wg, ffn1_wu, ffn1_wd, ffn2_wg, ffn2_wu, ffn2_wd,
              ln_g, ln_b, diff_lambda, diff_subln_g, rel_bias):
    for l in range(DEPTH):
        x = layer_norm(DEEPNORM_ALPHA * x + 0.5 * swiglu(x, ffn1_wg[l], ffn1_wu[l], ffn1_wd[l]),
                       ln_g[l, 0], ln_b[l, 0])
        x = layer_norm(DEEPNORM_ALPHA * x + hybrid_mixer(x, w_in[l], w_out[l], diff_lambda[l],
                                                         diff_subln_g[l], rel_bias, l),
                       ln_g[l, 1], ln_b[l, 1])
        x = layer_norm(DEEPNORM_ALPHA * x + 0.5 * swiglu(x, ffn2_wg[l], ffn2_wu[l], ffn2_wd[l]),
                       ln_g[l, 2], ln_b[l, 2])
    return x
```

```python
import functools
import math

import numpy as np
import jax
import jax.numpy as jnp
from jax import lax
from jax.experimental import pallas as pl
from jax.experimental.pallas import tpu as pltpu

D_MODEL = 1024
DEPTH = 2
CHUNK = 64
HEAD_DIM = 64
RET_HEADS = 6
DSA_HEADS = 5
IDX_HEADS = 4
IDX_DIM = 64
DSA_TOPK_MAX = 256
DIFF_HEADS = 5
DIFF_QK_DIM = HEAD_DIM // 2
D_RET = RET_HEADS * HEAD_DIM
D_DSA = DSA_HEADS * HEAD_DIM
D_DIFF = DIFF_HEADS * HEAD_DIM
D_FF = 2816
N_BUCKETS = 32
MAX_DISTANCE = 128
ROPE_BASE = 10000.0
LN_EPS = 1e-5
HEAD_NORM_EPS = 1e-6
DEEPNORM_ALPHA = (2 * DEPTH) ** 0.25

LANES = 128
PAIR = LANES // HEAD_DIM
VMEM_LIMIT_BYTES = 56 * 1024 * 1024

D_HEADS_PAD = 3 * LANES
TQ = 256
TK = 256
TR = 256
TM_FFN = 512
TF = 256
TM_PROJ = 512
TM_OUT = 512
NEG = -1e30

F32 = jnp.float32
BF16 = jnp.bfloat16


def _dot(a, b):
    return jnp.dot(a, b, preferred_element_type=F32)


def _dot_nt(a, b):
    return lax.dot_general(a, b, (((1,), (1,)), ((), ())), preferred_element_type=F32)


def _layer_norm(r, g, b):
    mu = jnp.mean(r, axis=-1, keepdims=True)
    d = r - mu
    var = jnp.mean(d * d, axis=-1, keepdims=True)
    return d * lax.rsqrt(var + LN_EPS) * g + b


def _group_mean(x, avg_bf16):
    hi = x.astype(BF16)
    lo = (x - hi.astype(F32)).astype(BF16)
    return _dot(hi, avg_bf16) + _dot(lo, avg_bf16)


def _cparams(n_axes):
    return pltpu.CompilerParams(
        dimension_semantics=("arbitrary",) * n_axes,
        vmem_limit_bytes=VMEM_LIMIT_BYTES,
    )


def _rel_bucket_static(rel):
    rel = np.asarray(rel, dtype=np.int64)
    half = N_BUCKETS // 2
    max_exact = half // 2
    n = np.abs(rel)
    large = np.full(n.shape, max_exact, dtype=np.int64)
    for k in range(1, 64):
        large = np.where(64 * (2 ** k) <= n * n, max_exact + k, large)
    large = np.minimum(large, half - 1)
    return (np.where(rel > 0, half, 0) + np.where(n < max_exact, n, large)).astype(np.int32)


@functools.lru_cache(maxsize=None)
def _band_bucket_index():
    i = np.arange(TQ)[:, None]
    j = np.arange(TK)[None, :]
    prev = _rel_bucket_static(j - TK - i)
    diag = _rel_bucket_static(j - i)
    visible = j < (i // CHUNK + 1) * CHUNK
    diag = np.where(visible, diag, -1)
    return np.stack([prev, diag]).astype(np.int32)


@functools.lru_cache(maxsize=None)
def _rotary_tables(seq):
    lane = np.arange(LANES)
    d = lane % HEAD_DIM
    f = d % (HEAD_DIM // 2)
    inv = ROPE_BASE ** (-(2.0 * f) / HEAD_DIM)
    ang = np.arange(seq, dtype=np.float64)[:, None] * inv[None, :]
    cos = np.cos(ang)
    sin = np.sin(ang) * np.where(d < HEAD_DIM // 2, -1.0, 1.0)[None, :]
    return cos.astype(np.float32), sin.astype(np.float32)


@functools.lru_cache(maxsize=None)
def _retention_tables():
    gamma = 1.0 - 2.0 ** (-5.0 - np.arange(RET_HEADS, dtype=np.float64))
    t = np.arange(TR)
    dist = np.abs(t[:, None] - t[None, :]).astype(np.float64)
    visible = t[None, :] < (t[:, None] // CHUNK + 1) * CHUNK
    intra = np.stack([np.where(visible, g ** dist, 0.0) for g in gamma])
    lane_head = np.arange(D_HEADS_PAD) // HEAD_DIM
    g_lane = gamma[lane_head]
    qdec = g_lane[None, :] ** (t[:, None] + 1.0)
    kdec = g_lane[None, :] ** (TR - 1.0 - t[:, None])
    a = np.arange(LANES)
    same = (a[:, None] // HEAD_DIM) == (a[None, :] // HEAD_DIM)
    sdec = np.stack([np.where(same, (gamma[PAIR * p + a // HEAD_DIM] ** TR)[:, None], 0.0)
                     for p in range(RET_HEADS // PAIR)])
    return (intra.astype(np.float32), qdec.astype(np.float32), kdec.astype(np.float32),
            sdec.astype(np.float32), same.astype(np.float32))


@functools.lru_cache(maxsize=None)
def _group_avg():
    a = np.arange(LANES)
    same = (a[:, None] // HEAD_DIM) == (a[None, :] // HEAD_DIM)
    return np.where(same, 1.0 / HEAD_DIM, 0.0).astype(np.float32)


@functools.lru_cache(maxsize=None)
def _prefix_tri():
    a = np.arange(TK)
    return (a[:, None] <= a[None, :]).astype(np.float32)


def _lane_group(shape):
    return lax.broadcasted_iota(jnp.int32, shape, len(shape) - 1) // HEAD_DIM


def _ffn_ln_kernel(x_ref, wg_ref, wu_ref, wd_ref, g_ref, b_ref, o_ref):
    x = x_ref[...]
    xb = x.astype(BF16)
    acc = jnp.zeros(x.shape, F32)
    for c in range(D_FF // TF):
        cols = pl.ds(c * TF, TF)
        gate = _dot(xb, wg_ref[:, cols])
        up = _dot(xb, wu_ref[:, cols])
        h = gate * jax.nn.sigmoid(gate) * up
        acc = acc + _dot(h.astype(BF16), wd_ref[cols, :])
    r = DEEPNORM_ALPHA * x + 0.5 * acc
    o_ref[...] = _layer_norm(r, g_ref[...], b_ref[...])


def _ffn_ln(x, wg, wu, wd, g, b):
    t = x.shape[0]
    full = lambda shape: pl.BlockSpec(shape, lambda i: (0, 0))
    return pl.pallas_call(
        _ffn_ln_kernel,
        grid=(t // TM_FFN,),
        in_specs=[pl.BlockSpec((TM_FFN, D_MODEL), lambda i: (i, 0)),
                  full((D_MODEL, D_FF)), full((D_MODEL, D_FF)), full((D_FF, D_MODEL)),
                  full((1, D_MODEL)), full((1, D_MODEL))],
        out_specs=pl.BlockSpec((TM_FFN, D_MODEL), lambda i: (i, 0)),
        out_shape=jax.ShapeDtypeStruct((t, D_MODEL), F32),
        compiler_params=_cparams(1),
        name="ffn_ln",
    )(x, wg, wu, wd, g, b)


_PROJ_SEGMENTS = (
    ("rq", D_HEADS_PAD, F32), ("rk", D_HEADS_PAD, F32), ("rv", D_HEADS_PAD, BF16), ("rg", D_HEADS_PAD, F32),
    ("aq", D_HEADS_PAD, BF16), ("ak", D_HEADS_PAD, BF16), ("av", D_HEADS_PAD, BF16),
    ("iq", IDX_HEADS * IDX_DIM, BF16), ("ik", LANES, BF16), ("iw", LANES, F32),
    ("dq", D_HEADS_PAD, BF16), ("dk", D_HEADS_PAD, BF16), ("dv", D_HEADS_PAD, BF16),
)
N_PROJ = sum(w for _, w, _ in _PROJ_SEGMENTS)


def _pad_w_in(w_in):
    splits = (D_RET, D_RET, D_RET, D_RET, D_DSA, D_DSA, D_DSA,
              IDX_HEADS * IDX_DIM, IDX_DIM, IDX_HEADS, D_DIFF, D_DIFF, D_DIFF)
    offs = np.cumsum((0,) + splits)
    seg = [w_in[:, offs[i]:offs[i + 1]] for i in range(len(splits))]
    zeros = lambda n: jnp.zeros((w_in.shape[0], n), w_in.dtype)
    pad = lambda a, n: jnp.concatenate([a, zeros(n - a.shape[1])], axis=1)
    cols = [seg[0], seg[1], seg[2], seg[3],
            pad(seg[4], D_HEADS_PAD), pad(seg[5], D_HEADS_PAD), pad(seg[6], D_HEADS_PAD),
            seg[7], jnp.concatenate([seg[8], seg[8]], axis=1), pad(seg[9], LANES),
            pad(seg[10], D_HEADS_PAD), pad(seg[11], D_HEADS_PAD), pad(seg[12], D_HEADS_PAD)]
    return jnp.concatenate(cols, axis=1)


def _proj_kernel(x_ref, w_ref, *o_refs):
    xb = x_ref[...].astype(BF16)
    off = 0
    for (_, width, dtype), o_ref in zip(_PROJ_SEGMENTS, o_refs):
        o_ref[...] = _dot(xb, w_ref[:, pl.ds(off, width)]).astype(dtype)
        off += width


def _proj(x, w_pad):
    t = x.shape[0]
    return pl.pallas_call(
        _proj_kernel,
        grid=(t // TM_PROJ,),
        in_specs=[pl.BlockSpec((TM_PROJ, D_MODEL), lambda i: (i, 0)),
                  pl.BlockSpec((D_MODEL, N_PROJ), lambda i: (0, 0))],
        out_specs=[pl.BlockSpec((TM_PROJ, w), lambda i: (i, 0)) for _, w, _ in _PROJ_SEGMENTS],
        out_shape=[jax.ShapeDtypeStruct((t, w), dt) for _, w, dt in _PROJ_SEGMENTS],
        compiler_params=_cparams(1),
        name="in_proj",
    )(x, w_pad)


def _swap_halves(x):
    lane = lax.broadcasted_iota(jnp.int32, x.shape, 1)
    first_half = (lane % HEAD_DIM) < (HEAD_DIM // 2)
    from_right = pltpu.roll(x, LANES - HEAD_DIM // 2, 1)
    from_left = pltpu.roll(x, HEAD_DIM // 2, 1)
    return jnp.where(first_half, from_right, from_left)


def _retention_kernel(q_ref, k_ref, v_ref, g_ref, cos_ref, sin_ref, intra_ref, qdec_ref, kdec_ref,
                      sdec_ref, same_ref, avg_ref, o_ref, state_ref):
    @pl.when(pl.program_id(1) == 0)
    def _():
        state_ref[...] = jnp.zeros(state_ref.shape, F32)

    cos = cos_ref[...]
    sin = sin_ref[...]
    avg = avg_ref[...]
    group = _lane_group((TR, LANES))
    for p in range(RET_HEADS // PAIR):
        lanes = pl.ds(p * LANES, LANES)
        q = q_ref[0, :, lanes]
        k = k_ref[0, :, lanes]
        q = (q * cos + _swap_halves(q) * sin) * (HEAD_DIM ** -0.5)
        k = k * cos + _swap_halves(k) * sin
        v = v_ref[0, :, lanes]
        kb = k.astype(BF16)
        intra = jnp.zeros((TR, LANES), F32)
        for j in range(PAIR):
            qh = jnp.where(group == j, q, 0.0).astype(BF16)
            scores = _dot_nt(qh, kb) * intra_ref[PAIR * p + j]
            oh = _dot(scores.astype(BF16), v)
            intra = jnp.where(group == j, oh, intra)
        state = state_ref[p]
        cross = _dot((q * qdec_ref[:, lanes]).astype(BF16), state.astype(BF16))
        o = intra + cross
        kd_t = (k * kdec_ref[:, lanes]).T.astype(BF16)
        state_ref[p] = state * sdec_ref[p] + _dot(kd_t, v) * same_ref[...]
        mu = _group_mean(o, avg)
        d = o - mu
        var = _group_mean(d * d, avg)
        normed = d * lax.rsqrt(var + HEAD_NORM_EPS)
        gate = g_ref[0, :, lanes]
        o_ref[0, :, lanes] = (gate * jax.nn.sigmoid(gate) * normed).astype(o_ref.dtype)


def _retention(rq, rk, rv, rg, batch, seq):
    cos, sin = _rotary_tables(seq)
    intra, qdec, kdec, sdec, same = _retention_tables()
    avg = jnp.asarray(_group_avg(), BF16)
    r3 = lambda a: a.reshape(batch, seq, D_HEADS_PAD)
    tok = pl.BlockSpec((1, TR, D_HEADS_PAD), lambda b, n: (b, n, 0))
    const = lambda a: pl.BlockSpec(a.shape, lambda b, n: (0,) * a.ndim)
    pos = pl.BlockSpec((TR, LANES), lambda b, n: (n, 0))
    out = pl.pallas_call(
        _retention_kernel,
        grid=(batch, seq // TR),
        in_specs=[tok, tok, tok, tok, pos, pos, const(intra), const(qdec), const(kdec),
                  const(sdec), const(same), const(avg)],
        out_specs=tok,
        out_shape=jax.ShapeDtypeStruct((batch, seq, D_HEADS_PAD), BF16),
        scratch_shapes=[pltpu.VMEM((RET_HEADS // PAIR, LANES, LANES), F32)],
        compiler_params=_cparams(2),
        name="retention",
    )(r3(rq), r3(rk), r3(rv), r3(rg), cos, sin, intra, qdec, kdec, sdec, same, avg)
    return out.reshape(batch * seq, D_HEADS_PAD)


def _build_band(bidx_ref, tab_ref, band_ref, n_heads):
    for h in range(n_heads):
        band_ref[h, 0] = jnp.full((TQ, TK), tab_ref[h, N_BUCKETS // 2 - 1], F32)
        for t in range(2):
            idx = bidx_ref[t]
            val = jnp.full((TQ, TK), NEG, F32)
            for bkt in range(N_BUCKETS):
                val = jnp.where(idx == bkt, tab_ref[h, bkt], val)
            band_ref[h, 1 + t] = val


def _band_tile(kt, qb):
    return jnp.clip(kt - qb + 2, 0, 2)


def _key_to_f32(key):
    bits = jnp.where(key < 0, key ^ jnp.int32(-2 ** 31), ~key)
    return lax.bitcast_convert_type(bits, F32)


def _dsa_kernel(tab_ref, q_ref, k_ref, v_ref, iq_ref, ik_ref, iw_ref, bidx_ref, tri_ref,
                o_ref, band_ref, score_ref, m_ref, l_ref, acc_ref):
    qb = pl.program_id(1)
    n_tiles = qb + 1

    @pl.when((pl.program_id(0) == 0) & (qb == 0))
    def _():
        _build_band(bidx_ref, tab_ref, band_ref, DSA_HEADS)

    group = _lane_group((TQ, LANES))

    iw = iw_ref[0] * (IDX_HEADS ** -0.5)
    iq = [jnp.where(group == (h % PAIR), iq_ref[0, :, pl.ds((h // PAIR) * LANES, LANES)], 0)
          for h in range(IDX_HEADS)]

    def score_tile(kt, carry):
        ki = ik_ref[0, pl.ds(pl.multiple_of(kt * TK, TK), TK), :]
        s = jnp.zeros((TQ, TK), F32)
        for h in range(IDX_HEADS):
            rel = jnp.maximum(_dot_nt(iq[h], ki) * (IDX_DIM ** -0.5), 0.0)
            s = s + rel * iw[:, h:h + 1]
        visible = band_ref[0, _band_tile(kt, qb)] > 0.5 * NEG
        score_ref[kt] = jnp.where(visible, s, -jnp.inf)
        return carry

    lax.fori_loop(0, n_tiles, score_tile, 0)

    k_sel = float(DSA_TOPK_MAX)

    def count_ge(thr):
        def body(kt, c):
            return c + jnp.sum(jnp.where(score_ref[kt] >= thr, 1.0, 0.0), axis=-1, keepdims=True)
        return lax.fori_loop(0, n_tiles, body, jnp.zeros((TQ, 1), F32))

    def bisect(i, prefix):
        cand = prefix | lax.shift_left(jnp.int32(1), 31 - i)
        take = count_ge(_key_to_f32(cand)) >= k_sel
        return jnp.where(take, cand, prefix)

    prefix = lax.fori_loop(0, 32, bisect, jnp.zeros((TQ, 1), jnp.int32))
    row = lax.broadcasted_iota(jnp.int32, (TQ, 1), 0)
    limit = qb * TQ + (row // CHUNK + 1) * CHUNK
    select_all = limit <= DSA_TOPK_MAX
    thr = jnp.where(select_all, -jnp.inf, _key_to_f32(prefix))

    def count_gt(kt, c):
        return c + jnp.sum(jnp.where(score_ref[kt] > thr, 1.0, 0.0), axis=-1, keepdims=True)

    n_gt = lax.fori_loop(0, n_tiles, count_gt, jnp.zeros((TQ, 1), F32))
    need = jnp.where(select_all, 0.0, k_sel - n_gt)

    m_ref[...] = jnp.full(m_ref.shape, NEG, F32)
    l_ref[...] = jnp.zeros(l_ref.shape, F32)
    acc_ref[...] = jnp.zeros(acc_ref.shape, F32)
    qh = [jnp.where(group == (h % PAIR), q_ref[0, :, pl.ds((h // PAIR) * LANES, LANES)], 0)
          for h in range(DSA_HEADS)]
    tri = tri_ref[...]

    def attend(kt, tied_before):
        rows = pl.ds(pl.multiple_of(kt * TK, TK), TK)
        s = score_ref[kt]
        tied = s == thr
        tied_upto = tied_before + _dot(jnp.where(tied, 1.0, 0.0).astype(BF16), tri)
        sel = (s > thr) | (tied & (tied_upto <= need))
        bt = _band_tile(kt, qb)
        for h in range(DSA_HEADS):
            lanes = pl.ds((h // PAIR) * LANES, LANES)
            logits = _dot_nt(qh[h], k_ref[0, rows, lanes]) * (HEAD_DIM ** -0.5) + band_ref[h, bt]
            m_old = m_ref[h]
            m_new = jnp.maximum(m_old, jnp.max(jnp.where(sel, logits, NEG), axis=-1, keepdims=True))
            p = jnp.where(sel, jnp.exp(logits - m_new), 0.0)
            alpha = jnp.exp(m_old - m_new)
            l_ref[h] = alpha * l_ref[h] + jnp.sum(p, axis=-1, keepdims=True)
            acc_ref[h] = alpha * acc_ref[h] + _dot(p.astype(BF16), v_ref[0, rows, lanes])
            m_ref[h] = m_new
        return tied_before + jnp.sum(jnp.where(tied, 1.0, 0.0), axis=-1, keepdims=True)

    lax.fori_loop(0, n_tiles, attend, jnp.zeros((TQ, 1), F32))

    for p in range(D_HEADS_PAD // LANES):
        out = jnp.zeros((TQ, LANES), F32)
        for j in range(PAIR):
            h = PAIR * p + j
            if h < DSA_HEADS:
                out = jnp.where(group == j, acc_ref[h] / l_ref[h], out)
        o_ref[0, :, pl.ds(p * LANES, LANES)] = out.astype(o_ref.dtype)


def _dsa(aq, ak, av, iq, ik, iw, bias_t, batch, seq):
    bidx = jnp.asarray(_band_bucket_index())
    tri = jnp.asarray(_prefix_tri(), BF16)
    r3 = lambda a: a.reshape(batch, seq, a.shape[-1])
    qblk = lambda w: pl.BlockSpec((1, TQ, w), lambda b, n: (b, n, 0))
    kblk = lambda w: pl.BlockSpec((1, seq, w), lambda b, n: (b, 0, 0))
    const = lambda a: pl.BlockSpec(a.shape, lambda b, n: (0,) * a.ndim)
    out = pl.pallas_call(
        _dsa_kernel,
        grid=(batch, seq // TQ),
        in_specs=[pl.BlockSpec(memory_space=pltpu.SMEM),
                  qblk(D_HEADS_PAD), kblk(D_HEADS_PAD), kblk(D_HEADS_PAD),
                  qblk(IDX_HEADS * IDX_DIM), kblk(LANES), qblk(LANES), const(bidx), const(tri)],
        out_specs=qblk(D_HEADS_PAD),
        out_shape=jax.ShapeDtypeStruct((batch, seq, D_HEADS_PAD), BF16),
        scratch_shapes=[pltpu.VMEM((DSA_HEADS, 3, TQ, TK), F32),
                        pltpu.VMEM((seq // TK, TQ, TK), F32),
                        pltpu.VMEM((DSA_HEADS, TQ, 1), F32),
                        pltpu.VMEM((DSA_HEADS, TQ, 1), F32),
                        pltpu.VMEM((DSA_HEADS, TQ, LANES), F32)],
        compiler_params=_cparams(2),
        name="dsa_attention",
    )(bias_t, r3(aq), r3(ak), r3(av), r3(iq), r3(ik), r3(iw), bidx, tri)
    return out.reshape(batch * seq, D_HEADS_PAD)


def _diff_kernel(lambda_init, tab_ref, q_ref, k_ref, v_ref, lam_ref, g_ref, bidx_ref, avg_ref,
                 o_ref, band_ref, m_ref, l_ref, acc_ref):
    qb = pl.program_id(1)

    @pl.when((pl.program_id(0) == 0) & (qb == 0))
    def _():
        _build_band(bidx_ref, tab_ref, band_ref, DIFF_HEADS)

    lv = lam_ref[...]
    lam = (jnp.exp(jnp.sum(lv[0:1] * lv[1:2], axis=-1, keepdims=True))
           - jnp.exp(jnp.sum(lv[2:3] * lv[3:4], axis=-1, keepdims=True)) + lambda_init)

    m_ref[...] = jnp.full(m_ref.shape, NEG, F32)
    l_ref[...] = jnp.zeros(l_ref.shape, F32)
    acc_ref[...] = jnp.zeros(acc_ref.shape, F32)

    lane = lax.broadcasted_iota(jnp.int32, (TQ, LANES), 1)
    q2 = []
    for h in range(DIFF_HEADS):
        qp = q_ref[0, :, pl.ds((h // PAIR) * LANES, LANES)]
        parts = []
        for mm in range(2):
            lo = (h % PAIR) * HEAD_DIM + mm * DIFF_QK_DIM
            parts.append(jnp.where((lane >= lo) & (lane < lo + DIFF_QK_DIM), qp, 0))
        q2.append(jnp.concatenate(parts, axis=0))

    def attend(kt, carry):
        rows = pl.ds(pl.multiple_of(kt * TK, TK), TK)
        bt = _band_tile(kt, qb)
        for h in range(DIFF_HEADS):
            lanes = pl.ds((h // PAIR) * LANES, LANES)
            bias = band_ref[h, bt]
            logits = (_dot_nt(q2[h], k_ref[0, rows, lanes]) * (DIFF_QK_DIM ** -0.5)
                      + jnp.concatenate([bias, bias], axis=0))
            m_old = m_ref[h]
            m_new = jnp.maximum(m_old, jnp.max(logits, axis=-1, keepdims=True))
            p = jnp.exp(logits - m_new)
            alpha = jnp.exp(m_old - m_new)
            l_ref[h] = alpha * l_ref[h] + jnp.sum(p, axis=-1, keepdims=True)
            acc_ref[h] = alpha * acc_ref[h] + _dot(p.astype(BF16), v_ref[0, rows, lanes])
            m_ref[h] = m_new
        return carry

    lax.fori_loop(0, qb + 1, attend, 0)

    group = _lane_group((TQ, LANES))
    avg = avg_ref[...]
    for p in range(D_HEADS_PAD // LANES):
        out = jnp.zeros((TQ, LANES), F32)
        for j in range(PAIR):
            h = PAIR * p + j
            if h < DIFF_HEADS:
                o2 = acc_ref[h] / l_ref[h]
                out = jnp.where(group == j, o2[:TQ] - lam * o2[TQ:], out)
        ms = _group_mean(out * out, avg)
        out = out * lax.rsqrt(ms + LN_EPS) * g_ref[...] * (1.0 - lambda_init)
        o_ref[0, :, pl.ds(p * LANES, LANES)] = out.astype(o_ref.dtype)


def _diff(dq, dk, dv, lam_vecs, subln_g, bias_t, lambda_init, batch, seq):
    bidx = jnp.asarray(_band_bucket_index())
    avg = jnp.asarray(_group_avg(), BF16)
    g2 = jnp.concatenate([subln_g, subln_g]).reshape(1, LANES)
    r3 = lambda a: a.reshape(batch, seq, a.shape[-1])
    qblk = pl.BlockSpec((1, TQ, D_HEADS_PAD), lambda b, n: (b, n, 0))
    kblk = pl.BlockSpec((1, seq, D_HEADS_PAD), lambda b, n: (b, 0, 0))
    const = lambda a: pl.BlockSpec(a.shape, lambda b, n: (0,) * a.ndim)
    out = pl.pallas_call(
        functools.partial(_diff_kernel, lambda_init),
        grid=(batch, seq // TQ),
        in_specs=[pl.BlockSpec(memory_space=pltpu.SMEM),
                  qblk, kblk, kblk, const(lam_vecs), const(g2), const(bidx), const(avg)],
        out_specs=qblk,
        out_shape=jax.ShapeDtypeStruct((batch, seq, D_HEADS_PAD), BF16),
        scratch_shapes=[pltpu.VMEM((DIFF_HEADS, 3, TQ, TK), F32),
                        pltpu.VMEM((DIFF_HEADS, 2 * TQ, 1), F32),
                        pltpu.VMEM((DIFF_HEADS, 2 * TQ, 1), F32),
                        pltpu.VMEM((DIFF_HEADS, 2 * TQ, LANES), F32)],
        compiler_params=_cparams(2),
        name="diff_attention",
    )(bias_t, r3(dq), r3(dk), r3(dv), lam_vecs, g2, bidx, avg)
    return out.reshape(batch * seq, D_HEADS_PAD)


def _out_ln_kernel(x_ref, ret_ref, dsa_ref, dif_ref, w_ref, g_ref, b_ref, o_ref):
    acc = _dot(ret_ref[...], w_ref[pl.ds(0, D_HEADS_PAD), :])
    acc = acc + _dot(dsa_ref[...], w_ref[pl.ds(D_HEADS_PAD, D_HEADS_PAD), :])
    acc = acc + _dot(dif_ref[...], w_ref[pl.ds(2 * D_HEADS_PAD, D_HEADS_PAD), :])
    r = DEEPNORM_ALPHA * x_ref[...] + acc
    o_ref[...] = _layer_norm(r, g_ref[...], b_ref[...])


def _pad_w_out(w_out):
    zeros = jnp.zeros((D_HEADS_PAD - D_DSA, w_out.shape[1]), w_out.dtype)
    return jnp.concatenate([w_out[:D_RET], w_out[D_RET:D_RET + D_DSA], zeros,
                            w_out[D_RET + D_DSA:], zeros], axis=0)


def _out_ln(x, ret, dsa, dif, w_pad, g, b):
    t = x.shape[0]
    full = lambda shape: pl.BlockSpec(shape, lambda i: (0, 0))
    tok = lambda w: pl.BlockSpec((TM_OUT, w), lambda i: (i, 0))
    return pl.pallas_call(
        _out_ln_kernel,
        grid=(t // TM_OUT,),
        in_specs=[tok(D_MODEL), tok(D_HEADS_PAD), tok(D_HEADS_PAD), tok(D_HEADS_PAD),
                  full((3 * D_HEADS_PAD, D_MODEL)), full((1, D_MODEL)), full((1, D_MODEL))],
        out_specs=tok(D_MODEL),
        out_shape=jax.ShapeDtypeStruct((t, D_MODEL), F32),
        compiler_params=_cparams(1),
        name="out_proj_ln",
    )(x, ret, dsa, dif, w_pad, g, b)


def kernel(x, w_in, w_out, ffn1_wg, ffn1_wu, ffn1_wd, ffn2_wg, ffn2_wu, ffn2_wd,
           ln_g, ln_b, diff_lambda, diff_subln_g, rel_bias):
    batch, seq, d = x.shape
    assert d == D_MODEL and seq % TQ == 0 and seq % TR == 0 and (batch * seq) % TM_FFN == 0
    assert min(DSA_TOPK_MAX, seq // 4) == DSA_TOPK_MAX
    h = x.reshape(batch * seq, d)
    bias_t = rel_bias.T
    row = lambda a: a.reshape(1, D_MODEL)
    for l in range(DEPTH):
        lambda_init = 0.8 - 0.6 * math.exp(-0.3 * l)
        h = _ffn_ln(h, ffn1_wg[l].astype(BF16), ffn1_wu[l].astype(BF16), ffn1_wd[l].astype(BF16),
                    row(ln_g[l, 0]), row(ln_b[l, 0]))
        (rq, rk, rv, rg, aq, ak, av, iq, ik, iw, dq, dk, dv) = _proj(h, _pad_w_in(w_in[l]).astype(BF16))
        ret = _retention(rq, rk, rv, rg, batch, seq)
        dsa = _dsa(aq, ak, av, iq, ik, iw, bias_t[:DSA_HEADS], batch, seq)
        dif = _diff(dq, dk, dv, diff_lambda[l], diff_subln_g[l], bias_t[DSA_HEADS:], lambda_init, batch, seq)
        h = _out_ln(h, ret, dsa, dif, _pad_w_out(w_out[l]).astype(BF16), row(ln_g[l, 1]), row(ln_b[l, 1]))
        h = _ffn_ln(h, ffn2_wg[l].astype(BF16), ffn2_wu[l].astype(BF16), ffn2_wd[l].astype(BF16),
                    row(ln_g[l, 2]), row(ln_b[l, 2]))
    return h.reshape(batch, seq, d)
```

```python
import functools
import math

import numpy as np
import jax
import jax.numpy as jnp
from jax import lax
from jax.experimental import pallas as pl
from jax.experimental.pallas import tpu as pltpu

D_MODEL = 1024
DEPTH = 2
CHUNK = 64
HEAD_DIM = 64
RET_HEADS = 6
DSA_HEADS = 5
IDX_HEADS = 4
IDX_DIM = 64
DSA_TOPK_MAX = 256
DIFF_HEADS = 5
DIFF_QK_DIM = HEAD_DIM // 2
D_RET = RET_HEADS * HEAD_DIM
D_DSA = DSA_HEADS * HEAD_DIM
D_DIFF = DIFF_HEADS * HEAD_DIM
D_FF = 2816
N_BUCKETS = 32
MAX_DISTANCE = 128
ROPE_BASE = 10000.0
LN_EPS = 1e-5
HEAD_NORM_EPS = 1e-6
DEEPNORM_ALPHA = (2 * DEPTH) ** 0.25

LANES = 128
SUBLANES = 8
PAIR = LANES // HEAD_DIM
VMEM_LIMIT_BYTES = 56 * 1024 * 1024

D_HEADS_PAD = 3 * LANES
N_PAIRS = D_HEADS_PAD // LANES
TQ = 256
TK = 256
TR = 256
TM_FFN = 512
TF = 256
TM_PROJ = 512
TM_OUT = 512
NEG = -1e30
LOG2E = math.log2(math.e)

F32 = jnp.float32
BF16 = jnp.bfloat16


def _dot(a, b):
    return jnp.dot(a, b, preferred_element_type=F32)


def _dot_nt(a, b):
    return lax.dot_general(a, b, (((1,), (1,)), ((), ())), preferred_element_type=F32)


def _layer_norm(r, g, b):
    mu = jnp.mean(r, axis=-1, keepdims=True)
    d = r - mu
    var = jnp.mean(d * d, axis=-1, keepdims=True)
    return d * lax.rsqrt(var + LN_EPS) * g + b


def _group_mean(x, avg_bf16):
    hi = x.astype(BF16)
    lo = (x - hi.astype(F32)).astype(BF16)
    return _dot(hi, avg_bf16) + _dot(lo, avg_bf16)


def _cparams(n_axes):
    return pltpu.CompilerParams(
        dimension_semantics=("arbitrary",) * n_axes,
        vmem_limit_bytes=VMEM_LIMIT_BYTES,
    )


def _rel_bucket_static(rel):
    rel = np.asarray(rel, dtype=np.int64)
    half = N_BUCKETS // 2
    max_exact = half // 2
    n = np.abs(rel)
    large = np.full(n.shape, max_exact, dtype=np.int64)
    for k in range(1, 64):
        large = np.where(64 * (2 ** k) <= n * n, max_exact + k, large)
    large = np.minimum(large, half - 1)
    return (np.where(rel > 0, half, 0) + np.where(n < max_exact, n, large)).astype(np.int32)


@functools.lru_cache(maxsize=None)
def _band_bucket_index_t():
    j = np.arange(TK)[:, None]
    i = np.arange(TQ)[None, :]
    prev = _rel_bucket_static(j - TK - i)
    diag = _rel_bucket_static(j - i)
    visible = j < (i // CHUNK + 1) * CHUNK
    diag = np.where(visible, diag, -1)
    return np.stack([prev, diag]).astype(np.int32)


@functools.lru_cache(maxsize=None)
def _rotary_tables(seq):
    lane = np.arange(LANES)
    d = lane % HEAD_DIM
    f = d % (HEAD_DIM // 2)
    inv = ROPE_BASE ** (-(2.0 * f) / HEAD_DIM)
    ang = np.arange(seq, dtype=np.float64)[:, None] * inv[None, :]
    cos = np.cos(ang)
    sin = np.sin(ang) * np.where(d < HEAD_DIM // 2, -1.0, 1.0)[None, :]
    return cos.astype(np.float32), sin.astype(np.float32)


@functools.lru_cache(maxsize=None)
def _retention_tables():
    gamma = 1.0 - 2.0 ** (-5.0 - np.arange(RET_HEADS, dtype=np.float64))
    t = np.arange(TR)
    dist = np.abs(t[:, None] - t[None, :]).astype(np.float64)
    visible = t[None, :] < (t[:, None] // CHUNK + 1) * CHUNK
    intra = np.stack([np.where(visible, g ** dist, 0.0) for g in gamma])
    lane_head = np.arange(D_HEADS_PAD) // HEAD_DIM
    g_lane = gamma[lane_head]
    qdec = g_lane[None, :] ** (t[:, None] + 1.0)
    kdec = g_lane[None, :] ** (TR - 1.0 - t[:, None])
    a = np.arange(LANES)
    same = (a[:, None] // HEAD_DIM) == (a[None, :] // HEAD_DIM)
    sdec = np.stack([np.where(same, (gamma[PAIR * p + a // HEAD_DIM] ** TR)[:, None], 0.0)
                     for p in range(RET_HEADS // PAIR)])
    return (intra.astype(np.float32), qdec.astype(np.float32), kdec.astype(np.float32),
            sdec.astype(np.float32), same.astype(np.float32))


@functools.lru_cache(maxsize=None)
def _group_avg():
    a = np.arange(LANES)
    same = (a[:, None] // HEAD_DIM) == (a[None, :] // HEAD_DIM)
    return np.where(same, 1.0 / HEAD_DIM, 0.0).astype(np.float32)


@functools.lru_cache(maxsize=None)
def _prefix_tril():
    a = np.arange(TK)
    return (a[None, :] <= a[:, None]).astype(np.float32)


def _lane_group(shape):
    return lax.broadcasted_iota(jnp.int32, shape, len(shape) - 1) // HEAD_DIM


def _row_group(shape):
    return lax.broadcasted_iota(jnp.int32, shape, 0) // HEAD_DIM


def _ffn_ln_kernel(x_ref, wg_ref, wu_ref, wd_ref, g_ref, b_ref, o_ref):
    x = x_ref[...]
    xb = x.astype(BF16)
    acc = jnp.zeros(x.shape, F32)
    for c in range(D_FF // TF):
        cols = pl.ds(c * TF, TF)
        gate = _dot(xb, wg_ref[:, cols])
        up = _dot(xb, wu_ref[:, cols])
        h = gate * jax.nn.sigmoid(gate) * up
        acc = acc + _dot(h.astype(BF16), wd_ref[cols, :])
    r = DEEPNORM_ALPHA * x + 0.5 * acc
    o_ref[...] = _layer_norm(r, g_ref[...], b_ref[...])


def _ffn_ln(x, wg, wu, wd, g, b):
    t = x.shape[0]
    full = lambda shape: pl.BlockSpec(shape, lambda i: (0, 0))
    return pl.pallas_call(
        _ffn_ln_kernel,
        grid=(t // TM_FFN,),
        in_specs=[pl.BlockSpec((TM_FFN, D_MODEL), lambda i: (i, 0)),
                  full((D_MODEL, D_FF)), full((D_MODEL, D_FF)), full((D_FF, D_MODEL)),
                  full((1, D_MODEL)), full((1, D_MODEL))],
        out_specs=pl.BlockSpec((TM_FFN, D_MODEL), lambda i: (i, 0)),
        out_shape=jax.ShapeDtypeStruct((t, D_MODEL), F32),
        compiler_params=_cparams(1),
        name="ffn_ln",
    )(x, wg, wu, wd, g, b)


_IW_ROWS = SUBLANES
_TOK_SEGMENTS = (("rq", D_HEADS_PAD, F32), ("rk", D_HEADS_PAD, F32), ("rv", D_HEADS_PAD, BF16),
                 ("rg", D_HEADS_PAD, F32), ("ak", D_HEADS_PAD, BF16), ("dk", D_HEADS_PAD, BF16),
                 ("ik", LANES, BF16))
_T_SEGMENTS = (("aqT", D_HEADS_PAD, BF16), ("dqT", D_HEADS_PAD, BF16), ("iqT", IDX_HEADS * IDX_DIM, BF16),
               ("iwT", _IW_ROWS, F32))
_VT_SEGMENTS = (("avT", D_HEADS_PAD, BF16), ("dvT", D_HEADS_PAD, BF16))
N_TOK = sum(w for _, w, _ in _TOK_SEGMENTS)
N_T = sum(w for _, w, _ in _T_SEGMENTS + _VT_SEGMENTS)


def _split_w_in(w_in):
    splits = (D_RET, D_RET, D_RET, D_RET, D_DSA, D_DSA, D_DSA,
              IDX_HEADS * IDX_DIM, IDX_DIM, IDX_HEADS, D_DIFF, D_DIFF, D_DIFF)
    offs = np.cumsum((0,) + splits)
    (rq, rk, rv, rg, aq, ak, av, iq, ik, iw, dq, dk, dv) = [
        w_in[:, offs[i]:offs[i + 1]] for i in range(len(splits))]
    zeros = lambda n: jnp.zeros((w_in.shape[0], n), w_in.dtype)
    pad = lambda a, n: jnp.concatenate([a, zeros(n - a.shape[1])], axis=1)
    w_tok = jnp.concatenate([rq, rk, rv, rg, pad(ak, D_HEADS_PAD), pad(dk, D_HEADS_PAD),
                             ik, ik], axis=1)
    w_t = jnp.concatenate([pad(aq, D_HEADS_PAD), pad(dq, D_HEADS_PAD), iq, pad(iw, _IW_ROWS),
                           pad(av, D_HEADS_PAD), pad(dv, D_HEADS_PAD)], axis=1).T
    return w_tok, w_t


def _proj_kernel(x_ref, wtok_ref, wt_ref, *o_refs):
    xb = x_ref[...].astype(BF16)
    refs = list(o_refs)
    off = 0
    for _, width, dtype in _TOK_SEGMENTS:
        refs.pop(0)[...] = _dot(xb, wtok_ref[:, pl.ds(off, width)]).astype(dtype)
        off += width
    off = 0
    for _, width, dtype in _T_SEGMENTS:
        refs.pop(0)[...] = _dot_nt(wt_ref[pl.ds(off, width), :], xb).astype(dtype)
        off += width
    for _, width, dtype in _VT_SEGMENTS:
        o_ref = refs.pop(0)
        res = _dot_nt(wt_ref[pl.ds(off, width), :], xb).astype(dtype)
        for t in range(TM_PROJ // TK):
            o_ref[t] = res[:, t * TK:(t + 1) * TK]
        off += width


def _proj(x, w_tok, w_t):
    t = x.shape[0]
    out_specs = ([pl.BlockSpec((TM_PROJ, w), lambda i: (i, 0)) for _, w, _ in _TOK_SEGMENTS]
                 + [pl.BlockSpec((w, TM_PROJ), lambda i: (0, i)) for _, w, _ in _T_SEGMENTS]
                 + [pl.BlockSpec((TM_PROJ // TK, w, TK), lambda i: (i, 0, 0)) for _, w, _ in _VT_SEGMENTS])
    out_shape = ([jax.ShapeDtypeStruct((t, w), dt) for _, w, dt in _TOK_SEGMENTS]
                 + [jax.ShapeDtypeStruct((w, t), dt) for _, w, dt in _T_SEGMENTS]
                 + [jax.ShapeDtypeStruct((t // TK, w, TK), dt) for _, w, dt in _VT_SEGMENTS])
    return pl.pallas_call(
        _proj_kernel,
        grid=(t // TM_PROJ,),
        in_specs=[pl.BlockSpec((TM_PROJ, D_MODEL), lambda i: (i, 0)),
                  pl.BlockSpec((D_MODEL, N_TOK), lambda i: (0, 0)),
                  pl.BlockSpec((N_T, D_MODEL), lambda i: (0, 0))],
        out_specs=out_specs,
        out_shape=out_shape,
        compiler_params=_cparams(1),
        name="in_proj",
    )(x, w_tok, w_t)


def _swap_halves(x):
    lane = lax.broadcasted_iota(jnp.int32, x.shape, 1)
    first_half = (lane % HEAD_DIM) < (HEAD_DIM // 2)
    from_right = pltpu.roll(x, LANES - HEAD_DIM // 2, 1)
    from_left = pltpu.roll(x, HEAD_DIM // 2, 1)
    return jnp.where(first_half, from_right, from_left)


def _retention_kernel(q_ref, k_ref, v_ref, g_ref, cos_ref, sin_ref, intra_ref, qdec_ref, kdec_ref,
                      sdec_ref, same_ref, avg_ref, o_ref, state_ref):
    @pl.when(pl.program_id(1) == 0)
    def _():
        state_ref[...] = jnp.zeros(state_ref.shape, F32)

    cos = cos_ref[...]
    sin = sin_ref[...]
    avg = avg_ref[...]
    group = _lane_group((TR, LANES))
    for p in range(RET_HEADS // PAIR):
        lanes = pl.ds(p * LANES, LANES)
        q = q_ref[0, :, lanes]
        k = k_ref[0, :, lanes]
        q = (q * cos + _swap_halves(q) * sin) * (HEAD_DIM ** -0.5)
        k = k * cos + _swap_halves(k) * sin
        v = v_ref[0, :, lanes]
        kb = k.astype(BF16)
        intra = jnp.zeros((TR, LANES), F32)
        for j in range(PAIR):
            qh = jnp.where(group == j, q, 0.0).astype(BF16)
            scores = _dot_nt(qh, kb) * intra_ref[PAIR * p + j]
            oh = _dot(scores.astype(BF16), v)
            intra = jnp.where(group == j, oh, intra)
        state = state_ref[p]
        cross = _dot((q * qdec_ref[:, lanes]).astype(BF16), state.astype(BF16))
        o = intra + cross
        kd_t = (k * kdec_ref[:, lanes]).T.astype(BF16)
        state_ref[p] = state * sdec_ref[p] + _dot(kd_t, v) * same_ref[...]
        mu = _group_mean(o, avg)
        d = o - mu
        var = _group_mean(d * d, avg)
        normed = d * lax.rsqrt(var + HEAD_NORM_EPS)
        gate = g_ref[0, :, lanes]
        o_ref[0, :, lanes] = (gate * jax.nn.sigmoid(gate) * normed).astype(o_ref.dtype)


def _retention(rq, rk, rv, rg, batch, seq):
    cos, sin = _rotary_tables(seq)
    intra, qdec, kdec, sdec, same = _retention_tables()
    avg = jnp.asarray(_group_avg(), BF16)
    r3 = lambda a: a.reshape(batch, seq, D_HEADS_PAD)
    tok = pl.BlockSpec((1, TR, D_HEADS_PAD), lambda b, n: (b, n, 0))
    const = lambda a: pl.BlockSpec(a.shape, lambda b, n: (0,) * a.ndim)
    pos = pl.BlockSpec((TR, LANES), lambda b, n: (n, 0))
    out = pl.pallas_call(
        _retention_kernel,
        grid=(batch, seq // TR),
        in_specs=[tok, tok, tok, tok, pos, pos, const(intra), const(qdec), const(kdec),
                  const(sdec), const(same), const(avg)],
        out_specs=tok,
        out_shape=jax.ShapeDtypeStruct((batch, seq, D_HEADS_PAD), BF16),
        scratch_shapes=[pltpu.VMEM((RET_HEADS // PAIR, LANES, LANES), F32)],
        compiler_params=_cparams(2),
        name="retention",
    )(r3(rq), r3(rk), r3(rv), r3(rg), cos, sin, intra, qdec, kdec, sdec, same, avg)
    return out.reshape(batch * seq, D_HEADS_PAD)


def _build_band(bidx_ref, tab_ref, band_ref, n_heads):
    for h in range(n_heads):
        band_ref[h, 0] = jnp.full((TK, TQ), tab_ref[h, N_BUCKETS // 2 - 1] * LOG2E, F32)
        for t in range(2):
            idx = bidx_ref[t]
            val = jnp.full((TK, TQ), NEG, F32)
            for bkt in range(N_BUCKETS):
                val = jnp.where(idx == bkt, tab_ref[h, bkt] * LOG2E, val)
            band_ref[h, 1 + t] = val


def _band_tile(kt, qb):
    return jnp.clip(kt - qb + 2, 0, 2)


def _softmax_tile_update(logits2, vt, m_ref, l_ref, acc_ref, h):
    m_old = m_ref[h]
    m_new = jnp.maximum(m_old, jnp.max(logits2, axis=0, keepdims=True))
    p = jnp.exp2(logits2 - m_new)
    alpha = jnp.exp2(m_old - m_new)
    l_ref[h] = alpha * l_ref[h] + jnp.sum(p, axis=0, keepdims=True)
    acc_ref[h] = alpha * acc_ref[h] + _dot(vt, p.astype(BF16))
    m_ref[h] = m_new


def _pair_rows(acc_ref, l_ref, p, n_heads, cols):
    group = _row_group((LANES, cols))
    out = jnp.zeros((LANES, cols), F32)
    for j in range(PAIR):
        h = PAIR * p + j
        if h < n_heads:
            out = jnp.where(group == j, acc_ref[h] / l_ref[h], out)
    return out


def _key_to_f32(key):
    bits = jnp.where(key < 0, key ^ jnp.int32(-2 ** 31), ~key)
    return lax.bitcast_convert_type(bits, F32)


def _count_rows(ind):
    return jnp.sum(ind.reshape(TK // SUBLANES, SUBLANES, TQ), axis=0)


def _dsa_kernel(tab_ref, qt_ref, k_ref, vt_ref, iqt_ref, ik_ref, iwt_ref, bidx_ref, tril_ref,
                o_ref, band_ref, score_ref, m_ref, l_ref, acc_ref):
    qb = pl.program_id(1)
    n_tiles = qb + 1

    @pl.when((pl.program_id(0) == 0) & (qb == 0))
    def _():
        _build_band(bidx_ref, tab_ref, band_ref, DSA_HEADS)

    group = _row_group((LANES, TQ))

    iw = iwt_ref[...] * (IDX_HEADS ** -0.5)
    iq = [jnp.where(group == (h % PAIR), iqt_ref[pl.ds((h // PAIR) * LANES, LANES), :], 0)
          for h in range(IDX_HEADS)]

    def score_tile(kt, carry):
        ki = ik_ref[0, pl.ds(pl.multiple_of(kt * TK, TK), TK), :]
        s = jnp.zeros((TK, TQ), F32)
        for h in range(IDX_HEADS):
            rel = jnp.maximum(_dot(ki, iq[h]) * (IDX_DIM ** -0.5), 0.0)
            s = s + rel * iw[h:h + 1, :]
        visible = band_ref[0, _band_tile(kt, qb)] > 0.5 * NEG
        score_ref[kt] = jnp.where(visible, s, -jnp.inf)
        return carry

    lax.fori_loop(0, n_tiles, score_tile, 0)

    k_sel = float(DSA_TOPK_MAX)

    def count(pred):
        def body(kt, c):
            return c + _count_rows(jnp.where(pred(score_ref[kt]), 1.0, 0.0))
        c = lax.fori_loop(0, n_tiles, body, jnp.zeros((SUBLANES, TQ), F32))
        return jnp.sum(c, axis=0, keepdims=True)

    def bisect(i, prefix):
        cand = prefix | lax.shift_left(jnp.int32(1), 31 - i)
        cand_f = _key_to_f32(cand)
        take = count(lambda s: s >= cand_f) >= k_sel
        return jnp.where(take, cand, prefix)

    prefix = lax.fori_loop(0, 32, bisect, jnp.zeros((1, TQ), jnp.int32))
    col = lax.broadcasted_iota(jnp.int32, (1, TQ), 1)
    limit = qb * TQ + (col // CHUNK + 1) * CHUNK
    select_all = limit <= DSA_TOPK_MAX
    thr = jnp.where(select_all, -jnp.inf, _key_to_f32(prefix))
    n_gt = count(lambda s: s > thr)
    need = jnp.where(select_all, 0.0, k_sel - n_gt)

    m_ref[...] = jnp.full(m_ref.shape, NEG, F32)
    l_ref[...] = jnp.zeros(l_ref.shape, F32)
    acc_ref[...] = jnp.zeros(acc_ref.shape, F32)
    qh = [jnp.where(group == (h % PAIR), qt_ref[pl.ds((h // PAIR) * LANES, LANES), :], 0)
          for h in range(DSA_HEADS)]
    tril = tril_ref[...]

    def attend(kt, tied_before):
        rows = pl.ds(pl.multiple_of(kt * TK, TK), TK)
        s = score_ref[kt]
        tied = s == thr
        tied_upto = tied_before + _dot(tril, jnp.where(tied, 1.0, 0.0).astype(BF16))
        sel = (s > thr) | (tied & (tied_upto <= need))
        bt = _band_tile(kt, qb)
        for h in range(DSA_HEADS):
            pr = h // PAIR
            logits2 = (_dot(k_ref[0, rows, pl.ds(pr * LANES, LANES)], qh[h]) * (HEAD_DIM ** -0.5 * LOG2E)
                       + band_ref[h, bt])
            _softmax_tile_update(jnp.where(sel, logits2, NEG), vt_ref[0, kt, pl.ds(pr * LANES, LANES), :],
                                 m_ref, l_ref, acc_ref, h)
        return tied_upto[TK - 1:TK, :]

    lax.fori_loop(0, n_tiles, attend, jnp.zeros((1, TQ), F32))

    for p in range(N_PAIRS):
        out_t = _pair_rows(acc_ref, l_ref, p, DSA_HEADS, TQ)
        o_ref[0, :, pl.ds(p * LANES, LANES)] = out_t.T.astype(o_ref.dtype)


def _dsa(aqT, ak, avT, iqT, ik, iwT, bias_t, batch, seq):
    bidx = jnp.asarray(_band_bucket_index_t())
    tril = jnp.asarray(_prefix_tril(), BF16)
    nq = seq // TQ
    qcols = lambda w: pl.BlockSpec((w, TQ), lambda b, n: (0, b * nq + n))
    ktok = lambda w: pl.BlockSpec((1, seq, w), lambda b, n: (b, 0, 0))
    const = lambda a: pl.BlockSpec(a.shape, lambda b, n: (0,) * a.ndim)
    out = pl.pallas_call(
        _dsa_kernel,
        grid=(batch, nq),
        in_specs=[pl.BlockSpec(memory_space=pltpu.SMEM),
                  qcols(D_HEADS_PAD), ktok(D_HEADS_PAD),
                  pl.BlockSpec((1, seq // TK, D_HEADS_PAD, TK), lambda b, n: (b, 0, 0, 0)),
                  qcols(IDX_HEADS * IDX_DIM), ktok(LANES), qcols(_IW_ROWS), const(bidx), const(tril)],
        out_specs=pl.BlockSpec((1, TQ, D_HEADS_PAD), lambda b, n: (b, n, 0)),
        out_shape=jax.ShapeDtypeStruct((batch, seq, D_HEADS_PAD), BF16),
        scratch_shapes=[pltpu.VMEM((DSA_HEADS, 3, TK, TQ), F32),
                        pltpu.VMEM((seq // TK, TK, TQ), F32),
                        pltpu.VMEM((DSA_HEADS, 1, TQ), F32),
                        pltpu.VMEM((DSA_HEADS, 1, TQ), F32),
                        pltpu.VMEM((DSA_HEADS, LANES, TQ), F32)],
        compiler_params=_cparams(2),
        name="dsa_attention",
    )(bias_t, aqT, ak.reshape(batch, seq, D_HEADS_PAD),
      avT.reshape(batch, seq // TK, D_HEADS_PAD, TK), iqT, ik.reshape(batch, seq, LANES), iwT, bidx, tril)
    return out.reshape(batch * seq, D_HEADS_PAD)


def _diff_kernel(lambda_init, tab_ref, qt_ref, k_ref, vt_ref, lam_ref, g_ref, bidx_ref, avg_ref,
                 o_ref, band_ref, m_ref, l_ref, acc_ref):
    qb = pl.program_id(1)

    @pl.when((pl.program_id(0) == 0) & (qb == 0))
    def _():
        _build_band(bidx_ref, tab_ref, band_ref, DIFF_HEADS)

    lv = lam_ref[...]
    lam = (jnp.exp(jnp.sum(lv[0:1] * lv[1:2], axis=-1, keepdims=True))
           - jnp.exp(jnp.sum(lv[2:3] * lv[3:4], axis=-1, keepdims=True)) + lambda_init)

    m_ref[...] = jnp.full(m_ref.shape, NEG, F32)
    l_ref[...] = jnp.zeros(l_ref.shape, F32)
    acc_ref[...] = jnp.zeros(acc_ref.shape, F32)

    row = lax.broadcasted_iota(jnp.int32, (LANES, TQ), 0)
    q2 = []
    for h in range(DIFF_HEADS):
        qp = qt_ref[pl.ds((h // PAIR) * LANES, LANES), :]
        parts = []
        for mm in range(2):
            lo = (h % PAIR) * HEAD_DIM + mm * DIFF_QK_DIM
            parts.append(jnp.where((row >= lo) & (row < lo + DIFF_QK_DIM), qp, 0))
        q2.append(jnp.concatenate(parts, axis=1))

    def attend(kt, carry):
        rows = pl.ds(pl.multiple_of(kt * TK, TK), TK)
        bt = _band_tile(kt, qb)
        for h in range(DIFF_HEADS):
            pr = h // PAIR
            bias = band_ref[h, bt]
            logits2 = (_dot(k_ref[0, rows, pl.ds(pr * LANES, LANES)], q2[h]) * (DIFF_QK_DIM ** -0.5 * LOG2E)
                       + jnp.concatenate([bias, bias], axis=1))
            _softmax_tile_update(logits2, vt_ref[0, kt, pl.ds(pr * LANES, LANES), :], m_ref, l_ref, acc_ref, h)
        return carry

    lax.fori_loop(0, qb + 1, attend, 0)

    avg = avg_ref[...]
    for p in range(N_PAIRS):
        o2 = _pair_rows(acc_ref, l_ref, p, DIFF_HEADS, 2 * TQ)
        out = (o2[:, :TQ] - lam * o2[:, TQ:]).T
        ms = _group_mean(out * out, avg)
        out = out * lax.rsqrt(ms + LN_EPS) * g_ref[...] * (1.0 - lambda_init)
        o_ref[0, :, pl.ds(p * LANES, LANES)] = out.astype(o_ref.dtype)


def _diff(dqT, dk, dvT, lam_vecs, subln_g, bias_t, lambda_init, batch, seq):
    bidx = jnp.asarray(_band_bucket_index_t())
    avg = jnp.asarray(_group_avg(), BF16)
    g2 = jnp.concatenate([subln_g, subln_g]).reshape(1, LANES)
    nq = seq // TQ
    const = lambda a: pl.BlockSpec(a.shape, lambda b, n: (0,) * a.ndim)
    out = pl.pallas_call(
        functools.partial(_diff_kernel, lambda_init),
        grid=(batch, nq),
        in_specs=[pl.BlockSpec(memory_space=pltpu.SMEM),
                  pl.BlockSpec((D_HEADS_PAD, TQ), lambda b, n: (0, b * nq + n)),
                  pl.BlockSpec((1, seq, D_HEADS_PAD), lambda b, n: (b, 0, 0)),
                  pl.BlockSpec((1, seq // TK, D_HEADS_PAD, TK), lambda b, n: (b, 0, 0, 0)),
                  const(lam_vecs), const(g2), const(bidx), const(avg)],
        out_specs=pl.BlockSpec((1, TQ, D_HEADS_PAD), lambda b, n: (b, n, 0)),
        out_shape=jax.ShapeDtypeStruct((batch, seq, D_HEADS_PAD), BF16),
        scratch_shapes=[pltpu.VMEM((DIFF_HEADS, 3, TK, TQ), F32),
                        pltpu.VMEM((DIFF_HEADS, 1, 2 * TQ), F32),
                        pltpu.VMEM((DIFF_HEADS, 1, 2 * TQ), F32),
                        pltpu.VMEM((DIFF_HEADS, LANES, 2 * TQ), F32)],
        compiler_params=_cparams(2),
        name="diff_attention",
    )(bias_t, dqT, dk.reshape(batch, seq, D_HEADS_PAD),
      dvT.reshape(batch, seq // TK, D_HEADS_PAD, TK), lam_vecs, g2, bidx, avg)
    return out.reshape(batch * seq, D_HEADS_PAD)


def _out_ln_kernel(x_ref, ret_ref, dsa_ref, dif_ref, w_ref, g_ref, b_ref, o_ref):
    acc = _dot(ret_ref[...], w_ref[pl.ds(0, D_HEADS_PAD), :])
    acc = acc + _dot(dsa_ref[...], w_ref[pl.ds(D_HEADS_PAD, D_HEADS_PAD), :])
    acc = acc + _dot(dif_ref[...], w_ref[pl.ds(2 * D_HEADS_PAD, D_HEADS_PAD), :])
    r = DEEPNORM_ALPHA * x_ref[...] + acc
    o_ref[...] = _layer_norm(r, g_ref[...], b_ref[...])


def _pad_w_out(w_out):
    zeros = jnp.zeros((D_HEADS_PAD - D_DSA, w_out.shape[1]), w_out.dtype)
    return jnp.concatenate([w_out[:D_RET], w_out[D_RET:D_RET + D_DSA], zeros,
                            w_out[D_RET + D_DSA:], zeros], axis=0)


def _out_ln(x, ret, dsa, dif, w_pad, g, b):
    t = x.shape[0]
    full = lambda shape: pl.BlockSpec(shape, lambda i: (0, 0))
    tok = lambda w: pl.BlockSpec((TM_OUT, w), lambda i: (i, 0))
    return pl.pallas_call(
        _out_ln_kernel,
        grid=(t // TM_OUT,),
        in_specs=[tok(D_MODEL), tok(D_HEADS_PAD), tok(D_HEADS_PAD), tok(D_HEADS_PAD),
                  full((3 * D_HEADS_PAD, D_MODEL)), full((1, D_MODEL)), full((1, D_MODEL))],
        out_specs=tok(D_MODEL),
        out_shape=jax.ShapeDtypeStruct((t, D_MODEL), F32),
        compiler_params=_cparams(1),
        name="out_proj_ln",
    )(x, ret, dsa, dif, w_pad, g, b)


def kernel(x, w_in, w_out, ffn1_wg, ffn1_wu, ffn1_wd, ffn2_wg, ffn2_wu, ffn2_wd,
           ln_g, ln_b, diff_lambda, diff_subln_g, rel_bias):
    batch, seq, d = x.shape
    assert d == D_MODEL and seq % TQ == 0 and seq % TR == 0 and (batch * seq) % TM_FFN == 0
    assert min(DSA_TOPK_MAX, seq // 4) == DSA_TOPK_MAX
    h = x.reshape(batch * seq, d)
    bias_t = rel_bias.T
    row = lambda a: a.reshape(1, D_MODEL)
    for l in range(DEPTH):
        lambda_init = 0.8 - 0.6 * math.exp(-0.3 * l)
        h = _ffn_ln(h, ffn1_wg[l].astype(BF16), ffn1_wu[l].astype(BF16), ffn1_wd[l].astype(BF16),
                    row(ln_g[l, 0]), row(ln_b[l, 0]))
        w_tok, w_t = _split_w_in(w_in[l])
        (rq, rk, rv, rg, ak, dk, ik, aqT, dqT, iqT, iwT, avT, dvT) = _proj(
            h, w_tok.astype(BF16), w_t.astype(BF16))
        ret = _retention(rq, rk, rv, rg, batch, seq)
        dsa = _dsa(aqT, ak, avT, iqT, ik, iwT, bias_t[:DSA_HEADS], batch, seq)
        dif = _diff(dqT, dk, dvT, diff_lambda[l], diff_subln_g[l], bias_t[DSA_HEADS:], lambda_init, batch, seq)
        h = _out_ln(h, ret, dsa, dif, _pad_w_out(w_out[l]).astype(BF16), row(ln_g[l, 1]), row(ln_b[l, 1]))
        h = _ffn_ln(h, ffn2_wg[l].astype(BF16), ffn2_wu[l].astype(BF16), ffn2_wd[l].astype(BF16),
                    row(ln_g[l, 2]), row(ln_b[l, 2]))
    return h.reshape(batch, seq, d)
```

```python
import functools
import math

import numpy as np
import jax
import jax.numpy as jnp
from jax import lax
from jax.experimental import pallas as pl
from jax.experimental.pallas import tpu as pltpu

D_MODEL = 1024
DEPTH = 2
CHUNK = 64
HEAD_DIM = 64
RET_HEADS = 6
DSA_HEADS = 5
IDX_HEADS = 4
IDX_DIM = 64
DSA_TOPK_MAX = 256
DIFF_HEADS = 5
DIFF_QK_DIM = HEAD_DIM // 2
D_RET = RET_HEADS * HEAD_DIM
D_DSA = DSA_HEADS * HEAD_DIM
D_DIFF = DIFF_HEADS * HEAD_DIM
D_FF = 2816
N_BUCKETS = 32
MAX_DISTANCE = 128
ROPE_BASE = 10000.0
LN_EPS = 1e-5
HEAD_NORM_EPS = 1e-6
DEEPNORM_ALPHA = (2 * DEPTH) ** 0.25

LANES = 128
SUBLANES = 8
BF16_ROWS = 16
PAIR = LANES // HEAD_DIM
VMEM_LIMIT_BYTES = 56 * 1024 * 1024

D_HEADS_PAD = 3 * LANES
N_PAIRS = D_HEADS_PAD // LANES
VT_ROWS = HEAD_DIM + BF16_ROWS
TQ = 256
TK = 256
TR = 256
TM_FFN = 512
TF = 256
TM_PROJ = 512
TM_OUT = 512
N_ACC = 4
NEG = -1e30
LOG2E = math.log2(math.e)

F32 = jnp.float32
BF16 = jnp.bfloat16


def _dot(a, b):
    return jnp.dot(a, b, preferred_element_type=F32)


def _dot_nt(a, b):
    return lax.dot_general(a, b, (((1,), (1,)), ((), ())), preferred_element_type=F32)


def _layer_norm(r, g, b):
    mu = jnp.mean(r, axis=-1, keepdims=True)
    d = r - mu
    var = jnp.mean(d * d, axis=-1, keepdims=True)
    return d * lax.rsqrt(var + LN_EPS) * g + b


def _group_mean(x, avg_bf16):
    hi = x.astype(BF16)
    lo = (x - hi.astype(F32)).astype(BF16)
    return _dot(hi, avg_bf16) + _dot(lo, avg_bf16)


def _cparams(n_axes):
    return pltpu.CompilerParams(
        dimension_semantics=("arbitrary",) * n_axes,
        vmem_limit_bytes=VMEM_LIMIT_BYTES,
    )


def _rel_bucket_static(rel):
    rel = np.asarray(rel, dtype=np.int64)
    half = N_BUCKETS // 2
    max_exact = half // 2
    n = np.abs(rel)
    large = np.full(n.shape, max_exact, dtype=np.int64)
    for k in range(1, 64):
        large = np.where(64 * (2 ** k) <= n * n, max_exact + k, large)
    large = np.minimum(large, half - 1)
    return (np.where(rel > 0, half, 0) + np.where(n < max_exact, n, large)).astype(np.int32)


FAR_BUCKET = int(_rel_bucket_static(-(TK + 1)))


@functools.lru_cache(maxsize=None)
def _band_bucket_index_t():
    j = np.arange(TK)[:, None]
    i = np.arange(TQ)[None, :]
    prev = _rel_bucket_static(j - TK - i)
    diag = _rel_bucket_static(j - i)
    visible = j < (i // CHUNK + 1) * CHUNK
    diag = np.where(visible, diag, -1)
    return np.stack([prev, diag]).astype(np.int32)


@functools.lru_cache(maxsize=None)
def _rotary_tables(seq):
    lane = np.arange(LANES)
    d = lane % HEAD_DIM
    f = d % (HEAD_DIM // 2)
    inv = ROPE_BASE ** (-(2.0 * f) / HEAD_DIM)
    ang = np.arange(seq, dtype=np.float64)[:, None] * inv[None, :]
    cos = np.cos(ang)
    sin = np.sin(ang) * np.where(d < HEAD_DIM // 2, -1.0, 1.0)[None, :]
    return cos.astype(np.float32), sin.astype(np.float32)


@functools.lru_cache(maxsize=None)
def _retention_tables():
    gamma = 1.0 - 2.0 ** (-5.0 - np.arange(RET_HEADS, dtype=np.float64))
    t = np.arange(TR)
    dist = np.abs(t[:, None] - t[None, :]).astype(np.float64)
    visible = t[None, :] < (t[:, None] // CHUNK + 1) * CHUNK
    intra = np.stack([np.where(visible, g ** dist, 0.0) for g in gamma])
    lane_head = np.arange(D_HEADS_PAD) // HEAD_DIM
    g_lane = gamma[lane_head]
    qdec = g_lane[None, :] ** (t[:, None] + 1.0)
    kdec = g_lane[None, :] ** (TR - 1.0 - t[:, None])
    a = np.arange(LANES)
    same = (a[:, None] // HEAD_DIM) == (a[None, :] // HEAD_DIM)
    sdec = np.stack([np.where(same, (gamma[PAIR * p + a // HEAD_DIM] ** TR)[:, None], 0.0)
                     for p in range(RET_HEADS // PAIR)])
    return (intra.astype(np.float32), qdec.astype(np.float32), kdec.astype(np.float32),
            sdec.astype(np.float32), same.astype(np.float32))


@functools.lru_cache(maxsize=None)
def _group_avg():
    a = np.arange(LANES)
    same = (a[:, None] // HEAD_DIM) == (a[None, :] // HEAD_DIM)
    return np.where(same, 1.0 / HEAD_DIM, 0.0).astype(np.float32)


@functools.lru_cache(maxsize=None)
def _prefix_tril():
    a = np.arange(TK)
    return (a[None, :] <= a[:, None]).astype(np.float32)


def _lane_group(shape):
    return lax.broadcasted_iota(jnp.int32, shape, len(shape) - 1) // HEAD_DIM


def _ffn_ln_kernel(x_ref, wg_ref, wu_ref, wd_ref, g_ref, b_ref, o_ref):
    x = x_ref[...]
    xb = x.astype(BF16)
    acc = jnp.zeros(x.shape, F32)
    for c in range(D_FF // TF):
        cols = pl.ds(c * TF, TF)
        gate = _dot(xb, wg_ref[:, cols])
        up = _dot(xb, wu_ref[:, cols])
        h = gate * jax.nn.sigmoid(gate) * up
        acc = acc + _dot(h.astype(BF16), wd_ref[cols, :])
    r = DEEPNORM_ALPHA * x + 0.5 * acc
    o_ref[...] = _layer_norm(r, g_ref[...], b_ref[...])


def _ffn_ln(x, wg, wu, wd, g, b):
    t = x.shape[0]
    full = lambda shape: pl.BlockSpec(shape, lambda i: (0, 0))
    return pl.pallas_call(
        _ffn_ln_kernel,
        grid=(t // TM_FFN,),
        in_specs=[pl.BlockSpec((TM_FFN, D_MODEL), lambda i: (i, 0)),
                  full((D_MODEL, D_FF)), full((D_MODEL, D_FF)), full((D_FF, D_MODEL)),
                  full((1, D_MODEL)), full((1, D_MODEL))],
        out_specs=pl.BlockSpec((TM_FFN, D_MODEL), lambda i: (i, 0)),
        out_shape=jax.ShapeDtypeStruct((t, D_MODEL), F32),
        compiler_params=_cparams(1),
        name="ffn_ln",
    )(x, wg, wu, wd, g, b)


_IW_ROWS = BF16_ROWS
_TOK_SEGMENTS = (("rq", D_HEADS_PAD, F32), ("rk", D_HEADS_PAD, F32), ("rv", D_HEADS_PAD, BF16),
                 ("rg", D_HEADS_PAD, F32), ("ak", D_HEADS_PAD, BF16), ("dk", D_HEADS_PAD, BF16),
                 ("ik", LANES, BF16))
_T_SEGMENTS = (("aqT", D_HEADS_PAD, BF16), ("dqT", D_HEADS_PAD, BF16), ("iqT", IDX_HEADS * IDX_DIM, BF16),
               ("iwT", _IW_ROWS, F32))
_VT_SEGMENTS = (("avT", DSA_HEADS), ("dvT", DIFF_HEADS))
N_TOK = sum(w for _, w, _ in _TOK_SEGMENTS)
N_T = sum(w for _, w, _ in _T_SEGMENTS) + sum(n * HEAD_DIM for _, n in _VT_SEGMENTS)


def _split_w_in(w_in):
    splits = (D_RET, D_RET, D_RET, D_RET, D_DSA, D_DSA, D_DSA,
              IDX_HEADS * IDX_DIM, IDX_DIM, IDX_HEADS, D_DIFF, D_DIFF, D_DIFF)
    offs = np.cumsum((0,) + splits)
    (rq, rk, rv, rg, aq, ak, av, iq, ik, iw, dq, dk, dv) = [
        w_in[:, offs[i]:offs[i + 1]] for i in range(len(splits))]
    zeros = lambda n: jnp.zeros((w_in.shape[0], n), w_in.dtype)
    pad = lambda a, n: jnp.concatenate([a, zeros(n - a.shape[1])], axis=1)
    w_tok = jnp.concatenate([rq, rk, rv, rg, pad(ak, D_HEADS_PAD), pad(dk, D_HEADS_PAD),
                             ik, ik], axis=1)
    w_t = jnp.concatenate([pad(aq, D_HEADS_PAD), pad(dq, D_HEADS_PAD), iq, pad(iw, _IW_ROWS),
                           av, dv], axis=1).T
    return w_tok, w_t


def _proj_kernel(x_ref, wtok_ref, wt_ref, *o_refs):
    xb = x_ref[...].astype(BF16)
    refs = list(o_refs)
    off = 0
    for _, width, dtype in _TOK_SEGMENTS:
        refs.pop(0)[...] = _dot(xb, wtok_ref[:, pl.ds(off, width)]).astype(dtype)
        off += width
    off = 0
    for _, width, dtype in _T_SEGMENTS:
        refs.pop(0)[...] = _dot_nt(wt_ref[pl.ds(off, width), :], xb).astype(dtype)
        off += width
    ones = jnp.ones((BF16_ROWS, TK), BF16)
    for _, n_heads in _VT_SEGMENTS:
        o_ref = refs.pop(0)
        res = _dot_nt(wt_ref[pl.ds(off, n_heads * HEAD_DIM), :], xb).astype(BF16)
        for t in range(TM_PROJ // TK):
            for h in range(n_heads):
                o_ref[t, pl.ds(h * VT_ROWS, HEAD_DIM), :] = res[h * HEAD_DIM:(h + 1) * HEAD_DIM,
                                                                t * TK:(t + 1) * TK]
                o_ref[t, pl.ds(h * VT_ROWS + HEAD_DIM, BF16_ROWS), :] = ones
        off += n_heads * HEAD_DIM


def _proj(x, w_tok, w_t):
    t = x.shape[0]
    out_specs = ([pl.BlockSpec((TM_PROJ, w), lambda i: (i, 0)) for _, w, _ in _TOK_SEGMENTS]
                 + [pl.BlockSpec((w, TM_PROJ), lambda i: (0, i)) for _, w, _ in _T_SEGMENTS]
                 + [pl.BlockSpec((TM_PROJ // TK, n * VT_ROWS, TK), lambda i: (i, 0, 0)) for _, n in _VT_SEGMENTS])
    out_shape = ([jax.ShapeDtypeStruct((t, w), dt) for _, w, dt in _TOK_SEGMENTS]
                 + [jax.ShapeDtypeStruct((w, t), dt) for _, w, dt in _T_SEGMENTS]
                 + [jax.ShapeDtypeStruct((t // TK, n * VT_ROWS, TK), BF16) for _, n in _VT_SEGMENTS])
    return pl.pallas_call(
        _proj_kernel,
        grid=(t // TM_PROJ,),
        in_specs=[pl.BlockSpec((TM_PROJ, D_MODEL), lambda i: (i, 0)),
                  pl.BlockSpec((D_MODEL, N_TOK), lambda i: (0, 0)),
                  pl.BlockSpec((N_T, D_MODEL), lambda i: (0, 0))],
        out_specs=out_specs,
        out_shape=out_shape,
        compiler_params=_cparams(1),
        name="in_proj",
    )(x, w_tok, w_t)


def _swap_halves(x):
    lane = lax.broadcasted_iota(jnp.int32, x.shape, 1)
    first_half = (lane % HEAD_DIM) < (HEAD_DIM // 2)
    from_right = pltpu.roll(x, LANES - HEAD_DIM // 2, 1)
    from_left = pltpu.roll(x, HEAD_DIM // 2, 1)
    return jnp.where(first_half, from_right, from_left)


def _retention_kernel(q_ref, k_ref, v_ref, g_ref, cos_ref, sin_ref, intra_ref, qdec_ref, kdec_ref,
                      sdec_ref, same_ref, avg_ref, o_ref, state_ref):
    @pl.when(pl.program_id(1) == 0)
    def _():
        state_ref[...] = jnp.zeros(state_ref.shape, F32)

    cos = cos_ref[...]
    sin = sin_ref[...]
    avg = avg_ref[...]
    group = _lane_group((TR, LANES))
    for p in range(RET_HEADS // PAIR):
        lanes = pl.ds(p * LANES, LANES)
        q = q_ref[0, :, lanes]
        k = k_ref[0, :, lanes]
        q = (q * cos + _swap_halves(q) * sin) * (HEAD_DIM ** -0.5)
        k = k * cos + _swap_halves(k) * sin
        v = v_ref[0, :, lanes]
        kb = k.astype(BF16)
        intra = jnp.zeros((TR, LANES), F32)
        for j in range(PAIR):
            qh = jnp.where(group == j, q, 0.0).astype(BF16)
            scores = _dot_nt(qh, kb) * intra_ref[PAIR * p + j]
            oh = _dot(scores.astype(BF16), v)
            intra = jnp.where(group == j, oh, intra)
        state = state_ref[p]
        cross = _dot((q * qdec_ref[:, lanes]).astype(BF16), state.astype(BF16))
        o = intra + cross
        kd_t = (k * kdec_ref[:, lanes]).T.astype(BF16)
        state_ref[p] = state * sdec_ref[p] + _dot(kd_t, v) * same_ref[...]
        mu = _group_mean(o, avg)
        d = o - mu
        var = _group_mean(d * d, avg)
        normed = d * lax.rsqrt(var + HEAD_NORM_EPS)
        gate = g_ref[0, :, lanes]
        o_ref[0, :, lanes] = (gate * jax.nn.sigmoid(gate) * normed).astype(o_ref.dtype)


def _retention(rq, rk, rv, rg, batch, seq):
    cos, sin = _rotary_tables(seq)
    intra, qdec, kdec, sdec, same = _retention_tables()
    avg = jnp.asarray(_group_avg(), BF16)
    r3 = lambda a: a.reshape(batch, seq, D_HEADS_PAD)
    tok = pl.BlockSpec((1, TR, D_HEADS_PAD), lambda b, n: (b, n, 0))
    const = lambda a: pl.BlockSpec(a.shape, lambda b, n: (0,) * a.ndim)
    pos = pl.BlockSpec((TR, LANES), lambda b, n: (n, 0))
    out = pl.pallas_call(
        _retention_kernel,
        grid=(batch, seq // TR),
        in_specs=[tok, tok, tok, tok, pos, pos, const(intra), const(qdec), const(kdec),
                  const(sdec), const(same), const(avg)],
        out_specs=tok,
        out_shape=jax.ShapeDtypeStruct((batch, seq, D_HEADS_PAD), BF16),
        scratch_shapes=[pltpu.VMEM((RET_HEADS // PAIR, LANES, LANES), F32)],
        compiler_params=_cparams(2),
        name="retention",
    )(r3(rq), r3(rk), r3(rv), r3(rg), cos, sin, intra, qdec, kdec, sdec, same, avg)
    return out.reshape(batch * seq, D_HEADS_PAD)


def _build_band(bidx_ref, tab_ref, band_ref, n_heads):
    for h in range(n_heads):
        far = tab_ref[h, FAR_BUCKET]
        for t in range(2):
            idx = bidx_ref[t]
            val = jnp.full((TK, TQ), NEG, F32)
            for bkt in range(N_BUCKETS):
                val = jnp.where(idx == bkt, (tab_ref[h, bkt] - far) * LOG2E, val)
            band_ref[h, t] = val


def _load_queries(qt_ref, q_scr, streams):
    row = lax.broadcasted_iota(jnp.int32, (LANES, TQ), 0)
    for i, (pair, _, lo, hi) in enumerate(streams):
        q_scr[i] = jnp.where((row >= lo) & (row < hi), qt_ref[pl.ds(pair * LANES, LANES), :], 0)


def _logits_pass(qb, k_ref, q_scr, band_ref, logit_ref, m_ref, streams, scale, mask_fn):
    m_ref[...] = jnp.full(m_ref.shape, NEG, F32)

    def tile(kt, band_t):
        rows = pl.ds(pl.multiple_of(kt * TK, TK), TK)
        sel = mask_fn(kt)
        for i, (pair, head, _, _) in enumerate(streams):
            lg = _dot(k_ref[0, rows, pl.ds(pair * LANES, LANES)], q_scr[i]) * (scale * LOG2E)
            if band_t is not None:
                lg = lg + band_ref[head, band_t]
            if sel is not None:
                lg = jnp.where(sel, lg, NEG)
            logit_ref[kt, i] = lg
            m_ref[i] = jnp.maximum(m_ref[i], jnp.max(lg.reshape(TK // SUBLANES, SUBLANES, TQ), axis=0))

    def far_tile(kt, carry):
        tile(kt, None)
        return carry

    lax.fori_loop(0, jnp.maximum(qb - 1, 0), far_tile, 0)

    @pl.when(qb >= 1)
    def _():
        tile(qb - 1, 0)

    tile(qb, 1)


def _values_pass(qb, vt_ref, logit_ref, m_ref, acc_ref, streams):
    m_fin = [jnp.max(m_ref[i], axis=0, keepdims=True) for i in range(len(streams))]
    acc_ref[...] = jnp.zeros(acc_ref.shape, F32)

    def tile(kt, carry):
        for i, (_, head, _, _) in enumerate(streams):
            p = jnp.exp2(logit_ref[kt, i] - m_fin[i])
            acc_ref[i] += _dot(vt_ref[0, kt, pl.ds(head * VT_ROWS, VT_ROWS), :], p.astype(BF16))
        return carry

    lax.fori_loop(0, qb + 1, tile, 0)


def _normalised(acc_ref, i):
    a = acc_ref[i]
    return a[:HEAD_DIM] / a[HEAD_DIM:HEAD_DIM + 1]


_DSA_STREAMS = tuple((h // PAIR, h, (h % PAIR) * HEAD_DIM, (h % PAIR + 1) * HEAD_DIM) for h in range(DSA_HEADS))


def _key_to_f32(key):
    bits = jnp.where(key < 0, key ^ jnp.int32(-2 ** 31), ~key)
    return lax.bitcast_convert_type(bits, F32)


def _dsa_kernel(tab_ref, qt_ref, k_ref, vt_ref, iqt_ref, ik_ref, iwt_ref, bidx_ref, tril_ref,
                o_ref, band_ref, score_ref, q_scr, logit_ref, m_ref, acc_ref, tie_ref):
    qb = pl.program_id(1)
    n_tiles = qb + 1

    @pl.when((pl.program_id(0) == 0) & (qb == 0))
    def _():
        _build_band(bidx_ref, tab_ref, band_ref, DSA_HEADS)

    iw = iwt_ref[...] * (IDX_HEADS ** -0.5 * IDX_DIM ** -0.5)
    row = lax.broadcasted_iota(jnp.int32, (LANES, TQ), 0)
    iq = [jnp.where(row // HEAD_DIM == (h % PAIR), iqt_ref[pl.ds((h // PAIR) * LANES, LANES), :], 0)
          for h in range(IDX_HEADS)]

    def score_tile(kt):
        ki = ik_ref[0, pl.ds(pl.multiple_of(kt * TK, TK), TK), :]
        s = jnp.zeros((TK, TQ), F32)
        for h in range(IDX_HEADS):
            s = s + jnp.maximum(_dot(ki, iq[h]), 0.0) * iw[h:h + 1, :]
        return s

    def past_scores(kt, carry):
        score_ref[kt] = score_tile(kt)
        return carry

    lax.fori_loop(0, qb, past_scores, 0)
    score_ref[qb] = jnp.where(band_ref[0, 1] > 0.5 * NEG, score_tile(qb), -jnp.inf)

    k_sel = float(DSA_TOPK_MAX)

    def count(pred):
        def body(kt, c):
            xs = score_ref[kt].reshape(TK // SUBLANES, SUBLANES, TQ)
            accs = [c[a] for a in range(N_ACC)]
            for g in range(TK // SUBLANES):
                a = accs[g % N_ACC]
                accs[g % N_ACC] = jnp.where(pred(xs[g]), a + 1.0, a)
            return jnp.stack(accs)
        c = lax.fori_loop(0, n_tiles, body, jnp.zeros((N_ACC, SUBLANES, TQ), F32))
        return jnp.sum(jnp.sum(c, axis=0), axis=0, keepdims=True)

    def bisect(i, prefix):
        cand = prefix | lax.shift_left(jnp.int32(1), 31 - i)
        cand_f = _key_to_f32(cand)
        take = count(lambda s: s >= cand_f) >= k_sel
        return jnp.where(take, cand, prefix)

    prefix = lax.fori_loop(0, 32, bisect, jnp.zeros((1, TQ), jnp.int32))
    col = lax.broadcasted_iota(jnp.int32, (1, TQ), 1)
    limit = qb * TQ + (col // CHUNK + 1) * CHUNK
    select_all = limit <= DSA_TOPK_MAX
    thr = jnp.where(select_all, -jnp.inf, _key_to_f32(prefix))
    n_gt = count(lambda s: s > thr)
    need = jnp.where(select_all, 0.0, k_sel - n_gt)

    tril = tril_ref[...]
    tie_ref[...] = jnp.zeros(tie_ref.shape, F32)

    def selected(kt):
        s = score_ref[kt]
        tied = s == thr
        tied_upto = tie_ref[...] + _dot(tril, jnp.where(tied, 1.0, 0.0).astype(BF16))
        tie_ref[...] = tied_upto[TK - 1:TK, :]
        return (s > thr) | (tied & (tied_upto <= need))

    _load_queries(qt_ref, q_scr, _DSA_STREAMS)
    _logits_pass(qb, k_ref, q_scr, band_ref, logit_ref, m_ref, _DSA_STREAMS, HEAD_DIM ** -0.5, selected)
    _values_pass(qb, vt_ref, logit_ref, m_ref, acc_ref, _DSA_STREAMS)

    for p in range(N_PAIRS):
        halves = [_normalised(acc_ref, PAIR * p + j) if PAIR * p + j < DSA_HEADS
                  else jnp.zeros((HEAD_DIM, TQ), F32) for j in range(PAIR)]
        o_ref[0, :, pl.ds(p * LANES, LANES)] = jnp.concatenate(halves, axis=0).T.astype(o_ref.dtype)


def _dsa(aqT, ak, avT, iqT, ik, iwT, bias_t, batch, seq):
    bidx = jnp.asarray(_band_bucket_index_t())
    tril = jnp.asarray(_prefix_tril(), BF16)
    nq = seq // TQ
    n_streams = len(_DSA_STREAMS)
    qcols = lambda w: pl.BlockSpec((w, TQ), lambda b, n: (0, b * nq + n))
    ktok = lambda w: pl.BlockSpec((1, seq, w), lambda b, n: (b, 0, 0))
    const = lambda a: pl.BlockSpec(a.shape, lambda b, n: (0,) * a.ndim)
    out = pl.pallas_call(
        _dsa_kernel,
        grid=(batch, nq),
        in_specs=[pl.BlockSpec(memory_space=pltpu.SMEM),
                  qcols(D_HEADS_PAD), ktok(D_HEADS_PAD),
                  pl.BlockSpec((1, seq // TK, DSA_HEADS * VT_ROWS, TK), lambda b, n: (b, 0, 0, 0)),
                  qcols(IDX_HEADS * IDX_DIM), ktok(LANES), qcols(_IW_ROWS), const(bidx), const(tril)],
        out_specs=pl.BlockSpec((1, TQ, D_HEADS_PAD), lambda b, n: (b, n, 0)),
        out_shape=jax.ShapeDtypeStruct((batch, seq, D_HEADS_PAD), BF16),
        scratch_shapes=[pltpu.VMEM((DSA_HEADS, 2, TK, TQ), F32),
                        pltpu.VMEM((seq // TK, TK, TQ), F32),
                        pltpu.VMEM((n_streams, LANES, TQ), BF16),
                        pltpu.VMEM((seq // TK, n_streams, TK, TQ), F32),
                        pltpu.VMEM((n_streams, SUBLANES, TQ), F32),
                        pltpu.VMEM((n_streams, VT_ROWS, TQ), F32),
                        pltpu.VMEM((1, TQ), F32)],
        compiler_params=_cparams(2),
        name="dsa_attention",
    )(bias_t, aqT, ak.reshape(batch, seq, D_HEADS_PAD),
      avT.reshape(batch, seq // TK, DSA_HEADS * VT_ROWS, TK), iqT, ik.reshape(batch, seq, LANES), iwT, bidx, tril)
    return out.reshape(batch * seq, D_HEADS_PAD)


_DIFF_STREAMS = tuple(
    (h // PAIR, h, (h % PAIR) * HEAD_DIM + mm * DIFF_QK_DIM, (h % PAIR) * HEAD_DIM + (mm + 1) * DIFF_QK_DIM)
    for h in range(DIFF_HEADS) for mm in range(2))


def _diff_kernel(lambda_init, tab_ref, qt_ref, k_ref, vt_ref, lam_ref, g_ref, bidx_ref, avg_ref,
                 o_ref, band_ref, q_scr, logit_ref, m_ref, acc_ref):
    qb = pl.program_id(1)

    @pl.when((pl.program_id(0) == 0) & (qb == 0))
    def _():
        _build_band(bidx_ref, tab_ref, band_ref, DIFF_HEADS)

    lv = lam_ref[...]
    lam = (jnp.exp(jnp.sum(lv[0:1] * lv[1:2], axis=-1, keepdims=True))
           - jnp.exp(jnp.sum(lv[2:3] * lv[3:4], axis=-1, keepdims=True)) + lambda_init)

    _load_queries(qt_ref, q_scr, _DIFF_STREAMS)
    _logits_pass(qb, k_ref, q_scr, band_ref, logit_ref, m_ref, _DIFF_STREAMS, DIFF_QK_DIM ** -0.5,
                 lambda kt: None)
    _values_pass(qb, vt_ref, logit_ref, m_ref, acc_ref, _DIFF_STREAMS)

    avg = avg_ref[...]
    for p in range(N_PAIRS):
        halves = []
        for j in range(PAIR):
            h = PAIR * p + j
            if h < DIFF_HEADS:
                halves.append(_normalised(acc_ref, 2 * h) - lam * _normalised(acc_ref, 2 * h + 1))
            else:
                halves.append(jnp.zeros((HEAD_DIM, TQ), F32))
        out = jnp.concatenate(halves, axis=0).T
        ms = _group_mean(out * out, avg)
        out = out * lax.rsqrt(ms + LN_EPS) * g_ref[...] * (1.0 - lambda_init)
        o_ref[0, :, pl.ds(p * LANES, LANES)] = out.astype(o_ref.dtype)


def _diff(dqT, dk, dvT, lam_vecs, subln_g, bias_t, lambda_init, batch, seq):
    bidx = jnp.asarray(_band_bucket_index_t())
    avg = jnp.asarray(_group_avg(), BF16)
    g2 = jnp.concatenate([subln_g, subln_g]).reshape(1, LANES)
    nq = seq // TQ
    n_streams = len(_DIFF_STREAMS)
    const = lambda a: pl.BlockSpec(a.shape, lambda b, n: (0,) * a.ndim)
    out = pl.pallas_call(
        functools.partial(_diff_kernel, lambda_init),
        grid=(batch, nq),
        in_specs=[pl.BlockSpec(memory_space=pltpu.SMEM),
                  pl.BlockSpec((D_HEADS_PAD, TQ), lambda b, n: (0, b * nq + n)),
                  pl.BlockSpec((1, seq, D_HEADS_PAD), lambda b, n: (b, 0, 0)),
                  pl.BlockSpec((1, seq // TK, DIFF_HEADS * VT_ROWS, TK), lambda b, n: (b, 0, 0, 0)),
                  const(lam_vecs), const(g2), const(bidx), const(avg)],
        out_specs=pl.BlockSpec((1, TQ, D_HEADS_PAD), lambda b, n: (b, n, 0)),
        out_shape=jax.ShapeDtypeStruct((batch, seq, D_HEADS_PAD), BF16),
        scratch_shapes=[pltpu.VMEM((DIFF_HEADS, 2, TK, TQ), F32),
                        pltpu.VMEM((n_streams, LANES, TQ), BF16),
                        pltpu.VMEM((seq // TK, n_streams, TK, TQ), F32),
                        pltpu.VMEM((n_streams, SUBLANES, TQ), F32),
                        pltpu.VMEM((n_streams, VT_ROWS, TQ), F32)],
        compiler_params=_cparams(2),
        name="diff_attention",
    )(bias_t, dqT, dk.reshape(batch, seq, D_HEADS_PAD),
      dvT.reshape(batch, seq // TK, DIFF_HEADS * VT_ROWS, TK), lam_vecs, g2, bidx, avg)
    return out.reshape(batch * seq, D_HEADS_PAD)


def _out_ln_kernel(x_ref, ret_ref, dsa_ref, dif_ref, w_ref, g_ref, b_ref, o_ref):
    acc = _dot(ret_ref[...], w_ref[pl.ds(0, D_HEADS_PAD), :])
    acc = acc + _dot(dsa_ref[...], w_ref[pl.ds(D_HEADS_PAD, D_HEADS_PAD), :])
    acc = acc + _dot(dif_ref[...], w_ref[pl.ds(2 * D_HEADS_PAD, D_HEADS_PAD), :])
    r = DEEPNORM_ALPHA * x_ref[...] + acc
    o_ref[...] = _layer_norm(r, g_ref[...], b_ref[...])


def _pad_w_out(w_out):
    zeros = jnp.zeros((D_HEADS_PAD - D_DSA, w_out.shape[1]), w_out.dtype)
    return jnp.concatenate([w_out[:D_RET], w_out[D_RET:D_RET + D_DSA], zeros,
                            w_out[D_RET + D_DSA:], zeros], axis=0)


def _out_ln(x, ret, dsa, dif, w_pad, g, b):
    t = x.shape[0]
    full = lambda shape: pl.BlockSpec(shape, lambda i: (0, 0))
    tok = lambda w: pl.BlockSpec((TM_OUT, w), lambda i: (i, 0))
    return pl.pallas_call(
        _out_ln_kernel,
        grid=(t // TM_OUT,),
        in_specs=[tok(D_MODEL), tok(D_HEADS_PAD), tok(D_HEADS_PAD), tok(D_HEADS_PAD),
                  full((3 * D_HEADS_PAD, D_MODEL)), full((1, D_MODEL)), full((1, D_MODEL))],
        out_specs=tok(D_MODEL),
        out_shape=jax.ShapeDtypeStruct((t, D_MODEL), F32),
        compiler_params=_cparams(1),
        name="out_proj_ln",
    )(x, ret, dsa, dif, w_pad, g, b)


def kernel(x, w_in, w_out, ffn1_wg, ffn1_wu, ffn1_wd, ffn2_wg, ffn2_wu, ffn2_wd,
           ln_g, ln_b, diff_lambda, diff_subln_g, rel_bias):
    batch, seq, d = x.shape
    assert d == D_MODEL and seq % TQ == 0 and seq % TR == 0 and (batch * seq) % TM_FFN == 0
    assert min(DSA_TOPK_MAX, seq // 4) == DSA_TOPK_MAX
    h = x.reshape(batch * seq, d)
    bias_t = rel_bias.T
    row = lambda a: a.reshape(1, D_MODEL)
    for l in range(DEPTH):
        lambda_init = 0.8 - 0.6 * math.exp(-0.3 * l)
        h = _ffn_ln(h, ffn1_wg[l].astype(BF16), ffn1_wu[l].astype(BF16), ffn1_wd[l].astype(BF16),
                    row(ln_g[l, 0]), row(ln_b[l, 0]))
        w_tok, w_t = _split_w_in(w_in[l])
        (rq, rk, rv, rg, ak, dk, ik, aqT, dqT, iqT, iwT, avT, dvT) = _proj(
            h, w_tok.astype(BF16), w_t.astype(BF16))
        ret = _retention(rq, rk, rv, rg, batch, seq)
        dsa = _dsa(aqT, ak, avT, iqT, ik, iwT, bias_t[:DSA_HEADS], batch, seq)
        dif = _diff(dqT, dk, dvT, diff_lambda[l], diff_subln_g[l], bias_t[DSA_HEADS:], lambda_init, batch, seq)
        h = _out_ln(h, ret, dsa, dif, _pad_w_out(w_out[l]).astype(BF16), row(ln_g[l, 1]), row(ln_b[l, 1]))
        h = _ffn_ln(h, ffn2_wg[l].astype(BF16), ffn2_wu[l].astype(BF16), ffn2_wd[l].astype(BF16),
                    row(ln_g[l, 2]), row(ln_b[l, 2]))
    return h.reshape(batch, seq, d)
```

```python
import functools
import math

import numpy as np
import jax
import jax.numpy as jnp
from jax import lax
from jax.experimental import pallas as pl
from jax.experimental.pallas import tpu as pltpu

D_MODEL = 1024
DEPTH = 2
CHUNK = 64
HEAD_DIM = 64
RET_HEADS = 6
DSA_HEADS = 5
IDX_HEADS = 4
IDX_DIM = 64
DSA_TOPK_MAX = 256
DIFF_HEADS = 5
DIFF_QK_DIM = HEAD_DIM // 2
D_RET = RET_HEADS * HEAD_DIM
D_DSA = DSA_HEADS * HEAD_DIM
D_DIFF = DIFF_HEADS * HEAD_DIM
D_FF = 2816
N_BUCKETS = 32
MAX_DISTANCE = 128
ROPE_BASE = 10000.0
LN_EPS = 1e-5
HEAD_NORM_EPS = 1e-6
DEEPNORM_ALPHA = (2 * DEPTH) ** 0.25

LANES = 128
SUBLANES = 8
BF16_ROWS = 16
PAIR = LANES // HEAD_DIM
VMEM_LIMIT_BYTES = 56 * 1024 * 1024

D_HEADS_PAD = 3 * LANES
N_PAIRS = D_HEADS_PAD // LANES
VT_ROWS = HEAD_DIM + BF16_ROWS
TQ = 256
TK = 256
TR = 256
TM_FFN = 512
TF = 256
TM_PROJ = 512
TM_OUT = 512
N_ACC = 4
NEG = -1e30
LOG2E = math.log2(math.e)

F32 = jnp.float32
BF16 = jnp.bfloat16


def _dot(a, b):
    return jnp.dot(a, b, preferred_element_type=F32)


def _dot_nt(a, b):
    return lax.dot_general(a, b, (((1,), (1,)), ((), ())), preferred_element_type=F32)


def _layer_norm(r, g, b):
    mu = jnp.mean(r, axis=-1, keepdims=True)
    d = r - mu
    var = jnp.mean(d * d, axis=-1, keepdims=True)
    return d * lax.rsqrt(var + LN_EPS) * g + b


def _group_mean(x, avg_bf16):
    hi = x.astype(BF16)
    lo = (x - hi.astype(F32)).astype(BF16)
    return _dot(hi, avg_bf16) + _dot(lo, avg_bf16)


def _cparams(n_axes):
    return pltpu.CompilerParams(
        dimension_semantics=("arbitrary",) * n_axes,
        vmem_limit_bytes=VMEM_LIMIT_BYTES,
    )


def _rel_bucket_static(rel):
    rel = np.asarray(rel, dtype=np.int64)
    half = N_BUCKETS // 2
    max_exact = half // 2
    n = np.abs(rel)
    large = np.full(n.shape, max_exact, dtype=np.int64)
    for k in range(1, 64):
        large = np.where(64 * (2 ** k) <= n * n, max_exact + k, large)
    large = np.minimum(large, half - 1)
    return (np.where(rel > 0, half, 0) + np.where(n < max_exact, n, large)).astype(np.int32)


FAR_BUCKET = int(_rel_bucket_static(-(TK + 1)))


@functools.lru_cache(maxsize=None)
def _band_bucket_index_t():
    j = np.arange(TK)[:, None]
    i = np.arange(TQ)[None, :]
    prev = _rel_bucket_static(j - TK - i)
    diag = _rel_bucket_static(j - i)
    visible = j < (i // CHUNK + 1) * CHUNK
    diag = np.where(visible, diag, -1)
    return np.stack([prev, diag]).astype(np.int32)


@functools.lru_cache(maxsize=None)
def _rotary_tables(seq):
    lane = np.arange(LANES)
    d = lane % HEAD_DIM
    f = d % (HEAD_DIM // 2)
    inv = ROPE_BASE ** (-(2.0 * f) / HEAD_DIM)
    ang = np.arange(seq, dtype=np.float64)[:, None] * inv[None, :]
    cos = np.cos(ang)
    sin = np.sin(ang) * np.where(d < HEAD_DIM // 2, -1.0, 1.0)[None, :]
    return cos.astype(np.float32), sin.astype(np.float32)


@functools.lru_cache(maxsize=None)
def _retention_tables():
    gamma = 1.0 - 2.0 ** (-5.0 - np.arange(RET_HEADS, dtype=np.float64))
    t = np.arange(TR)
    dist = np.abs(t[:, None] - t[None, :]).astype(np.float64)
    visible = t[None, :] < (t[:, None] // CHUNK + 1) * CHUNK
    intra = np.stack([np.where(visible, g ** dist, 0.0) for g in gamma])
    lane_head = np.arange(D_HEADS_PAD) // HEAD_DIM
    g_lane = gamma[lane_head]
    qdec = g_lane[None, :] ** (t[:, None] + 1.0)
    kdec = g_lane[None, :] ** (TR - 1.0 - t[:, None])
    a = np.arange(LANES)
    same = (a[:, None] // HEAD_DIM) == (a[None, :] // HEAD_DIM)
    sdec = np.stack([np.where(same, (gamma[PAIR * p + a // HEAD_DIM] ** TR)[:, None], 0.0)
                     for p in range(RET_HEADS // PAIR)])
    return (intra.astype(np.float32), qdec.astype(np.float32), kdec.astype(np.float32),
            sdec.astype(np.float32), same.astype(np.float32))


@functools.lru_cache(maxsize=None)
def _group_avg():
    a = np.arange(LANES)
    same = (a[:, None] // HEAD_DIM) == (a[None, :] // HEAD_DIM)
    return np.where(same, 1.0 / HEAD_DIM, 0.0).astype(np.float32)


@functools.lru_cache(maxsize=None)
def _prefix_tril():
    a = np.arange(TK)
    return (a[None, :] <= a[:, None]).astype(np.float32)


def _lane_group(shape):
    return lax.broadcasted_iota(jnp.int32, shape, len(shape) - 1) // HEAD_DIM


def _ffn_ln_kernel(x_ref, wg_ref, wu_ref, wd_ref, g_ref, b_ref, o_ref):
    x = x_ref[...]
    xb = x.astype(BF16)
    acc = jnp.zeros(x.shape, F32)
    for c in range(D_FF // TF):
        cols = pl.ds(c * TF, TF)
        gate = _dot(xb, wg_ref[:, cols])
        up = _dot(xb, wu_ref[:, cols])
        h = gate * jax.nn.sigmoid(gate) * up
        acc = acc + _dot(h.astype(BF16), wd_ref[cols, :])
    r = DEEPNORM_ALPHA * x + 0.5 * acc
    o_ref[...] = _layer_norm(r, g_ref[...], b_ref[...])


def _ffn_ln(x, wg, wu, wd, g, b):
    t = x.shape[0]
    full = lambda shape: pl.BlockSpec(shape, lambda i: (0, 0))
    return pl.pallas_call(
        _ffn_ln_kernel,
        grid=(t // TM_FFN,),
        in_specs=[pl.BlockSpec((TM_FFN, D_MODEL), lambda i: (i, 0)),
                  full((D_MODEL, D_FF)), full((D_MODEL, D_FF)), full((D_FF, D_MODEL)),
                  full((1, D_MODEL)), full((1, D_MODEL))],
        out_specs=pl.BlockSpec((TM_FFN, D_MODEL), lambda i: (i, 0)),
        out_shape=jax.ShapeDtypeStruct((t, D_MODEL), F32),
        compiler_params=_cparams(1),
        name="ffn_ln",
    )(x, wg, wu, wd, g, b)


_IW_ROWS = BF16_ROWS
_TOK_SEGMENTS = (("rq", D_HEADS_PAD, F32), ("rk", D_HEADS_PAD, F32), ("rv", D_HEADS_PAD, BF16),
                 ("rg", D_HEADS_PAD, F32), ("ak", D_HEADS_PAD, BF16), ("dk", D_HEADS_PAD, BF16),
                 ("ik", LANES, BF16))
_T_SEGMENTS = (("aqT", D_HEADS_PAD, BF16, HEAD_DIM ** -0.5 * LOG2E),
               ("dqT", D_HEADS_PAD, BF16, DIFF_QK_DIM ** -0.5 * LOG2E),
               ("iqT", IDX_HEADS * IDX_DIM, BF16, None), ("iwT", _IW_ROWS, F32, None))
_VT_SEGMENTS = (("avT", DSA_HEADS), ("dvT", DIFF_HEADS))
N_TOK = sum(w for _, w, _ in _TOK_SEGMENTS)
N_T = sum(s[1] for s in _T_SEGMENTS) + sum(n * HEAD_DIM for _, n in _VT_SEGMENTS)


def _split_w_in(w_in):
    splits = (D_RET, D_RET, D_RET, D_RET, D_DSA, D_DSA, D_DSA,
              IDX_HEADS * IDX_DIM, IDX_DIM, IDX_HEADS, D_DIFF, D_DIFF, D_DIFF)
    offs = np.cumsum((0,) + splits)
    (rq, rk, rv, rg, aq, ak, av, iq, ik, iw, dq, dk, dv) = [
        w_in[:, offs[i]:offs[i + 1]] for i in range(len(splits))]
    zeros = lambda n: jnp.zeros((w_in.shape[0], n), w_in.dtype)
    pad = lambda a, n: jnp.concatenate([a, zeros(n - a.shape[1])], axis=1)
    w_tok = jnp.concatenate([rq, rk, rv, rg, pad(ak, D_HEADS_PAD), pad(dk, D_HEADS_PAD),
                             ik, ik], axis=1)
    w_t = jnp.concatenate([pad(aq, D_HEADS_PAD), pad(dq, D_HEADS_PAD), iq, pad(iw, _IW_ROWS),
                           av, dv], axis=1).T
    return w_tok, w_t


def _proj_kernel(x_ref, wtok_ref, wt_ref, *o_refs):
    xb = x_ref[...].astype(BF16)
    refs = list(o_refs)
    off = 0
    for _, width, dtype in _TOK_SEGMENTS:
        refs.pop(0)[...] = _dot(xb, wtok_ref[:, pl.ds(off, width)]).astype(dtype)
        off += width
    off = 0
    for _, width, dtype, scale in _T_SEGMENTS:
        res = _dot_nt(wt_ref[pl.ds(off, width), :], xb)
        refs.pop(0)[...] = (res if scale is None else res * scale).astype(dtype)
        off += width
    ones = jnp.ones((BF16_ROWS, TK), BF16)
    for _, n_heads in _VT_SEGMENTS:
        o_ref = refs.pop(0)
        res = _dot_nt(wt_ref[pl.ds(off, n_heads * HEAD_DIM), :], xb).astype(BF16)
        for t in range(TM_PROJ // TK):
            for h in range(n_heads):
                o_ref[t, pl.ds(h * VT_ROWS, HEAD_DIM), :] = res[h * HEAD_DIM:(h + 1) * HEAD_DIM,
                                                                t * TK:(t + 1) * TK]
                o_ref[t, pl.ds(h * VT_ROWS + HEAD_DIM, BF16_ROWS), :] = ones
        off += n_heads * HEAD_DIM


def _proj(x, w_tok, w_t):
    t = x.shape[0]
    out_specs = ([pl.BlockSpec((TM_PROJ, w), lambda i: (i, 0)) for _, w, _ in _TOK_SEGMENTS]
                 + [pl.BlockSpec((s[1], TM_PROJ), lambda i: (0, i)) for s in _T_SEGMENTS]
                 + [pl.BlockSpec((TM_PROJ // TK, n * VT_ROWS, TK), lambda i: (i, 0, 0)) for _, n in _VT_SEGMENTS])
    out_shape = ([jax.ShapeDtypeStruct((t, w), dt) for _, w, dt in _TOK_SEGMENTS]
                 + [jax.ShapeDtypeStruct((s[1], t), s[2]) for s in _T_SEGMENTS]
                 + [jax.ShapeDtypeStruct((t // TK, n * VT_ROWS, TK), BF16) for _, n in _VT_SEGMENTS])
    return pl.pallas_call(
        _proj_kernel,
        grid=(t // TM_PROJ,),
        in_specs=[pl.BlockSpec((TM_PROJ, D_MODEL), lambda i: (i, 0)),
                  pl.BlockSpec((D_MODEL, N_TOK), lambda i: (0, 0)),
                  pl.BlockSpec((N_T, D_MODEL), lambda i: (0, 0))],
        out_specs=out_specs,
        out_shape=out_shape,
        compiler_params=_cparams(1),
        name="in_proj",
    )(x, w_tok, w_t)


def _swap_halves(x):
    lane = lax.broadcasted_iota(jnp.int32, x.shape, 1)
    first_half = (lane % HEAD_DIM) < (HEAD_DIM // 2)
    from_right = pltpu.roll(x, LANES - HEAD_DIM // 2, 1)
    from_left = pltpu.roll(x, HEAD_DIM // 2, 1)
    return jnp.where(first_half, from_right, from_left)


def _retention_kernel(q_ref, k_ref, v_ref, g_ref, cos_ref, sin_ref, intra_ref, qdec_ref, kdec_ref,
                      sdec_ref, same_ref, avg_ref, o_ref, state_ref):
    @pl.when(pl.program_id(1) == 0)
    def _():
        state_ref[...] = jnp.zeros(state_ref.shape, F32)

    cos = cos_ref[...]
    sin = sin_ref[...]
    avg = avg_ref[...]
    group = _lane_group((TR, LANES))
    for p in range(RET_HEADS // PAIR):
        lanes = pl.ds(p * LANES, LANES)
        q = q_ref[0, :, lanes]
        k = k_ref[0, :, lanes]
        q = (q * cos + _swap_halves(q) * sin) * (HEAD_DIM ** -0.5)
        k = k * cos + _swap_halves(k) * sin
        v = v_ref[0, :, lanes]
        kb = k.astype(BF16)
        intra = jnp.zeros((TR, LANES), F32)
        for j in range(PAIR):
            qh = jnp.where(group == j, q, 0.0).astype(BF16)
            scores = _dot_nt(qh, kb) * intra_ref[PAIR * p + j]
            oh = _dot(scores.astype(BF16), v)
            intra = jnp.where(group == j, oh, intra)
        state = state_ref[p]
        cross = _dot((q * qdec_ref[:, lanes]).astype(BF16), state.astype(BF16))
        o = intra + cross
        kd_t = (k * kdec_ref[:, lanes]).T.astype(BF16)
        state_ref[p] = state * sdec_ref[p] + _dot(kd_t, v) * same_ref[...]
        mu = _group_mean(o, avg)
        d = o - mu
        var = _group_mean(d * d, avg)
        normed = d * lax.rsqrt(var + HEAD_NORM_EPS)
        gate = g_ref[0, :, lanes]
        o_ref[0, :, lanes] = (gate * jax.nn.sigmoid(gate) * normed).astype(o_ref.dtype)


def _retention(rq, rk, rv, rg, batch, seq):
    cos, sin = _rotary_tables(seq)
    intra, qdec, kdec, sdec, same = _retention_tables()
    avg = jnp.asarray(_group_avg(), BF16)
    r3 = lambda a: a.reshape(batch, seq, D_HEADS_PAD)
    tok = pl.BlockSpec((1, TR, D_HEADS_PAD), lambda b, n: (b, n, 0))
    const = lambda a: pl.BlockSpec(a.shape, lambda b, n: (0,) * a.ndim)
    pos = pl.BlockSpec((TR, LANES), lambda b, n: (n, 0))
    out = pl.pallas_call(
        _retention_kernel,
        grid=(batch, seq // TR),
        in_specs=[tok, tok, tok, tok, pos, pos, const(intra), const(qdec), const(kdec),
                  const(sdec), const(same), const(avg)],
        out_specs=tok,
        out_shape=jax.ShapeDtypeStruct((batch, seq, D_HEADS_PAD), BF16),
        scratch_shapes=[pltpu.VMEM((RET_HEADS // PAIR, LANES, LANES), F32)],
        compiler_params=_cparams(2),
        name="retention",
    )(r3(rq), r3(rk), r3(rv), r3(rg), cos, sin, intra, qdec, kdec, sdec, same, avg)
    return out.reshape(batch * seq, D_HEADS_PAD)


def _build_band(bidx_ref, tab_ref, band_ref, n_heads):
    for h in range(n_heads):
        far = tab_ref[h, FAR_BUCKET]
        for t in range(2):
            idx = bidx_ref[t]
            val = jnp.full((TK, TQ), NEG, F32)
            for bkt in range(N_BUCKETS):
                val = jnp.where(idx == bkt, (tab_ref[h, bkt] - far) * LOG2E, val)
            band_ref[h, t] = val


def _load_queries(qt_ref, q_scr, streams):
    row = lax.broadcasted_iota(jnp.int32, (LANES, TQ), 0)
    for i, (pair, _, lo, hi) in enumerate(streams):
        q_scr[i] = jnp.where((row >= lo) & (row < hi), qt_ref[pl.ds(pair * LANES, LANES), :], 0)


def _for_tiles(n, fn):
    def pair(j, carry):
        fn(2 * j)
        fn(2 * j + 1)
        return carry

    lax.fori_loop(0, lax.shift_right_logical(n, 1), pair, 0)

    @pl.when(lax.rem(n, 2) == 1)
    def _():
        fn(n - 1)


def _softmax_passes(qb, k_ref, vt_ref, q_scr, band_ref, mask_fn, logit_ref, m_ref, acc_ref, streams):
    m_ref[...] = jnp.full(m_ref.shape, NEG, F32)

    def logits_tile(kt, band_t):
        rows = pl.ds(pl.multiple_of(kt * TK, TK), TK)
        mask = mask_fn(kt)
        for i, (pair, head, _, _) in enumerate(streams):
            lg = _dot(k_ref[0, rows, pl.ds(pair * LANES, LANES)], q_scr[i])
            if band_t is not None:
                lg = lg + band_ref[head, band_t]
            if mask is not None:
                lg = lg + mask
            logit_ref[kt, i] = lg
            m_ref[i] = jnp.maximum(m_ref[i], jnp.max(lg.reshape(TK // SUBLANES, SUBLANES, TQ), axis=0))

    _for_tiles(jnp.maximum(qb - 1, 0), lambda kt: logits_tile(kt, None))

    @pl.when(qb >= 1)
    def _():
        logits_tile(qb - 1, 0)
        logits_tile(qb, 1)

    @pl.when(qb == 0)
    def _():
        logits_tile(qb, 1)

    m_fin = [jnp.max(m_ref[i], axis=0, keepdims=True) for i in range(len(streams))]
    acc_ref[...] = jnp.zeros(acc_ref.shape, F32)

    def values_tile(kt):
        for i, (_, head, _, _) in enumerate(streams):
            p = jnp.exp2(logit_ref[kt, i] - m_fin[i])
            acc_ref[i] += _dot(vt_ref[0, kt, pl.ds(head * VT_ROWS, VT_ROWS), :], p.astype(BF16))

    _for_tiles(qb + 1, values_tile)


def _normalised(acc_ref, i):
    a = acc_ref[i]
    return a[:HEAD_DIM] / a[HEAD_DIM:HEAD_DIM + 1]


_DSA_STREAMS = tuple((h // PAIR, h, (h % PAIR) * HEAD_DIM, (h % PAIR + 1) * HEAD_DIM) for h in range(DSA_HEADS))


def _key_to_f32(key):
    bits = jnp.where(key < 0, key ^ jnp.int32(-2 ** 31), ~key)
    return lax.bitcast_convert_type(bits, F32)


def _dsa_kernel(tab_ref, qt_ref, k_ref, vt_ref, iqt_ref, ik_ref, iwt_ref, bidx_ref, tril_ref,
                o_ref, band_ref, score_ref, q_scr, logit_ref, m_ref, acc_ref, tiecnt_ref, tie_ref):
    qb = pl.program_id(1)
    n_tiles = qb + 1

    @pl.when((pl.program_id(0) == 0) & (qb == 0))
    def _():
        _build_band(bidx_ref, tab_ref, band_ref, DSA_HEADS)

    iw = iwt_ref[...] * (IDX_HEADS ** -0.5 * IDX_DIM ** -0.5)
    row = lax.broadcasted_iota(jnp.int32, (LANES, TQ), 0)
    iq = [jnp.where(row // HEAD_DIM == (h % PAIR), iqt_ref[pl.ds((h // PAIR) * LANES, LANES), :], 0)
          for h in range(IDX_HEADS)]

    def score_tile(kt):
        ki = ik_ref[0, pl.ds(pl.multiple_of(kt * TK, TK), TK), :]
        s = jnp.zeros((TK, TQ), F32)
        for h in range(IDX_HEADS):
            s = s + jnp.maximum(_dot(ki, iq[h]), 0.0) * iw[h:h + 1, :]
        return s

    def past_scores(kt, carry):
        score_ref[kt] = score_tile(kt)
        return carry

    lax.fori_loop(0, qb, past_scores, 0)
    score_ref[qb] = jnp.where(band_ref[0, 1] > 0.5 * NEG, score_tile(qb), -jnp.inf)

    k_sel = float(DSA_TOPK_MAX)

    def count(pred):
        def body(kt, c):
            xs = score_ref[kt].reshape(TK // SUBLANES, SUBLANES, TQ)
            accs = [c[a] for a in range(N_ACC)]
            for g in range(TK // SUBLANES):
                a = accs[g % N_ACC]
                accs[g % N_ACC] = jnp.where(pred(xs[g]), a + 1.0, a)
            return jnp.stack(accs)
        c = lax.fori_loop(0, n_tiles, body, jnp.zeros((N_ACC, SUBLANES, TQ), F32))
        return jnp.sum(jnp.sum(c, axis=0), axis=0, keepdims=True)

    def bisect(i, prefix):
        cand = prefix | lax.shift_left(jnp.int32(1), 31 - i)
        cand_f = _key_to_f32(cand)
        take = count(lambda s: s >= cand_f) >= k_sel
        return jnp.where(take, cand, prefix)

    prefix = lax.fori_loop(0, 32, bisect, jnp.zeros((1, TQ), jnp.int32))
    col = lax.broadcasted_iota(jnp.int32, (1, TQ), 1)
    limit = qb * TQ + (col // CHUNK + 1) * CHUNK
    select_all = limit <= DSA_TOPK_MAX
    thr = jnp.where(select_all, -jnp.inf, _key_to_f32(prefix))
    n_gt = count(lambda s: s > thr)
    need = jnp.where(select_all, 0.0, k_sel - n_gt)

    def tie_counts(kt, carry):
        tied = jnp.where(score_ref[kt] == thr, 1.0, 0.0).reshape(TK // SUBLANES, SUBLANES, TQ)
        tiecnt_ref[kt] = jnp.sum(jnp.sum(tied, axis=0), axis=0, keepdims=True)
        return carry

    lax.fori_loop(0, n_tiles, tie_counts, 0)
    tie_ref[...] = jnp.zeros(tie_ref.shape, F32)
    tril = tril_ref[...]

    def selection_mask(kt):
        s = score_ref[kt]
        tied = s == thr
        tied_before = tie_ref[...]
        tie_ref[...] = tied_before + tiecnt_ref[kt]
        tied_upto = tied_before + _dot(tril, jnp.where(tied, 1.0, 0.0).astype(BF16))
        return jnp.where((s > thr) | (tied & (tied_upto <= need)), 0.0, NEG)

    _load_queries(qt_ref, q_scr, _DSA_STREAMS)
    _softmax_passes(qb, k_ref, vt_ref, q_scr, band_ref, selection_mask, logit_ref, m_ref, acc_ref, _DSA_STREAMS)

    for p in range(N_PAIRS):
        halves = [_normalised(acc_ref, PAIR * p + j) if PAIR * p + j < DSA_HEADS
                  else jnp.zeros((HEAD_DIM, TQ), F32) for j in range(PAIR)]
        o_ref[0, :, pl.ds(p * LANES, LANES)] = jnp.concatenate(halves, axis=0).T.astype(o_ref.dtype)


def _dsa(aqT, ak, avT, iqT, ik, iwT, bias_t, batch, seq):
    bidx = jnp.asarray(_band_bucket_index_t())
    tril = jnp.asarray(_prefix_tril(), BF16)
    nq = seq // TQ
    n_streams = len(_DSA_STREAMS)
    qcols = lambda w: pl.BlockSpec((w, TQ), lambda b, n: (0, b * nq + n))
    ktok = lambda w: pl.BlockSpec((1, seq, w), lambda b, n: (b, 0, 0))
    const = lambda a: pl.BlockSpec(a.shape, lambda b, n: (0,) * a.ndim)
    out = pl.pallas_call(
        _dsa_kernel,
        grid=(batch, nq),
        in_specs=[pl.BlockSpec(memory_space=pltpu.SMEM),
                  qcols(D_HEADS_PAD), ktok(D_HEADS_PAD),
                  pl.BlockSpec((1, seq // TK, DSA_HEADS * VT_ROWS, TK), lambda b, n: (b, 0, 0, 0)),
                  qcols(IDX_HEADS * IDX_DIM), ktok(LANES), qcols(_IW_ROWS), const(bidx), const(tril)],
        out_specs=pl.BlockSpec((1, TQ, D_HEADS_PAD), lambda b, n: (b, n, 0)),
        out_shape=jax.ShapeDtypeStruct((batch, seq, D_HEADS_PAD), BF16),
        scratch_shapes=[pltpu.VMEM((DSA_HEADS, 2, TK, TQ), F32),
                        pltpu.VMEM((seq // TK, TK, TQ), F32),
                        pltpu.VMEM((n_streams, LANES, TQ), BF16),
                        pltpu.VMEM((seq // TK, n_streams, TK, TQ), F32),
                        pltpu.VMEM((n_streams, SUBLANES, TQ), F32),
                        pltpu.VMEM((n_streams, VT_ROWS, TQ), F32),
                        pltpu.VMEM((seq // TK, 1, TQ), F32),
                        pltpu.VMEM((1, TQ), F32)],
        compiler_params=_cparams(2),
        name="dsa_attention",
    )(bias_t, aqT, ak.reshape(batch, seq, D_HEADS_PAD),
      avT.reshape(batch, seq // TK, DSA_HEADS * VT_ROWS, TK), iqT, ik.reshape(batch, seq, LANES), iwT, bidx, tril)
    return out.reshape(batch * seq, D_HEADS_PAD)


_DIFF_STREAMS = tuple(
    (h // PAIR, h, (h % PAIR) * HEAD_DIM + mm * DIFF_QK_DIM, (h % PAIR) * HEAD_DIM + (mm + 1) * DIFF_QK_DIM)
    for h in range(DIFF_HEADS) for mm in range(2))


def _diff_kernel(lambda_init, tab_ref, qt_ref, k_ref, vt_ref, lam_ref, g_ref, bidx_ref, avg_ref,
                 o_ref, band_ref, q_scr, logit_ref, m_ref, acc_ref):
    qb = pl.program_id(1)

    @pl.when((pl.program_id(0) == 0) & (qb == 0))
    def _():
        _build_band(bidx_ref, tab_ref, band_ref, DIFF_HEADS)

    lv = lam_ref[...]
    lam = (jnp.exp(jnp.sum(lv[0:1] * lv[1:2], axis=-1, keepdims=True))
           - jnp.exp(jnp.sum(lv[2:3] * lv[3:4], axis=-1, keepdims=True)) + lambda_init)

    _load_queries(qt_ref, q_scr, _DIFF_STREAMS)
    _softmax_passes(qb, k_ref, vt_ref, q_scr, band_ref, lambda kt: None, logit_ref, m_ref, acc_ref,
                    _DIFF_STREAMS)

    avg = avg_ref[...]
    for p in range(N_PAIRS):
        halves = []
        for j in range(PAIR):
            h = PAIR * p + j
            if h < DIFF_HEADS:
                halves.append(_normalised(acc_ref, 2 * h) - lam * _normalised(acc_ref, 2 * h + 1))
            else:
                halves.append(jnp.zeros((HEAD_DIM, TQ), F32))
        out = jnp.concatenate(halves, axis=0).T
        ms = _group_mean(out * out, avg)
        out = out * lax.rsqrt(ms + LN_EPS) * g_ref[...] * (1.0 - lambda_init)
        o_ref[0, :, pl.ds(p * LANES, LANES)] = out.astype(o_ref.dtype)


def _diff(dqT, dk, dvT, lam_vecs, subln_g, bias_t, lambda_init, batch, seq):
    bidx = jnp.asarray(_band_bucket_index_t())
    avg = jnp.asarray(_group_avg(), BF16)
    g2 = jnp.concatenate([subln_g, subln_g]).reshape(1, LANES)
    nq = seq // TQ
    n_streams = len(_DIFF_STREAMS)
    const = lambda a: pl.BlockSpec(a.shape, lambda b, n: (0,) * a.ndim)
    out = pl.pallas_call(
        functools.partial(_diff_kernel, lambda_init),
        grid=(batch, nq),
        in_specs=[pl.BlockSpec(memory_space=pltpu.SMEM),
                  pl.BlockSpec((D_HEADS_PAD, TQ), lambda b, n: (0, b * nq + n)),
                  pl.BlockSpec((1, seq, D_HEADS_PAD), lambda b, n: (b, 0, 0)),
                  pl.BlockSpec((1, seq // TK, DIFF_HEADS * VT_ROWS, TK), lambda b, n: (b, 0, 0, 0)),
                  const(lam_vecs), const(g2), const(bidx), const(avg)],
        out_specs=pl.BlockSpec((1, TQ, D_HEADS_PAD), lambda b, n: (b, n, 0)),
        out_shape=jax.ShapeDtypeStruct((batch, seq, D_HEADS_PAD), BF16),
        scratch_shapes=[pltpu.VMEM((DIFF_HEADS, 2, TK, TQ), F32),
                        pltpu.VMEM((n_streams, LANES, TQ), BF16),
                        pltpu.VMEM((seq // TK, n_streams, TK, TQ), F32),
                        pltpu.VMEM((n_streams, SUBLANES, TQ), F32),
                        pltpu.VMEM((n_streams, VT_ROWS, TQ), F32)],
        compiler_params=_cparams(2),
        name="diff_attention",
    )(bias_t, dqT, dk.reshape(batch, seq, D_HEADS_PAD),
      dvT.reshape(batch, seq // TK, DIFF_HEADS * VT_ROWS, TK), lam_vecs, g2, bidx, avg)
    return out.reshape(batch * seq, D_HEADS_PAD)


def _out_ln_kernel(x_ref, ret_ref, dsa_ref, dif_ref, w_ref, g_ref, b_ref, o_ref):
    acc = _dot(ret_ref[...], w_ref[pl.ds(0, D_HEADS_PAD), :])
    acc = acc + _dot(dsa_ref[...], w_ref[pl.ds(D_HEADS_PAD, D_HEADS_PAD), :])
    acc = acc + _dot(dif_ref[...], w_ref[pl.ds(2 * D_HEADS_PAD, D_HEADS_PAD), :])
    r = DEEPNORM_ALPHA * x_ref[...] + acc
    o_ref[...] = _layer_norm(r, g_ref[...], b_ref[...])


def _pad_w_out(w_out):
    zeros = jnp.zeros((D_HEADS_PAD - D_DSA, w_out.shape[1]), w_out.dtype)
    return jnp.concatenate([w_out[:D_RET], w_out[D_RET:D_RET + D_DSA], zeros,
                            w_out[D_RET + D_DSA:], zeros], axis=0)


def _out_ln(x, ret, dsa, dif, w_pad, g, b):
    t = x.shape[0]
    full = lambda shape: pl.BlockSpec(shape, lambda i: (0, 0))
    tok = lambda w: pl.BlockSpec((TM_OUT, w), lambda i: (i, 0))
    return pl.pallas_call(
        _out_ln_kernel,
        grid=(t // TM_OUT,),
        in_specs=[tok(D_MODEL), tok(D_HEADS_PAD), tok(D_HEADS_PAD), tok(D_HEADS_PAD),
                  full((3 * D_HEADS_PAD, D_MODEL)), full((1, D_MODEL)), full((1, D_MODEL))],
        out_specs=tok(D_MODEL),
        out_shape=jax.ShapeDtypeStruct((t, D_MODEL), F32),
        compiler_params=_cparams(1),
        name="out_proj_ln",
    )(x, ret, dsa, dif, w_pad, g, b)


def kernel(x, w_in, w_out, ffn1_wg, ffn1_wu, ffn1_wd, ffn2_wg, ffn2_wu, ffn2_wd,
           ln_g, ln_b, diff_lambda, diff_subln_g, rel_bias):
    batch, seq, d = x.shape
    assert d == D_MODEL and seq % TQ == 0 and seq % TR == 0 and (batch * seq) % TM_FFN == 0
    assert min(DSA_TOPK_MAX, seq // 4) == DSA_TOPK_MAX
    h = x.reshape(batch * seq, d)
    bias_t = rel_bias.T
    row = lambda a: a.reshape(1, D_MODEL)
    for l in range(DEPTH):
        lambda_init = 0.8 - 0.6 * math.exp(-0.3 * l)
        h = _ffn_ln(h, ffn1_wg[l].astype(BF16), ffn1_wu[l].astype(BF16), ffn1_wd[l].astype(BF16),
                    row(ln_g[l, 0]), row(ln_b[l, 0]))
        w_tok, w_t = _split_w_in(w_in[l])
        (rq, rk, rv, rg, ak, dk, ik, aqT, dqT, iqT, iwT, avT, dvT) = _proj(
            h, w_tok.astype(BF16), w_t.astype(BF16))
        ret = _retention(rq, rk, rv, rg, batch, seq)
        dsa = _dsa(aqT, ak, avT, iqT, ik, iwT, bias_t[:DSA_HEADS], batch, seq)
        dif = _diff(dqT, dk, dvT, diff_lambda[l], diff_subln_g[l], bias_t[DSA_HEADS:], lambda_init, batch, seq)
        h = _out_ln(h, ret, dsa, dif, _pad_w_out(w_out[l]).astype(BF16), row(ln_g[l, 1]), row(ln_b[l, 1]))
        h = _ffn_ln(h, ffn2_wg[l].astype(BF16), ffn2_wu[l].astype(BF16), ffn2_wd[l].astype(BF16),
                    row(ln_g[l, 2]), row(ln_b[l, 2]))
    return h.reshape(batch, seq, d)
```

```python
import functools
import math

import numpy as np
import jax
import jax.numpy as jnp
from jax import lax
from jax.experimental import pallas as pl
from jax.experimental.pallas import tpu as pltpu

D_MODEL = 1024
DEPTH = 2
CHUNK = 64
HEAD_DIM = 64
RET_HEADS = 6
DSA_HEADS = 5
IDX_HEADS = 4
IDX_DIM = 64
DSA_TOPK_MAX = 256
DIFF_HEADS = 5
DIFF_QK_DIM = HEAD_DIM // 2
D_RET = RET_HEADS * HEAD_DIM
D_DSA = DSA_HEADS * HEAD_DIM
D_DIFF = DIFF_HEADS * HEAD_DIM
D_FF = 2816
N_BUCKETS = 32
MAX_DISTANCE = 128
ROPE_BASE = 10000.0
LN_EPS = 1e-5
HEAD_NORM_EPS = 1e-6
DEEPNORM_ALPHA = (2 * DEPTH) ** 0.25

LANES = 128
SUBLANES = 8
BF16_ROWS = 16
PAIR = LANES // HEAD_DIM
VMEM_LIMIT_BYTES = 56 * 1024 * 1024

D_HEADS_PAD = 3 * LANES
N_PAIRS = D_HEADS_PAD // LANES
VT_ROWS = HEAD_DIM + BF16_ROWS
TQ = 256
TK = 256
TR = 256
TM_FFN = 512
TF = 256
TM_PROJ = 512
TM_OUT = 512
N_ACC = 4
NEG = -1e30
LOG2E = math.log2(math.e)

F32 = jnp.float32
BF16 = jnp.bfloat16


def _dot(a, b):
    return jnp.dot(a, b, preferred_element_type=F32)


def _dot_nt(a, b):
    return lax.dot_general(a, b, (((1,), (1,)), ((), ())), preferred_element_type=F32)


def _layer_norm(r, g, b):
    mu = jnp.mean(r, axis=-1, keepdims=True)
    d = r - mu
    var = jnp.mean(d * d, axis=-1, keepdims=True)
    return d * lax.rsqrt(var + LN_EPS) * g + b


def _group_mean(x, avg_bf16):
    hi = x.astype(BF16)
    lo = (x - hi.astype(F32)).astype(BF16)
    return _dot(hi, avg_bf16) + _dot(lo, avg_bf16)


def _cparams(n_axes):
    return pltpu.CompilerParams(
        dimension_semantics=("arbitrary",) * n_axes,
        vmem_limit_bytes=VMEM_LIMIT_BYTES,
    )


def _rel_bucket_static(rel):
    rel = np.asarray(rel, dtype=np.int64)
    half = N_BUCKETS // 2
    max_exact = half // 2
    n = np.abs(rel)
    large = np.full(n.shape, max_exact, dtype=np.int64)
    for k in range(1, 64):
        large = np.where(64 * (2 ** k) <= n * n, max_exact + k, large)
    large = np.minimum(large, half - 1)
    return (np.where(rel > 0, half, 0) + np.where(n < max_exact, n, large)).astype(np.int32)


FAR_BUCKET = int(_rel_bucket_static(-(TK + 1)))


@functools.lru_cache(maxsize=None)
def _band_bucket_index_t():
    j = np.arange(TK)[:, None]
    i = np.arange(TQ)[None, :]
    prev = _rel_bucket_static(j - TK - i)
    diag = _rel_bucket_static(j - i)
    visible = j < (i // CHUNK + 1) * CHUNK
    diag = np.where(visible, diag, -1)
    return np.stack([prev, diag]).astype(np.int32)


@functools.lru_cache(maxsize=None)
def _rotary_tables(seq):
    lane = np.arange(LANES)
    d = lane % HEAD_DIM
    f = d % (HEAD_DIM // 2)
    inv = ROPE_BASE ** (-(2.0 * f) / HEAD_DIM)
    ang = np.arange(seq, dtype=np.float64)[:, None] * inv[None, :]
    cos = np.cos(ang)
    sin = np.sin(ang) * np.where(d < HEAD_DIM // 2, -1.0, 1.0)[None, :]
    return cos.astype(np.float32), sin.astype(np.float32)


@functools.lru_cache(maxsize=None)
def _retention_tables():
    gamma = 1.0 - 2.0 ** (-5.0 - np.arange(RET_HEADS, dtype=np.float64))
    t = np.arange(TR)
    dist = np.abs(t[:, None] - t[None, :]).astype(np.float64)
    visible = t[None, :] < (t[:, None] // CHUNK + 1) * CHUNK
    intra = np.stack([np.where(visible, g ** dist, 0.0) for g in gamma])
    lane_head = np.arange(D_HEADS_PAD) // HEAD_DIM
    g_lane = gamma[lane_head]
    qdec = g_lane[None, :] ** (t[:, None] + 1.0)
    kdec = g_lane[None, :] ** (TR - 1.0 - t[:, None])
    a = np.arange(LANES)
    same = (a[:, None] // HEAD_DIM) == (a[None, :] // HEAD_DIM)
    sdec = np.stack([np.where(same, (gamma[PAIR * p + a // HEAD_DIM] ** TR)[:, None], 0.0)
                     for p in range(RET_HEADS // PAIR)])
    return (intra.astype(np.float32), qdec.astype(np.float32), kdec.astype(np.float32),
            sdec.astype(np.float32), same.astype(np.float32))


@functools.lru_cache(maxsize=None)
def _group_avg():
    a = np.arange(LANES)
    same = (a[:, None] // HEAD_DIM) == (a[None, :] // HEAD_DIM)
    return np.where(same, 1.0 / HEAD_DIM, 0.0).astype(np.float32)


@functools.lru_cache(maxsize=None)
def _prefix_tril():
    a = np.arange(TK)
    return (a[None, :] <= a[:, None]).astype(np.float32)


def _lane_group(shape):
    return lax.broadcasted_iota(jnp.int32, shape, len(shape) - 1) // HEAD_DIM


def _ffn_ln_kernel(x_ref, wg_ref, wu_ref, wd_ref, g_ref, b_ref, o_ref):
    x = x_ref[...]
    xb = x.astype(BF16)
    acc = jnp.zeros(x.shape, F32)
    for c in range(D_FF // TF):
        cols = pl.ds(c * TF, TF)
        gate = _dot(xb, wg_ref[:, cols].astype(BF16))
        up = _dot(xb, wu_ref[:, cols].astype(BF16))
        h = gate * jax.nn.sigmoid(gate) * up
        acc = acc + _dot(h.astype(BF16), wd_ref[cols, :].astype(BF16))
    r = DEEPNORM_ALPHA * x + 0.5 * acc
    o_ref[...] = _layer_norm(r, g_ref[...], b_ref[...])


def _ffn_ln(x, wg, wu, wd, g, b):
    t = x.shape[0]
    full = lambda shape: pl.BlockSpec(shape, lambda i: (0, 0), pipeline_mode=pl.Buffered(1))
    return pl.pallas_call(
        _ffn_ln_kernel,
        grid=(t // TM_FFN,),
        in_specs=[pl.BlockSpec((TM_FFN, D_MODEL), lambda i: (i, 0)),
                  full((D_MODEL, D_FF)), full((D_MODEL, D_FF)), full((D_FF, D_MODEL)),
                  full((1, D_MODEL)), full((1, D_MODEL))],
        out_specs=pl.BlockSpec((TM_FFN, D_MODEL), lambda i: (i, 0)),
        out_shape=jax.ShapeDtypeStruct((t, D_MODEL), F32),
        compiler_params=_cparams(1),
        name="ffn_ln",
    )(x, wg, wu, wd, g, b)


_IW_ROWS = BF16_ROWS
_TOK_SEGMENTS = (("rq", D_HEADS_PAD, F32), ("rk", D_HEADS_PAD, F32), ("rv", D_HEADS_PAD, BF16),
                 ("rg", D_HEADS_PAD, F32), ("ak", D_HEADS_PAD, BF16), ("dk", D_HEADS_PAD, BF16),
                 ("ik", LANES, BF16))
_T_SEGMENTS = (("aqT", D_HEADS_PAD, BF16, HEAD_DIM ** -0.5 * LOG2E),
               ("dqT", D_HEADS_PAD, BF16, DIFF_QK_DIM ** -0.5 * LOG2E),
               ("iqT", IDX_HEADS * IDX_DIM, BF16, None), ("iwT", _IW_ROWS, F32, None))
_VT_SEGMENTS = (("avT", DSA_HEADS), ("dvT", DIFF_HEADS))
N_TOK = sum(w for _, w, _ in _TOK_SEGMENTS)
N_T = sum(s[1] for s in _T_SEGMENTS) + sum(n * HEAD_DIM for _, n in _VT_SEGMENTS)


def _split_w_in(w_in):
    splits = (D_RET, D_RET, D_RET, D_RET, D_DSA, D_DSA, D_DSA,
              IDX_HEADS * IDX_DIM, IDX_DIM, IDX_HEADS, D_DIFF, D_DIFF, D_DIFF)
    offs = np.cumsum((0,) + splits)
    (rq, rk, rv, rg, aq, ak, av, iq, ik, iw, dq, dk, dv) = [
        w_in[:, offs[i]:offs[i + 1]] for i in range(len(splits))]
    zeros = lambda n: jnp.zeros((w_in.shape[0], n), w_in.dtype)
    pad = lambda a, n: jnp.concatenate([a, zeros(n - a.shape[1])], axis=1)
    w_tok = jnp.concatenate([rq, rk, rv, rg, pad(ak, D_HEADS_PAD), pad(dk, D_HEADS_PAD),
                             ik, ik], axis=1)
    w_t = jnp.concatenate([pad(aq, D_HEADS_PAD), pad(dq, D_HEADS_PAD), iq, pad(iw, _IW_ROWS),
                           av, dv], axis=1).T
    return w_tok, w_t


def _proj_kernel(x_ref, wtok_ref, wt_ref, *o_refs):
    xb = x_ref[...].astype(BF16)
    refs = list(o_refs)
    off = 0
    for _, width, dtype in _TOK_SEGMENTS:
        refs.pop(0)[...] = _dot(xb, wtok_ref[:, pl.ds(off, width)]).astype(dtype)
        off += width
    off = 0
    for _, width, dtype, scale in _T_SEGMENTS:
        res = _dot_nt(wt_ref[pl.ds(off, width), :], xb)
        refs.pop(0)[...] = (res if scale is None else res * scale).astype(dtype)
        off += width
    ones = jnp.ones((BF16_ROWS, TK), BF16)
    for _, n_heads in _VT_SEGMENTS:
        o_ref = refs.pop(0)
        res = _dot_nt(wt_ref[pl.ds(off, n_heads * HEAD_DIM), :], xb).astype(BF16)
        for t in range(TM_PROJ // TK):
            for h in range(n_heads):
                o_ref[t, pl.ds(h * VT_ROWS, HEAD_DIM), :] = res[h * HEAD_DIM:(h + 1) * HEAD_DIM,
                                                                t * TK:(t + 1) * TK]
                o_ref[t, pl.ds(h * VT_ROWS + HEAD_DIM, BF16_ROWS), :] = ones
        off += n_heads * HEAD_DIM


def _proj(x, w_tok, w_t):
    t = x.shape[0]
    out_specs = ([pl.BlockSpec((TM_PROJ, w), lambda i: (i, 0)) for _, w, _ in _TOK_SEGMENTS]
                 + [pl.BlockSpec((s[1], TM_PROJ), lambda i: (0, i)) for s in _T_SEGMENTS]
                 + [pl.BlockSpec((TM_PROJ // TK, n * VT_ROWS, TK), lambda i: (i, 0, 0)) for _, n in _VT_SEGMENTS])
    out_shape = ([jax.ShapeDtypeStruct((t, w), dt) for _, w, dt in _TOK_SEGMENTS]
                 + [jax.ShapeDtypeStruct((s[1], t), s[2]) for s in _T_SEGMENTS]
                 + [jax.ShapeDtypeStruct((t // TK, n * VT_ROWS, TK), BF16) for _, n in _VT_SEGMENTS])
    return pl.pallas_call(
        _proj_kernel,
        grid=(t // TM_PROJ,),
        in_specs=[pl.BlockSpec((TM_PROJ, D_MODEL), lambda i: (i, 0)),
                  pl.BlockSpec((D_MODEL, N_TOK), lambda i: (0, 0)),
                  pl.BlockSpec((N_T, D_MODEL), lambda i: (0, 0))],
        out_specs=out_specs,
        out_shape=out_shape,
        compiler_params=_cparams(1),
        name="in_proj",
    )(x, w_tok, w_t)


def _swap_halves(x):
    lane = lax.broadcasted_iota(jnp.int32, x.shape, 1)
    first_half = (lane % HEAD_DIM) < (HEAD_DIM // 2)
    from_right = pltpu.roll(x, LANES - HEAD_DIM // 2, 1)
    from_left = pltpu.roll(x, HEAD_DIM // 2, 1)
    return jnp.where(first_half, from_right, from_left)


def _retention_kernel(q_ref, k_ref, v_ref, g_ref, cos_ref, sin_ref, intra_ref, qdec_ref, kdec_ref,
                      sdec_ref, same_ref, avg_ref, o_ref, state_ref):
    @pl.when(pl.program_id(1) == 0)
    def _():
        state_ref[...] = jnp.zeros(state_ref.shape, F32)

    cos = cos_ref[...]
    sin = sin_ref[...]
    avg = avg_ref[...]
    group = _lane_group((TR, LANES))
    for p in range(RET_HEADS // PAIR):
        lanes = pl.ds(p * LANES, LANES)
        q = q_ref[0, :, lanes]
        k = k_ref[0, :, lanes]
        q = (q * cos + _swap_halves(q) * sin) * (HEAD_DIM ** -0.5)
        k = k * cos + _swap_halves(k) * sin
        v = v_ref[0, :, lanes]
        kb = k.astype(BF16)
        intra = jnp.zeros((TR, LANES), F32)
        for j in range(PAIR):
            qh = jnp.where(group == j, q, 0.0).astype(BF16)
            scores = _dot_nt(qh, kb) * intra_ref[PAIR * p + j]
            oh = _dot(scores.astype(BF16), v)
            intra = jnp.where(group == j, oh, intra)
        state = state_ref[p]
        cross = _dot((q * qdec_ref[:, lanes]).astype(BF16), state.astype(BF16))
        o = intra + cross
        kd_t = (k * kdec_ref[:, lanes]).T.astype(BF16)
        state_ref[p] = state * sdec_ref[p] + _dot(kd_t, v) * same_ref[...]
        mu = _group_mean(o, avg)
        d = o - mu
        var = _group_mean(d * d, avg)
        normed = d * lax.rsqrt(var + HEAD_NORM_EPS)
        gate = g_ref[0, :, lanes]
        o_ref[0, :, lanes] = (gate * jax.nn.sigmoid(gate) * normed).astype(o_ref.dtype)


def _retention(rq, rk, rv, rg, batch, seq):
    cos, sin = _rotary_tables(seq)
    intra, qdec, kdec, sdec, same = _retention_tables()
    avg = jnp.asarray(_group_avg(), BF16)
    r3 = lambda a: a.reshape(batch, seq, D_HEADS_PAD)
    tok = pl.BlockSpec((1, TR, D_HEADS_PAD), lambda b, n: (b, n, 0))
    const = lambda a: pl.BlockSpec(a.shape, lambda b, n: (0,) * a.ndim)
    pos = pl.BlockSpec((TR, LANES), lambda b, n: (n, 0))
    out = pl.pallas_call(
        _retention_kernel,
        grid=(batch, seq // TR),
        in_specs=[tok, tok, tok, tok, pos, pos, const(intra), const(qdec), const(kdec),
                  const(sdec), const(same), const(avg)],
        out_specs=tok,
        out_shape=jax.ShapeDtypeStruct((batch, seq, D_HEADS_PAD), BF16),
        scratch_shapes=[pltpu.VMEM((RET_HEADS // PAIR, LANES, LANES), F32)],
        compiler_params=_cparams(2),
        name="retention",
    )(r3(rq), r3(rk), r3(rv), r3(rg), cos, sin, intra, qdec, kdec, sdec, same, avg)
    return out.reshape(batch * seq, D_HEADS_PAD)


def _build_band(bidx_ref, tab_ref, band_ref, n_heads):
    for h in range(n_heads):
        far = tab_ref[h, FAR_BUCKET]
        for t in range(2):
            idx = bidx_ref[t]
            val = jnp.full((TK, TQ), NEG, F32)
            for bkt in range(N_BUCKETS):
                val = jnp.where(idx == bkt, (tab_ref[h, bkt] - far) * LOG2E, val)
            band_ref[h, t] = val


def _load_queries(qt_ref, q_scr, streams):
    row = lax.broadcasted_iota(jnp.int32, (LANES, TQ), 0)
    for i, (pair, _, lo, hi) in enumerate(streams):
        q_scr[i] = jnp.where((row >= lo) & (row < hi), qt_ref[pl.ds(pair * LANES, LANES), :], 0)


def _for_tiles(n, fn):
    def pair(j, carry):
        fn(2 * j)
        fn(2 * j + 1)
        return carry

    lax.fori_loop(0, lax.shift_right_logical(n, 1), pair, 0)

    @pl.when(lax.rem(n, 2) == 1)
    def _():
        fn(n - 1)


def _softmax_passes(qb, k_ref, vt_ref, q_scr, band_ref, mask_fn, logit_ref, m_ref, acc_ref, streams):
    m_ref[...] = jnp.full(m_ref.shape, NEG, F32)

    def logits_tile(kt, band_t):
        rows = pl.ds(pl.multiple_of(kt * TK, TK), TK)
        mask = mask_fn(kt)
        for i, (pair, head, _, _) in enumerate(streams):
            lg = _dot(k_ref[0, rows, pl.ds(pair * LANES, LANES)], q_scr[i])
            if band_t is not None:
                lg = lg + band_ref[head, band_t]
            if mask is not None:
                lg = lg + mask
            logit_ref[kt, i] = lg
            m_ref[i] = jnp.maximum(m_ref[i], jnp.max(lg.reshape(TK // SUBLANES, SUBLANES, TQ), axis=0))

    _for_tiles(jnp.maximum(qb - 1, 0), lambda kt: logits_tile(kt, None))

    @pl.when(qb >= 1)
    def _():
        logits_tile(qb - 1, 0)
        logits_tile(qb, 1)

    @pl.when(qb == 0)
    def _():
        logits_tile(qb, 1)

    m_fin = [jnp.max(m_ref[i], axis=0, keepdims=True) for i in range(len(streams))]
    acc_ref[...] = jnp.zeros(acc_ref.shape, F32)

    def values_tile(kt):
        for i, (_, head, _, _) in enumerate(streams):
            p = jnp.exp2(logit_ref[kt, i] - m_fin[i])
            acc_ref[i] += _dot(vt_ref[0, kt, pl.ds(head * VT_ROWS, VT_ROWS), :], p.astype(BF16))

    _for_tiles(qb + 1, values_tile)


def _normalised(acc_ref, i):
    a = acc_ref[i]
    return a[:HEAD_DIM] / a[HEAD_DIM:HEAD_DIM + 1]


_DSA_STREAMS = tuple((h // PAIR, h, (h % PAIR) * HEAD_DIM, (h % PAIR + 1) * HEAD_DIM) for h in range(DSA_HEADS))


def _key_to_f32(key):
    bits = jnp.where(key < 0, key ^ jnp.int32(-2 ** 31), ~key)
    return lax.bitcast_convert_type(bits, F32)


def _dsa_kernel(tab_ref, qt_ref, k_ref, vt_ref, iqt_ref, ik_ref, iwt_ref, bidx_ref, tril_ref,
                o_ref, band_ref, score_ref, q_scr, logit_ref, m_ref, acc_ref, tiecnt_ref, tie_ref):
    qb = pl.program_id(1)
    n_tiles = qb + 1

    @pl.when((pl.program_id(0) == 0) & (qb == 0))
    def _():
        _build_band(bidx_ref, tab_ref, band_ref, DSA_HEADS)

    iw = iwt_ref[...] * (IDX_HEADS ** -0.5 * IDX_DIM ** -0.5)
    row = lax.broadcasted_iota(jnp.int32, (LANES, TQ), 0)
    iq = [jnp.where(row // HEAD_DIM == (h % PAIR), iqt_ref[pl.ds((h // PAIR) * LANES, LANES), :], 0)
          for h in range(IDX_HEADS)]

    def score_tile(kt):
        ki = ik_ref[0, pl.ds(pl.multiple_of(kt * TK, TK), TK), :]
        s = jnp.zeros((TK, TQ), F32)
        for h in range(IDX_HEADS):
            s = s + jnp.maximum(_dot(ki, iq[h]), 0.0) * iw[h:h + 1, :]
        return s

    def past_scores(kt, carry):
        score_ref[kt] = score_tile(kt)
        return carry

    lax.fori_loop(0, qb, past_scores, 0)
    score_ref[qb] = jnp.where(band_ref[0, 1] > 0.5 * NEG, score_tile(qb), -jnp.inf)

    k_sel = float(DSA_TOPK_MAX)

    def count(pred):
        def body(kt, c):
            xs = score_ref[kt].reshape(TK // SUBLANES, SUBLANES, TQ)
            accs = [c[a] for a in range(N_ACC)]
            for g in range(TK // SUBLANES):
                a = accs[g % N_ACC]
                accs[g % N_ACC] = jnp.where(pred(xs[g]), a + 1.0, a)
            return jnp.stack(accs)
        c = lax.fori_loop(0, n_tiles, body, jnp.zeros((N_ACC, SUBLANES, TQ), F32))
        return jnp.sum(jnp.sum(c, axis=0), axis=0, keepdims=True)

    def bisect(i, prefix):
        cand = prefix | lax.shift_left(jnp.int32(1), 31 - i)
        cand_f = _key_to_f32(cand)
        take = count(lambda s: s >= cand_f) >= k_sel
        return jnp.where(take, cand, prefix)

    prefix = lax.fori_loop(0, 32, bisect, jnp.zeros((1, TQ), jnp.int32))
    col = lax.broadcasted_iota(jnp.int32, (1, TQ), 1)
    limit = qb * TQ + (col // CHUNK + 1) * CHUNK
    select_all = limit <= DSA_TOPK_MAX
    thr = jnp.where(select_all, -jnp.inf, _key_to_f32(prefix))
    n_gt = count(lambda s: s > thr)
    need = jnp.where(select_all, 0.0, k_sel - n_gt)

    def tie_counts(kt, carry):
        tied = jnp.where(score_ref[kt] == thr, 1.0, 0.0).reshape(TK // SUBLANES, SUBLANES, TQ)
        tiecnt_ref[kt] = jnp.sum(jnp.sum(tied, axis=0), axis=0, keepdims=True)
        return carry

    lax.fori_loop(0, n_tiles, tie_counts, 0)
    tie_ref[...] = jnp.zeros(tie_ref.shape, F32)
    tril = tril_ref[...]

    def selection_mask(kt):
        s = score_ref[kt]
        tied = s == thr
        tied_before = tie_ref[...]
        tie_ref[...] = tied_before + tiecnt_ref[kt]
        tied_upto = tied_before + _dot(tril, jnp.where(tied, 1.0, 0.0).astype(BF16))
        return jnp.where((s > thr) | (tied & (tied_upto <= need)), 0.0, NEG)

    _load_queries(qt_ref, q_scr, _DSA_STREAMS)
    _softmax_passes(qb, k_ref, vt_ref, q_scr, band_ref, selection_mask, logit_ref, m_ref, acc_ref, _DSA_STREAMS)

    for p in range(N_PAIRS):
        halves = [_normalised(acc_ref, PAIR * p + j) if PAIR * p + j < DSA_HEADS
                  else jnp.zeros((HEAD_DIM, TQ), F32) for j in range(PAIR)]
        o_ref[0, :, pl.ds(p * LANES, LANES)] = jnp.concatenate(halves, axis=0).T.astype(o_ref.dtype)


def _dsa(aqT, ak, avT, iqT, ik, iwT, bias_t, batch, seq):
    bidx = jnp.asarray(_band_bucket_index_t())
    tril = jnp.asarray(_prefix_tril(), BF16)
    nq = seq // TQ
    n_streams = len(_DSA_STREAMS)
    qcols = lambda w: pl.BlockSpec((w, TQ), lambda b, n: (0, b * nq + n))
    ktok = lambda w: pl.BlockSpec((1, seq, w), lambda b, n: (b, 0, 0))
    const = lambda a: pl.BlockSpec(a.shape, lambda b, n: (0,) * a.ndim)
    out = pl.pallas_call(
        _dsa_kernel,
        grid=(batch, nq),
        in_specs=[pl.BlockSpec(memory_space=pltpu.SMEM),
                  qcols(D_HEADS_PAD), ktok(D_HEADS_PAD),
                  pl.BlockSpec((1, seq // TK, DSA_HEADS * VT_ROWS, TK), lambda b, n: (b, 0, 0, 0)),
                  qcols(IDX_HEADS * IDX_DIM), ktok(LANES), qcols(_IW_ROWS), const(bidx), const(tril)],
        out_specs=pl.BlockSpec((1, TQ, D_HEADS_PAD), lambda b, n: (b, n, 0)),
        out_shape=jax.ShapeDtypeStruct((batch, seq, D_HEADS_PAD), BF16),
        scratch_shapes=[pltpu.VMEM((DSA_HEADS, 2, TK, TQ), F32),
                        pltpu.VMEM((seq // TK, TK, TQ), F32),
                        pltpu.VMEM((n_streams, LANES, TQ), BF16),
                        pltpu.VMEM((seq // TK, n_streams, TK, TQ), F32),
                        pltpu.VMEM((n_streams, SUBLANES, TQ), F32),
                        pltpu.VMEM((n_streams, VT_ROWS, TQ), F32),
                        pltpu.VMEM((seq // TK, 1, TQ), F32),
                        pltpu.VMEM((1, TQ), F32)],
        compiler_params=_cparams(2),
        name="dsa_attention",
    )(bias_t, aqT, ak.reshape(batch, seq, D_HEADS_PAD),
      avT.reshape(batch, seq // TK, DSA_HEADS * VT_ROWS, TK), iqT, ik.reshape(batch, seq, LANES), iwT, bidx, tril)
    return out.reshape(batch * seq, D_HEADS_PAD)


_DIFF_STREAMS = tuple(
    (h // PAIR, h, (h % PAIR) * HEAD_DIM + mm * DIFF_QK_DIM, (h % PAIR) * HEAD_DIM + (mm + 1) * DIFF_QK_DIM)
    for h in range(DIFF_HEADS) for mm in range(2))


def _diff_kernel(lambda_init, tab_ref, qt_ref, k_ref, vt_ref, lam_ref, g_ref, bidx_ref, avg_ref,
                 o_ref, band_ref, q_scr, logit_ref, m_ref, acc_ref):
    qb = pl.program_id(1)

    @pl.when((pl.program_id(0) == 0) & (qb == 0))
    def _():
        _build_band(bidx_ref, tab_ref, band_ref, DIFF_HEADS)

    lv = lam_ref[...]
    lam = (jnp.exp(jnp.sum(lv[0:1] * lv[1:2], axis=-1, keepdims=True))
           - jnp.exp(jnp.sum(lv[2:3] * lv[3:4], axis=-1, keepdims=True)) + lambda_init)

    _load_queries(qt_ref, q_scr, _DIFF_STREAMS)
    _softmax_passes(qb, k_ref, vt_ref, q_scr, band_ref, lambda kt: None, logit_ref, m_ref, acc_ref,
                    _DIFF_STREAMS)

    avg = avg_ref[...]
    for p in range(N_PAIRS):
        halves = []
        for j in range(PAIR):
            h = PAIR * p + j
            if h < DIFF_HEADS:
                halves.append(_normalised(acc_ref, 2 * h) - lam * _normalised(acc_ref, 2 * h + 1))
            else:
                halves.append(jnp.zeros((HEAD_DIM, TQ), F32))
        out = jnp.concatenate(halves, axis=0).T
        ms = _group_mean(out * out, avg)
        out = out * lax.rsqrt(ms + LN_EPS) * g_ref[...] * (1.0 - lambda_init)
        o_ref[0, :, pl.ds(p * LANES, LANES)] = out.astype(o_ref.dtype)


def _diff(dqT, dk, dvT, lam_vecs, subln_g, bias_t, lambda_init, batch, seq):
    bidx = jnp.asarray(_band_bucket_index_t())
    avg = jnp.asarray(_group_avg(), BF16)
    g2 = jnp.concatenate([subln_g, subln_g]).reshape(1, LANES)
    nq = seq // TQ
    n_streams = len(_DIFF_STREAMS)
    const = lambda a: pl.BlockSpec(a.shape, lambda b, n: (0,) * a.ndim)
    out = pl.pallas_call(
        functools.partial(_diff_kernel, lambda_init),
        grid=(batch, nq),
        in_specs=[pl.BlockSpec(memory_space=pltpu.SMEM),
                  pl.BlockSpec((D_HEADS_PAD, TQ), lambda b, n: (0, b * nq + n)),
                  pl.BlockSpec((1, seq, D_HEADS_PAD), lambda b, n: (b, 0, 0)),
                  pl.BlockSpec((1, seq // TK, DIFF_HEADS * VT_ROWS, TK), lambda b, n: (b, 0, 0, 0)),
                  const(lam_vecs), const(g2), const(bidx), const(avg)],
        out_specs=pl.BlockSpec((1, TQ, D_HEADS_PAD), lambda b, n: (b, n, 0)),
        out_shape=jax.ShapeDtypeStruct((batch, seq, D_HEADS_PAD), BF16),
        scratch_shapes=[pltpu.VMEM((DIFF_HEADS, 2, TK, TQ), F32),
                        pltpu.VMEM((n_streams, LANES, TQ), BF16),
                        pltpu.VMEM((seq // TK, n_streams, TK, TQ), F32),
                        pltpu.VMEM((n_streams, SUBLANES, TQ), F32),
                        pltpu.VMEM((n_streams, VT_ROWS, TQ), F32)],
        compiler_params=_cparams(2),
        name="diff_attention",
    )(bias_t, dqT, dk.reshape(batch, seq, D_HEADS_PAD),
      dvT.reshape(batch, seq // TK, DIFF_HEADS * VT_ROWS, TK), lam_vecs, g2, bidx, avg)
    return out.reshape(batch * seq, D_HEADS_PAD)


def _out_ln_kernel(x_ref, ret_ref, dsa_ref, dif_ref, w_ref, g_ref, b_ref, o_ref):
    acc = _dot(ret_ref[...], w_ref[pl.ds(0, D_HEADS_PAD), :])
    acc = acc + _dot(dsa_ref[...], w_ref[pl.ds(D_HEADS_PAD, D_HEADS_PAD), :])
    acc = acc + _dot(dif_ref[...], w_ref[pl.ds(2 * D_HEADS_PAD, D_HEADS_PAD), :])
    r = DEEPNORM_ALPHA * x_ref[...] + acc
    o_ref[...] = _layer_norm(r, g_ref[...], b_ref[...])


def _pad_w_out(w_out):
    zeros = jnp.zeros((D_HEADS_PAD - D_DSA, w_out.shape[1]), w_out.dtype)
    return jnp.concatenate([w_out[:D_RET], w_out[D_RET:D_RET + D_DSA], zeros,
                            w_out[D_RET + D_DSA:], zeros], axis=0)


def _out_ln(x, ret, dsa, dif, w_pad, g, b):
    t = x.shape[0]
    full = lambda shape: pl.BlockSpec(shape, lambda i: (0, 0))
    tok = lambda w: pl.BlockSpec((TM_OUT, w), lambda i: (i, 0))
    return pl.pallas_call(
        _out_ln_kernel,
        grid=(t // TM_OUT,),
        in_specs=[tok(D_MODEL), tok(D_HEADS_PAD), tok(D_HEADS_PAD), tok(D_HEADS_PAD),
                  full((3 * D_HEADS_PAD, D_MODEL)), full((1, D_MODEL)), full((1, D_MODEL))],
        out_specs=tok(D_MODEL),
        out_shape=jax.ShapeDtypeStruct((t, D_MODEL), F32),
        compiler_params=_cparams(1),
        name="out_proj_ln",
    )(x, ret, dsa, dif, w_pad, g, b)


def kernel(x, w_in, w_out, ffn1_wg, ffn1_wu, ffn1_wd, ffn2_wg, ffn2_wu, ffn2_wd,
           ln_g, ln_b, diff_lambda, diff_subln_g, rel_bias):
    batch, seq, d = x.shape
    assert d == D_MODEL and seq % TQ == 0 and seq % TR == 0 and (batch * seq) % TM_FFN == 0
    assert min(DSA_TOPK_MAX, seq // 4) == DSA_TOPK_MAX
    h = x.reshape(batch * seq, d)
    bias_t = rel_bias.T
    row = lambda a: a.reshape(1, D_MODEL)
    for l in range(DEPTH):
        lambda_init = 0.8 - 0.6 * math.exp(-0.3 * l)
        h = _ffn_ln(h, ffn1_wg[l], ffn1_wu[l], ffn1_wd[l],
                    row(ln_g[l, 0]), row(ln_b[l, 0]))
        w_tok, w_t = _split_w_in(w_in[l])
        (rq, rk, rv, rg, ak, dk, ik, aqT, dqT, iqT, iwT, avT, dvT) = _proj(
            h, w_tok.astype(BF16), w_t.astype(BF16))
        ret = _retention(rq, rk, rv, rg, batch, seq)
        dsa = _dsa(aqT, ak, avT, iqT, ik, iwT, bias_t[:DSA_HEADS], batch, seq)
        dif = _diff(dqT, dk, dvT, diff_lambda[l], diff_subln_g[l], bias_t[DSA_HEADS:], lambda_init, batch, seq)
        h = _out_ln(h, ret, dsa, dif, _pad_w_out(w_out[l]).astype(BF16), row(ln_g[l, 1]), row(ln_b[l, 1]))
        h = _ffn_ln(h, ffn2_wg[l], ffn2_wu[l], ffn2_wd[l],
                    row(ln_g[l, 2]), row(ln_b[l, 2]))
    return h.reshape(batch, seq, d)
```

```python
import functools
import math

import numpy as np
import jax
import jax.numpy as jnp
from jax import lax
from jax.experimental import pallas as pl
from jax.experimental.pallas import tpu as pltpu

D_MODEL = 1024
DEPTH = 2
CHUNK = 64
HEAD_DIM = 64
RET_HEADS = 6
DSA_HEADS = 5
IDX_HEADS = 4
IDX_DIM = 64
DSA_TOPK_MAX = 256
DIFF_HEADS = 5
DIFF_QK_DIM = HEAD_DIM // 2
D_RET = RET_HEADS * HEAD_DIM
D_DSA = DSA_HEADS * HEAD_DIM
D_DIFF = DIFF_HEADS * HEAD_DIM
D_FF = 2816
N_BUCKETS = 32
MAX_DISTANCE = 128
ROPE_BASE = 10000.0
LN_EPS = 1e-5
HEAD_NORM_EPS = 1e-6
DEEPNORM_ALPHA = (2 * DEPTH) ** 0.25

LANES = 128
SUBLANES = 8
BF16_ROWS = 16
PAIR = LANES // HEAD_DIM
VMEM_LIMIT_BYTES = 56 * 1024 * 1024

D_HEADS_PAD = 3 * LANES
N_PAIRS = D_HEADS_PAD // LANES
VT_ROWS = HEAD_DIM + BF16_ROWS
TQ = 256
TK = 256
TR = 256
TM_FFN = 512
TF = 256
TM_PROJ = 512
TM_OUT = 512
N_ACC = 4
SEARCH_STEPS_PER_CHECK = 4
SEARCH_ROUNDS = 6
NEG = -1e30
LOG2E = math.log2(math.e)

F32 = jnp.float32
BF16 = jnp.bfloat16


def _dot(a, b):
    return jnp.dot(a, b, preferred_element_type=F32)


def _dot_nt(a, b):
    return lax.dot_general(a, b, (((1,), (1,)), ((), ())), preferred_element_type=F32)


def _layer_norm(r, g, b):
    mu = jnp.mean(r, axis=-1, keepdims=True)
    d = r - mu
    var = jnp.mean(d * d, axis=-1, keepdims=True)
    return d * lax.rsqrt(var + LN_EPS) * g + b


def _group_mean(x, avg_bf16):
    hi = x.astype(BF16)
    lo = (x - hi.astype(F32)).astype(BF16)
    return _dot(hi, avg_bf16) + _dot(lo, avg_bf16)


def _cparams(n_axes):
    return pltpu.CompilerParams(
        dimension_semantics=("arbitrary",) * n_axes,
        vmem_limit_bytes=VMEM_LIMIT_BYTES,
    )


def _rel_bucket_static(rel):
    rel = np.asarray(rel, dtype=np.int64)
    half = N_BUCKETS // 2
    max_exact = half // 2
    n = np.abs(rel)
    large = np.full(n.shape, max_exact, dtype=np.int64)
    for k in range(1, 64):
        large = np.where(64 * (2 ** k) <= n * n, max_exact + k, large)
    large = np.minimum(large, half - 1)
    return (np.where(rel > 0, half, 0) + np.where(n < max_exact, n, large)).astype(np.int32)


FAR_BUCKET = int(_rel_bucket_static(-(TK + 1)))


@functools.lru_cache(maxsize=None)
def _band_bucket_index_t():
    j = np.arange(TK)[:, None]
    i = np.arange(TQ)[None, :]
    prev = _rel_bucket_static(j - TK - i)
    diag = _rel_bucket_static(j - i)
    visible = j < (i // CHUNK + 1) * CHUNK
    diag = np.where(visible, diag, -1)
    return np.stack([prev, diag]).astype(np.int32)


@functools.lru_cache(maxsize=None)
def _rotary_tables(seq):
    lane = np.arange(LANES)
    d = lane % HEAD_DIM
    f = d % (HEAD_DIM // 2)
    inv = ROPE_BASE ** (-(2.0 * f) / HEAD_DIM)
    ang = np.arange(seq, dtype=np.float64)[:, None] * inv[None, :]
    cos = np.cos(ang)
    sin = np.sin(ang) * np.where(d < HEAD_DIM // 2, -1.0, 1.0)[None, :]
    return cos.astype(np.float32), sin.astype(np.float32)


@functools.lru_cache(maxsize=None)
def _retention_tables():
    gamma = 1.0 - 2.0 ** (-5.0 - np.arange(RET_HEADS, dtype=np.float64))
    t = np.arange(TR)
    dist = np.abs(t[:, None] - t[None, :]).astype(np.float64)
    visible = t[None, :] < (t[:, None] // CHUNK + 1) * CHUNK
    intra = np.stack([np.where(visible, g ** dist, 0.0) for g in gamma])
    lane_head = np.arange(D_HEADS_PAD) // HEAD_DIM
    g_lane = gamma[lane_head]
    qdec = g_lane[None, :] ** (t[:, None] + 1.0)
    kdec = g_lane[None, :] ** (TR - 1.0 - t[:, None])
    a = np.arange(LANES)
    same = (a[:, None] // HEAD_DIM) == (a[None, :] // HEAD_DIM)
    sdec = np.stack([np.where(same, (gamma[PAIR * p + a // HEAD_DIM] ** TR)[:, None], 0.0)
                     for p in range(RET_HEADS // PAIR)])
    return (intra.astype(np.float32), qdec.astype(np.float32), kdec.astype(np.float32),
            sdec.astype(np.float32), same.astype(np.float32))


@functools.lru_cache(maxsize=None)
def _group_avg():
    a = np.arange(LANES)
    same = (a[:, None] // HEAD_DIM) == (a[None, :] // HEAD_DIM)
    return np.where(same, 1.0 / HEAD_DIM, 0.0).astype(np.float32)


@functools.lru_cache(maxsize=None)
def _prefix_tril():
    a = np.arange(TK)
    return (a[None, :] <= a[:, None]).astype(np.float32)


def _lane_group(shape):
    return lax.broadcasted_iota(jnp.int32, shape, len(shape) - 1) // HEAD_DIM


def _ffn_ln_kernel(x_ref, wg_ref, wu_ref, wd_ref, g_ref, b_ref, o_ref):
    x = x_ref[...]
    xb = x.astype(BF16)
    acc = jnp.zeros(x.shape, F32)
    for c in range(D_FF // TF):
        cols = pl.ds(c * TF, TF)
        gate = _dot(xb, wg_ref[:, cols].astype(BF16))
        up = _dot(xb, wu_ref[:, cols].astype(BF16))
        h = gate * jax.nn.sigmoid(gate) * up
        acc = acc + _dot(h.astype(BF16), wd_ref[cols, :].astype(BF16))
    r = DEEPNORM_ALPHA * x + 0.5 * acc
    o_ref[...] = _layer_norm(r, g_ref[...], b_ref[...])


def _ffn_ln(x, wg, wu, wd, g, b, layer):
    t = x.shape[0]
    stacked = lambda shape: pl.BlockSpec((None,) + shape, lambda i: (layer, 0, 0), pipeline_mode=pl.Buffered(1))
    full = lambda shape: pl.BlockSpec(shape, lambda i: (0, 0))
    return pl.pallas_call(
        _ffn_ln_kernel,
        grid=(t // TM_FFN,),
        in_specs=[pl.BlockSpec((TM_FFN, D_MODEL), lambda i: (i, 0)),
                  stacked((D_MODEL, D_FF)), stacked((D_MODEL, D_FF)), stacked((D_FF, D_MODEL)),
                  full((1, D_MODEL)), full((1, D_MODEL))],
        out_specs=pl.BlockSpec((TM_FFN, D_MODEL), lambda i: (i, 0)),
        out_shape=jax.ShapeDtypeStruct((t, D_MODEL), F32),
        compiler_params=_cparams(1),
        name="ffn_ln",
    )(x, wg, wu, wd, g, b)


_IW_ROWS = BF16_ROWS
_TOK_SEGMENTS = (("rq", D_HEADS_PAD, F32), ("rk", D_HEADS_PAD, F32), ("rv", D_HEADS_PAD, BF16),
                 ("rg", D_HEADS_PAD, F32), ("ak", D_HEADS_PAD, BF16), ("dk", D_HEADS_PAD, BF16),
                 ("ik", LANES, BF16))
_T_SEGMENTS = (("aqT", D_HEADS_PAD, BF16, HEAD_DIM ** -0.5 * LOG2E),
               ("dqT", D_HEADS_PAD, BF16, DIFF_QK_DIM ** -0.5 * LOG2E),
               ("iqT", IDX_HEADS * IDX_DIM, BF16, None), ("iwT", _IW_ROWS, F32, None))
_VT_SEGMENTS = (("avT", DSA_HEADS), ("dvT", DIFF_HEADS))
N_TOK = sum(w for _, w, _ in _TOK_SEGMENTS)
N_T = sum(s[1] for s in _T_SEGMENTS) + sum(n * HEAD_DIM for _, n in _VT_SEGMENTS)


def _split_w_in(w_in):
    splits = (D_RET, D_RET, D_RET, D_RET, D_DSA, D_DSA, D_DSA,
              IDX_HEADS * IDX_DIM, IDX_DIM, IDX_HEADS, D_DIFF, D_DIFF, D_DIFF)
    offs = np.cumsum((0,) + splits)
    (rq, rk, rv, rg, aq, ak, av, iq, ik, iw, dq, dk, dv) = [
        w_in[:, offs[i]:offs[i + 1]] for i in range(len(splits))]
    zeros = lambda n: jnp.zeros((w_in.shape[0], n), w_in.dtype)
    pad = lambda a, n: jnp.concatenate([a, zeros(n - a.shape[1])], axis=1)
    w_tok = jnp.concatenate([rq, rk, rv, rg, pad(ak, D_HEADS_PAD), pad(dk, D_HEADS_PAD),
                             ik, ik], axis=1)
    w_t = jnp.concatenate([pad(aq, D_HEADS_PAD), pad(dq, D_HEADS_PAD), iq, pad(iw, _IW_ROWS),
                           av, dv], axis=1).T
    return w_tok, w_t


def _proj_kernel(x_ref, wtok_ref, wt_ref, *o_refs):
    xb = x_ref[...].astype(BF16)
    refs = list(o_refs)
    off = 0
    for _, width, dtype in _TOK_SEGMENTS:
        refs.pop(0)[...] = _dot(xb, wtok_ref[:, pl.ds(off, width)]).astype(dtype)
        off += width
    off = 0
    for _, width, dtype, scale in _T_SEGMENTS:
        res = _dot_nt(wt_ref[pl.ds(off, width), :], xb)
        refs.pop(0)[...] = (res if scale is None else res * scale).astype(dtype)
        off += width
    ones = jnp.ones((BF16_ROWS, TK), BF16)
    for _, n_heads in _VT_SEGMENTS:
        o_ref = refs.pop(0)
        res = _dot_nt(wt_ref[pl.ds(off, n_heads * HEAD_DIM), :], xb).astype(BF16)
        for t in range(TM_PROJ // TK):
            for h in range(n_heads):
                o_ref[t, pl.ds(h * VT_ROWS, HEAD_DIM), :] = res[h * HEAD_DIM:(h + 1) * HEAD_DIM,
                                                                t * TK:(t + 1) * TK]
                o_ref[t, pl.ds(h * VT_ROWS + HEAD_DIM, BF16_ROWS), :] = ones
        off += n_heads * HEAD_DIM


def _proj(x, w_tok, w_t):
    t = x.shape[0]
    out_specs = ([pl.BlockSpec((TM_PROJ, w), lambda i: (i, 0)) for _, w, _ in _TOK_SEGMENTS]
                 + [pl.BlockSpec((s[1], TM_PROJ), lambda i: (0, i)) for s in _T_SEGMENTS]
                 + [pl.BlockSpec((TM_PROJ // TK, n * VT_ROWS, TK), lambda i: (i, 0, 0)) for _, n in _VT_SEGMENTS])
    out_shape = ([jax.ShapeDtypeStruct((t, w), dt) for _, w, dt in _TOK_SEGMENTS]
                 + [jax.ShapeDtypeStruct((s[1], t), s[2]) for s in _T_SEGMENTS]
                 + [jax.ShapeDtypeStruct((t // TK, n * VT_ROWS, TK), BF16) for _, n in _VT_SEGMENTS])
    return pl.pallas_call(
        _proj_kernel,
        grid=(t // TM_PROJ,),
        in_specs=[pl.BlockSpec((TM_PROJ, D_MODEL), lambda i: (i, 0)),
                  pl.BlockSpec((D_MODEL, N_TOK), lambda i: (0, 0)),
                  pl.BlockSpec((N_T, D_MODEL), lambda i: (0, 0))],
        out_specs=out_specs,
        out_shape=out_shape,
        compiler_params=_cparams(1),
        name="in_proj",
    )(x, w_tok, w_t)


def _swap_halves(x):
    lane = lax.broadcasted_iota(jnp.int32, x.shape, 1)
    first_half = (lane % HEAD_DIM) < (HEAD_DIM // 2)
    from_right = pltpu.roll(x, LANES - HEAD_DIM // 2, 1)
    from_left = pltpu.roll(x, HEAD_DIM // 2, 1)
    return jnp.where(first_half, from_right, from_left)


def _retention_kernel(q_ref, k_ref, v_ref, g_ref, cos_ref, sin_ref, intra_ref, qdec_ref, kdec_ref,
                      sdec_ref, same_ref, avg_ref, o_ref, state_ref):
    @pl.when(pl.program_id(1) == 0)
    def _():
        state_ref[...] = jnp.zeros(state_ref.shape, F32)

    cos = cos_ref[...]
    sin = sin_ref[...]
    avg = avg_ref[...]
    group = _lane_group((TR, LANES))
    for p in range(RET_HEADS // PAIR):
        lanes = pl.ds(p * LANES, LANES)
        q = q_ref[0, :, lanes]
        k = k_ref[0, :, lanes]
        q = (q * cos + _swap_halves(q) * sin) * (HEAD_DIM ** -0.5)
        k = k * cos + _swap_halves(k) * sin
        v = v_ref[0, :, lanes]
        kb = k.astype(BF16)
        intra = jnp.zeros((TR, LANES), F32)
        for j in range(PAIR):
            qh = jnp.where(group == j, q, 0.0).astype(BF16)
            scores = _dot_nt(qh, kb) * intra_ref[PAIR * p + j]
            oh = _dot(scores.astype(BF16), v)
            intra = jnp.where(group == j, oh, intra)
        state = state_ref[p]
        cross = _dot((q * qdec_ref[:, lanes]).astype(BF16), state.astype(BF16))
        o = intra + cross
        kd_t = (k * kdec_ref[:, lanes]).T.astype(BF16)
        state_ref[p] = state * sdec_ref[p] + _dot(kd_t, v) * same_ref[...]
        mu = _group_mean(o, avg)
        d = o - mu
        var = _group_mean(d * d, avg)
        normed = d * lax.rsqrt(var + HEAD_NORM_EPS)
        gate = g_ref[0, :, lanes]
        o_ref[0, :, lanes] = (gate * jax.nn.sigmoid(gate) * normed).astype(o_ref.dtype)


def _retention(rq, rk, rv, rg, batch, seq):
    cos, sin = _rotary_tables(seq)
    intra, qdec, kdec, sdec, same = _retention_tables()
    avg = jnp.asarray(_group_avg(), BF16)
    r3 = lambda a: a.reshape(batch, seq, D_HEADS_PAD)
    tok = pl.BlockSpec((1, TR, D_HEADS_PAD), lambda b, n: (b, n, 0))
    const = lambda a: pl.BlockSpec(a.shape, lambda b, n: (0,) * a.ndim)
    pos = pl.BlockSpec((TR, LANES), lambda b, n: (n, 0))
    out = pl.pallas_call(
        _retention_kernel,
        grid=(batch, seq // TR),
        in_specs=[tok, tok, tok, tok, pos, pos, const(intra), const(qdec), const(kdec),
                  const(sdec), const(same), const(avg)],
        out_specs=tok,
        out_shape=jax.ShapeDtypeStruct((batch, seq, D_HEADS_PAD), BF16),
        scratch_shapes=[pltpu.VMEM((RET_HEADS // PAIR, LANES, LANES), F32)],
        compiler_params=_cparams(2),
        name="retention",
    )(r3(rq), r3(rk), r3(rv), r3(rg), cos, sin, intra, qdec, kdec, sdec, same, avg)
    return out.reshape(batch * seq, D_HEADS_PAD)


def _build_band(bidx_ref, tab_ref, band_ref, n_heads):
    for h in range(n_heads):
        far = tab_ref[h, FAR_BUCKET]
        for t in range(2):
            idx = bidx_ref[t]
            val = jnp.full((TK, TQ), NEG, F32)
            for bkt in range(N_BUCKETS):
                val = jnp.where(idx == bkt, (tab_ref[h, bkt] - far) * LOG2E, val)
            band_ref[h, t] = val


def _load_queries(qt_ref, q_scr, streams):
    row = lax.broadcasted_iota(jnp.int32, (LANES, TQ), 0)
    for i, (pair, _, lo, hi) in enumerate(streams):
        q_scr[i] = jnp.where((row >= lo) & (row < hi), qt_ref[pl.ds(pair * LANES, LANES), :], 0)


def _for_tiles(n, fn):
    def pair(j, carry):
        fn(2 * j)
        fn(2 * j + 1)
        return carry

    lax.fori_loop(0, lax.shift_right_logical(n, 1), pair, 0)

    @pl.when(lax.rem(n, 2) == 1)
    def _():
        fn(n - 1)


def _softmax_passes(qb, k_ref, vt_ref, q_scr, band_ref, mask_fn, logit_ref, m_ref, acc_ref, streams):
    m_ref[...] = jnp.full(m_ref.shape, NEG, F32)

    def logits_tile(kt, band_t):
        rows = pl.ds(pl.multiple_of(kt * TK, TK), TK)
        mask = mask_fn(kt)
        for i, (pair, head, _, _) in enumerate(streams):
            lg = _dot(k_ref[0, rows, pl.ds(pair * LANES, LANES)], q_scr[i])
            if band_t is not None:
                lg = lg + band_ref[head, band_t]
            if mask is not None:
                lg = lg + mask
            logit_ref[kt, i] = lg
            m_ref[i] = jnp.maximum(m_ref[i], jnp.max(lg.reshape(TK // SUBLANES, SUBLANES, TQ), axis=0))

    _for_tiles(jnp.maximum(qb - 1, 0), lambda kt: logits_tile(kt, None))

    @pl.when(qb >= 1)
    def _():
        logits_tile(qb - 1, 0)
        logits_tile(qb, 1)

    @pl.when(qb == 0)
    def _():
        logits_tile(qb, 1)

    m_fin = [jnp.max(m_ref[i], axis=0, keepdims=True) for i in range(len(streams))]
    acc_ref[...] = jnp.zeros(acc_ref.shape, F32)

    def values_tile(kt):
        for i, (_, head, _, _) in enumerate(streams):
            p = jnp.exp2(logit_ref[kt, i] - m_fin[i])
            acc_ref[i] += _dot(vt_ref[0, kt, pl.ds(head * VT_ROWS, VT_ROWS), :], p.astype(BF16))

    _for_tiles(qb + 1, values_tile)


def _normalised(acc_ref, i):
    a = acc_ref[i]
    return a[:HEAD_DIM] / a[HEAD_DIM:HEAD_DIM + 1]


_DSA_STREAMS = tuple((h // PAIR, h, (h % PAIR) * HEAD_DIM, (h % PAIR + 1) * HEAD_DIM) for h in range(DSA_HEADS))


def _f32_to_key(x):
    bits = lax.bitcast_convert_type(x, jnp.int32)
    return jnp.where(bits < 0, ~bits, bits ^ jnp.int32(-2 ** 31))


def _key_to_f32(key):
    bits = jnp.where(key < 0, key ^ jnp.int32(-2 ** 31), ~key)
    return lax.bitcast_convert_type(bits, F32)


def _dsa_kernel(tab_ref, qt_ref, k_ref, vt_ref, iqt_ref, ik_ref, iwt_ref, bidx_ref, tril_ref,
                o_ref, band_ref, score_ref, q_scr, logit_ref, m_ref, acc_ref, tiecnt_ref, tie_ref, stat_ref):
    qb = pl.program_id(1)
    n_tiles = qb + 1

    @pl.when((pl.program_id(0) == 0) & (qb == 0))
    def _():
        _build_band(bidx_ref, tab_ref, band_ref, DSA_HEADS)

    iw = iwt_ref[...] * (IDX_HEADS ** -0.5 * IDX_DIM ** -0.5)
    row = lax.broadcasted_iota(jnp.int32, (LANES, TQ), 0)
    iq = [jnp.where(row // HEAD_DIM == (h % PAIR), iqt_ref[pl.ds((h // PAIR) * LANES, LANES), :], 0)
          for h in range(IDX_HEADS)]

    def score_tile(kt):
        ki = ik_ref[0, pl.ds(pl.multiple_of(kt * TK, TK), TK), :]
        s = jnp.zeros((TK, TQ), F32)
        for h in range(IDX_HEADS):
            s = s + jnp.maximum(_dot(ki, iq[h]), 0.0) * iw[h:h + 1, :]
        return s

    stat_ref[0] = jnp.full((SUBLANES, TQ), -jnp.inf, F32)
    stat_ref[1] = jnp.full((SUBLANES, TQ), jnp.inf, F32)
    stat_ref[2] = jnp.zeros((SUBLANES, TQ), F32)
    stat_ref[3] = jnp.zeros((SUBLANES, TQ), F32)

    def add_stats(s_for_max, s_for_min):
        hi_part = s_for_max.reshape(TK // SUBLANES, SUBLANES, TQ)
        stat_ref[0] = jnp.maximum(stat_ref[0], jnp.max(hi_part, axis=0))
        stat_ref[1] = jnp.minimum(stat_ref[1], jnp.min(s_for_min.reshape(TK // SUBLANES, SUBLANES, TQ), axis=0))
        stat_ref[2] += jnp.sum(jnp.where(hi_part > 0.0, 1.0, 0.0), axis=0)
        stat_ref[3] += jnp.sum(jnp.where(hi_part >= 0.0, 1.0, 0.0), axis=0)

    def past_scores(kt, carry):
        s = score_tile(kt)
        score_ref[kt] = s
        add_stats(s, s)
        return carry

    lax.fori_loop(0, qb, past_scores, 0)
    s_diag = score_tile(qb)
    visible = band_ref[0, 1] > 0.5 * NEG
    score_ref[qb] = jnp.where(visible, s_diag, -jnp.inf)
    add_stats(jnp.where(visible, s_diag, -jnp.inf), jnp.where(visible, s_diag, jnp.inf))

    k_sel = float(DSA_TOPK_MAX)

    def count(pred):
        def body(kt, c):
            xs = score_ref[kt].reshape(TK // SUBLANES, SUBLANES, TQ)
            accs = [c[a] for a in range(N_ACC)]
            for g in range(TK // SUBLANES):
                a = accs[g % N_ACC]
                accs[g % N_ACC] = jnp.where(pred(xs[g]), a + 1.0, a)
            return jnp.stack(accs)
        c = lax.fori_loop(0, n_tiles, body, jnp.zeros((N_ACC, SUBLANES, TQ), F32))
        return jnp.sum(jnp.sum(c, axis=0), axis=0, keepdims=True)

    col = lax.broadcasted_iota(jnp.int32, (1, TQ), 1)
    limit = qb * TQ + (col // CHUNK + 1) * CHUNK
    select_all = limit <= DSA_TOPK_MAX
    s_max = jnp.max(stat_ref[0], axis=0, keepdims=True)
    s_min = jnp.min(stat_ref[1], axis=0, keepdims=True)
    n_pos = jnp.sum(stat_ref[2], axis=0, keepdims=True)
    n_nonneg = jnp.sum(stat_ref[3], axis=0, keepdims=True)

    positive = n_pos > k_sel
    zero_thr = (n_pos <= k_sel) & (k_sel <= n_nonneg)
    lo0 = jnp.where(positive, 0.0, _key_to_f32(_f32_to_key(s_min) - 1))
    hi0 = jnp.where(positive, _key_to_f32(_f32_to_key(s_max) + 1), 0.0)
    flo0 = jnp.where(positive, n_pos, limit.astype(F32)) - k_sel
    fhi0 = jnp.where(positive, 0.0, n_nonneg) - k_sel
    done0 = jnp.where(select_all | zero_thr, 1.0, 0.0)
    thr0 = jnp.where(select_all, -jnp.inf, 0.0)

    def search_step(vec):
        lo, hi, flo, fhi, side, done, thr = vec
        w = flo / (flo - fhi)
        t = lo * (1.0 - w) + hi * w
        t = jnp.where((t > lo) & (t < hi), t, 0.5 * lo + 0.5 * hi)
        splits = (t > lo) & (t < hi)
        f = count(lambda s: s >= t) - k_sel
        active = done == 0.0
        hit = active & splits & (f == 0.0)
        adjacent = active & ~splits
        thr = jnp.where(hit, t, jnp.where(adjacent, lo, thr))
        done = jnp.where(hit | adjacent, 1.0, done)
        up = active & splits & (f > 0.0)
        down = active & splits & (f < 0.0)
        fhi_new = jnp.where(up & (side > 0.0), 0.5 * fhi, jnp.where(down, f, fhi))
        flo_new = jnp.where(down & (side < 0.0), 0.5 * flo, jnp.where(up, f, flo))
        lo = jnp.where(up, t, lo)
        hi = jnp.where(down, t, hi)
        side = jnp.where(up, 1.0, jnp.where(down, -1.0, side))
        return lo, hi, flo_new, fhi_new, side, done, thr

    def open_queries(done):
        return (jnp.min(done) == 0.0).astype(jnp.int32)

    def search_round(state):
        rounds, _, vec = state
        for _ in range(SEARCH_STEPS_PER_CHECK):
            vec = search_step(vec)
        return rounds + 1, open_queries(vec[5]), vec

    vec0 = (lo0, hi0, flo0, fhi0, jnp.zeros((1, TQ), F32), done0, thr0)
    state = lax.while_loop(lambda st: (st[0] < SEARCH_ROUNDS) & (st[1] > 0), search_round,
                           (jnp.int32(0), open_queries(done0), vec0))
    done, thr = state[2][5], state[2][6]

    def bisect(i, prefix):
        cand = prefix | lax.shift_left(jnp.int32(1), 31 - i)
        cand_f = _key_to_f32(cand)
        take = count(lambda s: s >= cand_f) >= k_sel
        return jnp.where(take, cand, prefix)

    def exact_thr():
        prefix = lax.fori_loop(0, 32, bisect, jnp.zeros((1, TQ), jnp.int32))
        return jnp.where(done == 0.0, _key_to_f32(prefix), thr)

    thr = lax.cond(state[1] > 0, exact_thr, lambda: thr)
    n_gt = count(lambda s: s > thr)
    need = jnp.where(select_all, 0.0, k_sel - n_gt)

    def tie_counts(kt, carry):
        tied = jnp.where(score_ref[kt] == thr, 1.0, 0.0).reshape(TK // SUBLANES, SUBLANES, TQ)
        tiecnt_ref[kt] = jnp.sum(jnp.sum(tied, axis=0), axis=0, keepdims=True)
        return carry

    lax.fori_loop(0, n_tiles, tie_counts, 0)
    tie_ref[...] = jnp.zeros(tie_ref.shape, F32)
    tril = tril_ref[...]

    def selection_mask(kt):
        s = score_ref[kt]
        tied = s == thr
        tied_before = tie_ref[...]
        tie_ref[...] = tied_before + tiecnt_ref[kt]
        tied_upto = tied_before + _dot(tril, jnp.where(tied, 1.0, 0.0).astype(BF16))
        return jnp.where((s > thr) | (tied & (tied_upto <= need)), 0.0, NEG)

    _load_queries(qt_ref, q_scr, _DSA_STREAMS)
    _softmax_passes(qb, k_ref, vt_ref, q_scr, band_ref, selection_mask, logit_ref, m_ref, acc_ref, _DSA_STREAMS)

    for p in range(N_PAIRS):
        halves = [_normalised(acc_ref, PAIR * p + j) if PAIR * p + j < DSA_HEADS
                  else jnp.zeros((HEAD_DIM, TQ), F32) for j in range(PAIR)]
        o_ref[0, :, pl.ds(p * LANES, LANES)] = jnp.concatenate(halves, axis=0).T.astype(o_ref.dtype)


def _dsa(aqT, ak, avT, iqT, ik, iwT, bias_t, batch, seq):
    bidx = jnp.asarray(_band_bucket_index_t())
    tril = jnp.asarray(_prefix_tril(), BF16)
    nq = seq // TQ
    n_streams = len(_DSA_STREAMS)
    qcols = lambda w: pl.BlockSpec((w, TQ), lambda b, n: (0, b * nq + n))
    ktok = lambda w: pl.BlockSpec((1, seq, w), lambda b, n: (b, 0, 0))
    const = lambda a: pl.BlockSpec(a.shape, lambda b, n: (0,) * a.ndim)
    out = pl.pallas_call(
        _dsa_kernel,
        grid=(batch, nq),
        in_specs=[pl.BlockSpec(memory_space=pltpu.SMEM),
                  qcols(D_HEADS_PAD), ktok(D_HEADS_PAD),
                  pl.BlockSpec((1, seq // TK, DSA_HEADS * VT_ROWS, TK), lambda b, n: (b, 0, 0, 0)),
                  qcols(IDX_HEADS * IDX_DIM), ktok(LANES), qcols(_IW_ROWS), const(bidx), const(tril)],
        out_specs=pl.BlockSpec((1, TQ, D_HEADS_PAD), lambda b, n: (b, n, 0)),
        out_shape=jax.ShapeDtypeStruct((batch, seq, D_HEADS_PAD), BF16),
        scratch_shapes=[pltpu.VMEM((DSA_HEADS, 2, TK, TQ), F32),
                        pltpu.VMEM((seq // TK, TK, TQ), F32),
                        pltpu.VMEM((n_streams, LANES, TQ), BF16),
                        pltpu.VMEM((seq // TK, n_streams, TK, TQ), F32),
                        pltpu.VMEM((n_streams, SUBLANES, TQ), F32),
                        pltpu.VMEM((n_streams, VT_ROWS, TQ), F32),
                        pltpu.VMEM((seq // TK, 1, TQ), F32),
                        pltpu.VMEM((1, TQ), F32),
                        pltpu.VMEM((4, SUBLANES, TQ), F32)],
        compiler_params=_cparams(2),
        name="dsa_attention",
    )(bias_t, aqT, ak.reshape(batch, seq, D_HEADS_PAD),
      avT.reshape(batch, seq // TK, DSA_HEADS * VT_ROWS, TK), iqT, ik.reshape(batch, seq, LANES), iwT, bidx, tril)
    return out.reshape(batch * seq, D_HEADS_PAD)


_DIFF_STREAMS = tuple(
    (h // PAIR, h, (h % PAIR) * HEAD_DIM + mm * DIFF_QK_DIM, (h % PAIR) * HEAD_DIM + (mm + 1) * DIFF_QK_DIM)
    for h in range(DIFF_HEADS) for mm in range(2))


def _diff_kernel(lambda_init, tab_ref, qt_ref, k_ref, vt_ref, lam_ref, g_ref, bidx_ref, avg_ref,
                 o_ref, band_ref, q_scr, logit_ref, m_ref, acc_ref):
    qb = pl.program_id(1)

    @pl.when((pl.program_id(0) == 0) & (qb == 0))
    def _():
        _build_band(bidx_ref, tab_ref, band_ref, DIFF_HEADS)

    lv = lam_ref[...]
    lam = (jnp.exp(jnp.sum(lv[0:1] * lv[1:2], axis=-1, keepdims=True))
           - jnp.exp(jnp.sum(lv[2:3] * lv[3:4], axis=-1, keepdims=True)) + lambda_init)

    _load_queries(qt_ref, q_scr, _DIFF_STREAMS)
    _softmax_passes(qb, k_ref, vt_ref, q_scr, band_ref, lambda kt: None, logit_ref, m_ref, acc_ref,
                    _DIFF_STREAMS)

    avg = avg_ref[...]
    for p in range(N_PAIRS):
        halves = []
        for j in range(PAIR):
            h = PAIR * p + j
            if h < DIFF_HEADS:
                halves.append(_normalised(acc_ref, 2 * h) - lam * _normalised(acc_ref, 2 * h + 1))
            else:
                halves.append(jnp.zeros((HEAD_DIM, TQ), F32))
        out = jnp.concatenate(halves, axis=0).T
        ms = _group_mean(out * out, avg)
        out = out * lax.rsqrt(ms + LN_EPS) * g_ref[...] * (1.0 - lambda_init)
        o_ref[0, :, pl.ds(p * LANES, LANES)] = out.astype(o_ref.dtype)


def _diff(dqT, dk, dvT, lam_vecs, subln_g, bias_t, lambda_init, batch, seq):
    bidx = jnp.asarray(_band_bucket_index_t())
    avg = jnp.asarray(_group_avg(), BF16)
    g2 = jnp.concatenate([subln_g, subln_g]).reshape(1, LANES)
    nq = seq // TQ
    n_streams = len(_DIFF_STREAMS)
    const = lambda a: pl.BlockSpec(a.shape, lambda b, n: (0,) * a.ndim)
    out = pl.pallas_call(
        functools.partial(_diff_kernel, lambda_init),
        grid=(batch, nq),
        in_specs=[pl.BlockSpec(memory_space=pltpu.SMEM),
                  pl.BlockSpec((D_HEADS_PAD, TQ), lambda b, n: (0, b * nq + n)),
                  pl.BlockSpec((1, seq, D_HEADS_PAD), lambda b, n: (b, 0, 0)),
                  pl.BlockSpec((1, seq // TK, DIFF_HEADS * VT_ROWS, TK), lambda b, n: (b, 0, 0, 0)),
                  const(lam_vecs), const(g2), const(bidx), const(avg)],
        out_specs=pl.BlockSpec((1, TQ, D_HEADS_PAD), lambda b, n: (b, n, 0)),
        out_shape=jax.ShapeDtypeStruct((batch, seq, D_HEADS_PAD), BF16),
        scratch_shapes=[pltpu.VMEM((DIFF_HEADS, 2, TK, TQ), F32),
                        pltpu.VMEM((n_streams, LANES, TQ), BF16),
                        pltpu.VMEM((seq // TK, n_streams, TK, TQ), F32),
                        pltpu.VMEM((n_streams, SUBLANES, TQ), F32),
                        pltpu.VMEM((n_streams, VT_ROWS, TQ), F32)],
        compiler_params=_cparams(2),
        name="diff_attention",
    )(bias_t, dqT, dk.reshape(batch, seq, D_HEADS_PAD),
      dvT.reshape(batch, seq // TK, DIFF_HEADS * VT_ROWS, TK), lam_vecs, g2, bidx, avg)
    return out.reshape(batch * seq, D_HEADS_PAD)


def _out_ln_kernel(x_ref, ret_ref, dsa_ref, dif_ref, w_ref, g_ref, b_ref, o_ref):
    acc = _dot(ret_ref[...], w_ref[pl.ds(0, D_HEADS_PAD), :])
    acc = acc + _dot(dsa_ref[...], w_ref[pl.ds(D_HEADS_PAD, D_HEADS_PAD), :])
    acc = acc + _dot(dif_ref[...], w_ref[pl.ds(2 * D_HEADS_PAD, D_HEADS_PAD), :])
    r = DEEPNORM_ALPHA * x_ref[...] + acc
    o_ref[...] = _layer_norm(r, g_ref[...], b_ref[...])


def _pad_w_out(w_out):
    zeros = jnp.zeros((D_HEADS_PAD - D_DSA, w_out.shape[1]), w_out.dtype)
    return jnp.concatenate([w_out[:D_RET], w_out[D_RET:D_RET + D_DSA], zeros,
                            w_out[D_RET + D_DSA:], zeros], axis=0)


def _out_ln(x, ret, dsa, dif, w_pad, g, b):
    t = x.shape[0]
    full = lambda shape: pl.BlockSpec(shape, lambda i: (0, 0))
    tok = lambda w: pl.BlockSpec((TM_OUT, w), lambda i: (i, 0))
    return pl.pallas_call(
        _out_ln_kernel,
        grid=(t // TM_OUT,),
        in_specs=[tok(D_MODEL), tok(D_HEADS_PAD), tok(D_HEADS_PAD), tok(D_HEADS_PAD),
                  full((3 * D_HEADS_PAD, D_MODEL)), full((1, D_MODEL)), full((1, D_MODEL))],
        out_specs=tok(D_MODEL),
        out_shape=jax.ShapeDtypeStruct((t, D_MODEL), F32),
        compiler_params=_cparams(1),
        name="out_proj_ln",
    )(x, ret, dsa, dif, w_pad, g, b)


def kernel(x, w_in, w_out, ffn1_wg, ffn1_wu, ffn1_wd, ffn2_wg, ffn2_wu, ffn2_wd,
           ln_g, ln_b, diff_lambda, diff_subln_g, rel_bias):
    batch, seq, d = x.shape
    assert d == D_MODEL and seq % TQ == 0 and seq % TR == 0 and (batch * seq) % TM_FFN == 0
    assert min(DSA_TOPK_MAX, seq // 4) == DSA_TOPK_MAX
    h = x.reshape(batch * seq, d)
    bias_t = rel_bias.T
    row = lambda a: a.reshape(1, D_MODEL)
    for l in range(DEPTH):
        lambda_init = 0.8 - 0.6 * math.exp(-0.3 * l)
        h = _ffn_ln(h, ffn1_wg, ffn1_wu, ffn1_wd, row(ln_g[l, 0]), row(ln_b[l, 0]), l)
        w_tok, w_t = _split_w_in(w_in[l])
        (rq, rk, rv, rg, ak, dk, ik, aqT, dqT, iqT, iwT, avT, dvT) = _proj(
            h, w_tok.astype(BF16), w_t.astype(BF16))
        ret = _retention(rq, rk, rv, rg, batch, seq)
        dsa = _dsa(aqT, ak, avT, iqT, ik, iwT, bias_t[:DSA_HEADS], batch, seq)
        dif = _diff(dqT, dk, dvT, diff_lambda[l], diff_subln_g[l], bias_t[DSA_HEADS:], lambda_init, batch, seq)
        h = _out_ln(h, ret, dsa, dif, _pad_w_out(w_out[l]).astype(BF16), row(ln_g[l, 1]), row(ln_b[l, 1]))
        h = _ffn_ln(h, ffn2_wg, ffn2_wu, ffn2_wd, row(ln_g[l, 2]), row(ln_b[l, 2]), l)
    return h.reshape(batch, seq, d)
```

```python
import functools
import math

import numpy as np
import jax
import jax.numpy as jnp
from jax import lax
from jax.experimental import pallas as pl
from jax.experimental.pallas import tpu as pltpu

D_MODEL = 1024
DEPTH = 2
CHUNK = 64
HEAD_DIM = 64
RET_HEADS = 6
DSA_HEADS = 5
IDX_HEADS = 4
IDX_DIM = 64
DSA_TOPK_MAX = 256
DIFF_HEADS = 5
DIFF_QK_DIM = HEAD_DIM // 2
D_RET = RET_HEADS * HEAD_DIM
D_DSA = DSA_HEADS * HEAD_DIM
D_DIFF = DIFF_HEADS * HEAD_DIM
D_FF = 2816
N_BUCKETS = 32
MAX_DISTANCE = 128
ROPE_BASE = 10000.0
LN_EPS = 1e-5
HEAD_NORM_EPS = 1e-6
DEEPNORM_ALPHA = (2 * DEPTH) ** 0.25

LANES = 128
SUBLANES = 8
BF16_ROWS = 16
PAIR = LANES // HEAD_DIM
VMEM_LIMIT_BYTES = 56 * 1024 * 1024

D_HEADS_PAD = 3 * LANES
N_PAIRS = D_HEADS_PAD // LANES
VT_ROWS = HEAD_DIM + BF16_ROWS
TQ = 256
TK = 256
TR = 256
TM_FFN = 512
TF = 256
TM_PROJ = 512
N_ACC = 4
SEARCH_STEPS_PER_CHECK = 4
SEARCH_ROUNDS = 6
NEG = -1e30
LOG2E = math.log2(math.e)

F32 = jnp.float32
BF16 = jnp.bfloat16


def _dot(a, b):
    return jnp.dot(a, b, preferred_element_type=F32)


def _dot_nt(a, b):
    return lax.dot_general(a, b, (((1,), (1,)), ((), ())), preferred_element_type=F32)


def _layer_norm(r, g, b):
    mu = jnp.mean(r, axis=-1, keepdims=True)
    d = r - mu
    var = jnp.mean(d * d, axis=-1, keepdims=True)
    return d * lax.rsqrt(var + LN_EPS) * g + b


def _group_mean(x, avg_bf16):
    hi = x.astype(BF16)
    lo = (x - hi.astype(F32)).astype(BF16)
    return _dot(hi, avg_bf16) + _dot(lo, avg_bf16)


def _cparams(n_axes):
    return pltpu.CompilerParams(
        dimension_semantics=("arbitrary",) * n_axes,
        vmem_limit_bytes=VMEM_LIMIT_BYTES,
    )


def _rel_bucket_static(rel):
    rel = np.asarray(rel, dtype=np.int64)
    half = N_BUCKETS // 2
    max_exact = half // 2
    n = np.abs(rel)
    large = np.full(n.shape, max_exact, dtype=np.int64)
    for k in range(1, 64):
        large = np.where(64 * (2 ** k) <= n * n, max_exact + k, large)
    large = np.minimum(large, half - 1)
    return (np.where(rel > 0, half, 0) + np.where(n < max_exact, n, large)).astype(np.int32)


FAR_BUCKET = int(_rel_bucket_static(-(TK + 1)))


@functools.lru_cache(maxsize=None)
def _band_bucket_index_t():
    j = np.arange(TK)[:, None]
    i = np.arange(TQ)[None, :]
    prev = _rel_bucket_static(j - TK - i)
    diag = _rel_bucket_static(j - i)
    visible = j < (i // CHUNK + 1) * CHUNK
    diag = np.where(visible, diag, -1)
    return np.stack([prev, diag]).astype(np.int32)


@functools.lru_cache(maxsize=None)
def _rotary_tables(seq):
    lane = np.arange(LANES)
    d = lane % HEAD_DIM
    f = d % (HEAD_DIM // 2)
    inv = ROPE_BASE ** (-(2.0 * f) / HEAD_DIM)
    ang = np.arange(seq, dtype=np.float64)[:, None] * inv[None, :]
    cos = np.cos(ang)
    sin = np.sin(ang) * np.where(d < HEAD_DIM // 2, -1.0, 1.0)[None, :]
    return cos.astype(np.float32), sin.astype(np.float32)


@functools.lru_cache(maxsize=None)
def _retention_tables():
    gamma = 1.0 - 2.0 ** (-5.0 - np.arange(RET_HEADS, dtype=np.float64))
    t = np.arange(TR)
    dist = np.abs(t[:, None] - t[None, :]).astype(np.float64)
    visible = t[None, :] < (t[:, None] // CHUNK + 1) * CHUNK
    intra = np.stack([np.where(visible, g ** dist, 0.0) for g in gamma])
    lane_head = np.arange(D_HEADS_PAD) // HEAD_DIM
    g_lane = gamma[lane_head]
    qdec = g_lane[None, :] ** (t[:, None] + 1.0)
    kdec = g_lane[None, :] ** (TR - 1.0 - t[:, None])
    a = np.arange(LANES)
    same = (a[:, None] // HEAD_DIM) == (a[None, :] // HEAD_DIM)
    sdec = np.stack([np.where(same, (gamma[PAIR * p + a // HEAD_DIM] ** TR)[:, None], 0.0)
                     for p in range(RET_HEADS // PAIR)])
    return (intra.astype(np.float32), qdec.astype(np.float32), kdec.astype(np.float32),
            sdec.astype(np.float32), same.astype(np.float32))


@functools.lru_cache(maxsize=None)
def _group_avg():
    a = np.arange(LANES)
    same = (a[:, None] // HEAD_DIM) == (a[None, :] // HEAD_DIM)
    return np.where(same, 1.0 / HEAD_DIM, 0.0).astype(np.float32)


@functools.lru_cache(maxsize=None)
def _prefix_tril():
    a = np.arange(TK)
    return (a[None, :] <= a[:, None]).astype(np.float32)


def _lane_group(shape):
    return lax.broadcasted_iota(jnp.int32, shape, len(shape) - 1) // HEAD_DIM


def _ffn_ln_body(x, wg_ref, wu_ref, wd_ref, g, b):
    xb = x.astype(BF16)
    acc = jnp.zeros(x.shape, F32)
    for c in range(D_FF // TF):
        cols = pl.ds(c * TF, TF)
        gate = _dot(xb, wg_ref[:, cols].astype(BF16))
        up = _dot(xb, wu_ref[:, cols].astype(BF16))
        h = gate * jax.nn.sigmoid(gate) * up
        acc = acc + _dot(h.astype(BF16), wd_ref[cols, :].astype(BF16))
    return _layer_norm(DEEPNORM_ALPHA * x + 0.5 * acc, g, b)


def _ffn_ln_kernel(x_ref, wg_ref, wu_ref, wd_ref, g_ref, b_ref, o_ref):
    o_ref[...] = _ffn_ln_body(x_ref[...], wg_ref, wu_ref, wd_ref, g_ref[...], b_ref[...])


def _ffn_ln(x, wg, wu, wd, g, b, layer):
    t = x.shape[0]
    stacked = lambda shape: pl.BlockSpec((None,) + shape, lambda i: (layer, 0, 0), pipeline_mode=pl.Buffered(1))
    full = lambda shape: pl.BlockSpec(shape, lambda i: (0, 0))
    return pl.pallas_call(
        _ffn_ln_kernel,
        grid=(t // TM_FFN,),
        in_specs=[pl.BlockSpec((TM_FFN, D_MODEL), lambda i: (i, 0)),
                  stacked((D_MODEL, D_FF)), stacked((D_MODEL, D_FF)), stacked((D_FF, D_MODEL)),
                  full((1, D_MODEL)), full((1, D_MODEL))],
        out_specs=pl.BlockSpec((TM_FFN, D_MODEL), lambda i: (i, 0)),
        out_shape=jax.ShapeDtypeStruct((t, D_MODEL), F32),
        compiler_params=_cparams(1),
        name="ffn_ln",
    )(x, wg, wu, wd, g, b)


_IW_ROWS = BF16_ROWS
_TOK_SEGMENTS = (("rq", D_HEADS_PAD, F32), ("rk", D_HEADS_PAD, F32), ("rv", D_HEADS_PAD, BF16),
                 ("rg", D_HEADS_PAD, F32), ("ak", D_HEADS_PAD, BF16), ("dk", D_HEADS_PAD, BF16),
                 ("ik", LANES, BF16))
_T_SEGMENTS = (("aqT", D_HEADS_PAD, BF16, HEAD_DIM ** -0.5 * LOG2E),
               ("dqT", D_HEADS_PAD, BF16, DIFF_QK_DIM ** -0.5 * LOG2E),
               ("iqT", IDX_HEADS * IDX_DIM, BF16, None), ("iwT", _IW_ROWS, F32, None))
_VT_SEGMENTS = (("avT", DSA_HEADS), ("dvT", DIFF_HEADS))
N_TOK = sum(w for _, w, _ in _TOK_SEGMENTS)
N_T = sum(s[1] for s in _T_SEGMENTS) + sum(n * HEAD_DIM for _, n in _VT_SEGMENTS)


def _split_w_in(w_in):
    splits = (D_RET, D_RET, D_RET, D_RET, D_DSA, D_DSA, D_DSA,
              IDX_HEADS * IDX_DIM, IDX_DIM, IDX_HEADS, D_DIFF, D_DIFF, D_DIFF)
    offs = np.cumsum((0,) + splits)
    (rq, rk, rv, rg, aq, ak, av, iq, ik, iw, dq, dk, dv) = [
        w_in[:, offs[i]:offs[i + 1]] for i in range(len(splits))]
    zeros = lambda n: jnp.zeros((w_in.shape[0], n), w_in.dtype)
    pad = lambda a, n: jnp.concatenate([a, zeros(n - a.shape[1])], axis=1)
    w_tok = jnp.concatenate([rq, rk, rv, rg, pad(ak, D_HEADS_PAD), pad(dk, D_HEADS_PAD),
                             ik, ik], axis=1)
    w_t = jnp.concatenate([pad(aq, D_HEADS_PAD), pad(dq, D_HEADS_PAD), iq, pad(iw, _IW_ROWS),
                           av, dv], axis=1).T
    return w_tok, w_t


def _proj_kernel(x_ref, wtok_ref, wt_ref, *o_refs):
    xb = x_ref[...].astype(BF16)
    refs = list(o_refs)
    off = 0
    for _, width, dtype in _TOK_SEGMENTS:
        refs.pop(0)[...] = _dot(xb, wtok_ref[:, pl.ds(off, width)]).astype(dtype)
        off += width
    off = 0
    for _, width, dtype, scale in _T_SEGMENTS:
        res = _dot_nt(wt_ref[pl.ds(off, width), :], xb)
        refs.pop(0)[...] = (res if scale is None else res * scale).astype(dtype)
        off += width
    ones = jnp.ones((BF16_ROWS, TK), BF16)
    for _, n_heads in _VT_SEGMENTS:
        o_ref = refs.pop(0)
        res = _dot_nt(wt_ref[pl.ds(off, n_heads * HEAD_DIM), :], xb).astype(BF16)
        for t in range(TM_PROJ // TK):
            for h in range(n_heads):
                o_ref[t, pl.ds(h * VT_ROWS, HEAD_DIM), :] = res[h * HEAD_DIM:(h + 1) * HEAD_DIM,
                                                                t * TK:(t + 1) * TK]
                o_ref[t, pl.ds(h * VT_ROWS + HEAD_DIM, BF16_ROWS), :] = ones
        off += n_heads * HEAD_DIM


def _proj(x, w_tok, w_t):
    t = x.shape[0]
    out_specs = ([pl.BlockSpec((TM_PROJ, w), lambda i: (i, 0)) for _, w, _ in _TOK_SEGMENTS]
                 + [pl.BlockSpec((s[1], TM_PROJ), lambda i: (0, i)) for s in _T_SEGMENTS]
                 + [pl.BlockSpec((TM_PROJ // TK, n * VT_ROWS, TK), lambda i: (i, 0, 0)) for _, n in _VT_SEGMENTS])
    out_shape = ([jax.ShapeDtypeStruct((t, w), dt) for _, w, dt in _TOK_SEGMENTS]
                 + [jax.ShapeDtypeStruct((s[1], t), s[2]) for s in _T_SEGMENTS]
                 + [jax.ShapeDtypeStruct((t // TK, n * VT_ROWS, TK), BF16) for _, n in _VT_SEGMENTS])
    return pl.pallas_call(
        _proj_kernel,
        grid=(t // TM_PROJ,),
        in_specs=[pl.BlockSpec((TM_PROJ, D_MODEL), lambda i: (i, 0)),
                  pl.BlockSpec((D_MODEL, N_TOK), lambda i: (0, 0)),
                  pl.BlockSpec((N_T, D_MODEL), lambda i: (0, 0))],
        out_specs=out_specs,
        out_shape=out_shape,
        compiler_params=_cparams(1),
        name="in_proj",
    )(x, w_tok, w_t)


def _swap_halves(x):
    lane = lax.broadcasted_iota(jnp.int32, x.shape, 1)
    first_half = (lane % HEAD_DIM) < (HEAD_DIM // 2)
    from_right = pltpu.roll(x, LANES - HEAD_DIM // 2, 1)
    from_left = pltpu.roll(x, HEAD_DIM // 2, 1)
    return jnp.where(first_half, from_right, from_left)


def _retention_kernel(q_ref, k_ref, v_ref, g_ref, cos_ref, sin_ref, intra_ref, qdec_ref, kdec_ref,
                      sdec_ref, same_ref, avg_ref, o_ref, state_ref):
    @pl.when(pl.program_id(1) == 0)
    def _():
        state_ref[...] = jnp.zeros(state_ref.shape, F32)

    cos = cos_ref[...]
    sin = sin_ref[...]
    avg = avg_ref[...]
    group = _lane_group((TR, LANES))
    for p in range(RET_HEADS // PAIR):
        lanes = pl.ds(p * LANES, LANES)
        q = q_ref[0, :, lanes]
        k = k_ref[0, :, lanes]
        q = (q * cos + _swap_halves(q) * sin) * (HEAD_DIM ** -0.5)
        k = k * cos + _swap_halves(k) * sin
        v = v_ref[0, :, lanes]
        kb = k.astype(BF16)
        intra = jnp.zeros((TR, LANES), F32)
        for j in range(PAIR):
            qh = jnp.where(group == j, q, 0.0).astype(BF16)
            scores = _dot_nt(qh, kb) * intra_ref[PAIR * p + j]
            oh = _dot(scores.astype(BF16), v)
            intra = jnp.where(group == j, oh, intra)
        state = state_ref[p]
        cross = _dot((q * qdec_ref[:, lanes]).astype(BF16), state.astype(BF16))
        o = intra + cross
        kd_t = (k * kdec_ref[:, lanes]).T.astype(BF16)
        state_ref[p] = state * sdec_ref[p] + _dot(kd_t, v) * same_ref[...]
        mu = _group_mean(o, avg)
        d = o - mu
        var = _group_mean(d * d, avg)
        normed = d * lax.rsqrt(var + HEAD_NORM_EPS)
        gate = g_ref[0, :, lanes]
        o_ref[0, :, lanes] = (gate * jax.nn.sigmoid(gate) * normed).astype(o_ref.dtype)


def _retention(rq, rk, rv, rg, batch, seq):
    cos, sin = _rotary_tables(seq)
    intra, qdec, kdec, sdec, same = _retention_tables()
    avg = jnp.asarray(_group_avg(), BF16)
    r3 = lambda a: a.reshape(batch, seq, D_HEADS_PAD)
    tok = pl.BlockSpec((1, TR, D_HEADS_PAD), lambda b, n: (b, n, 0))
    const = lambda a: pl.BlockSpec(a.shape, lambda b, n: (0,) * a.ndim)
    pos = pl.BlockSpec((TR, LANES), lambda b, n: (n, 0))
    out = pl.pallas_call(
        _retention_kernel,
        grid=(batch, seq // TR),
        in_specs=[tok, tok, tok, tok, pos, pos, const(intra), const(qdec), const(kdec),
                  const(sdec), const(same), const(avg)],
        out_specs=tok,
        out_shape=jax.ShapeDtypeStruct((batch, seq, D_HEADS_PAD), BF16),
        scratch_shapes=[pltpu.VMEM((RET_HEADS // PAIR, LANES, LANES), F32)],
        compiler_params=_cparams(2),
        name="retention",
    )(r3(rq), r3(rk), r3(rv), r3(rg), cos, sin, intra, qdec, kdec, sdec, same, avg)
    return out.reshape(batch * seq, D_HEADS_PAD)


def _build_band(bidx_ref, tab_ref, band_ref, n_heads):
    for h in range(n_heads):
        far = tab_ref[h, FAR_BUCKET]
        for t in range(2):
            idx = bidx_ref[t]
            val = jnp.full((TK, TQ), NEG, F32)
            for bkt in range(N_BUCKETS):
                val = jnp.where(idx == bkt, (tab_ref[h, bkt] - far) * LOG2E, val)
            band_ref[h, t] = val


def _load_queries(qt_ref, q_scr, streams):
    row = lax.broadcasted_iota(jnp.int32, (LANES, TQ), 0)
    for i, (pair, _, lo, hi) in enumerate(streams):
        q_scr[i] = jnp.where((row >= lo) & (row < hi), qt_ref[pl.ds(pair * LANES, LANES), :], 0)


def _for_tiles(n, fn):
    def pair(j, carry):
        fn(2 * j)
        fn(2 * j + 1)
        return carry

    lax.fori_loop(0, lax.shift_right_logical(n, 1), pair, 0)

    @pl.when(lax.rem(n, 2) == 1)
    def _():
        fn(n - 1)


def _softmax_passes(qb, k_ref, vt_ref, q_scr, band_ref, mask_fn, logit_ref, m_ref, acc_ref, streams):
    m_ref[...] = jnp.full(m_ref.shape, NEG, F32)

    def logits_tile(kt, band_t):
        rows = pl.ds(pl.multiple_of(kt * TK, TK), TK)
        mask = mask_fn(kt)
        for i, (pair, head, _, _) in enumerate(streams):
            lg = _dot(k_ref[0, rows, pl.ds(pair * LANES, LANES)], q_scr[i])
            if band_t is not None:
                lg = lg + band_ref[head, band_t]
            if mask is not None:
                lg = lg + mask
            logit_ref[kt, i] = lg
            m_ref[i] = jnp.maximum(m_ref[i], jnp.max(lg.reshape(TK // SUBLANES, SUBLANES, TQ), axis=0))

    _for_tiles(jnp.maximum(qb - 1, 0), lambda kt: logits_tile(kt, None))

    @pl.when(qb >= 1)
    def _():
        logits_tile(qb - 1, 0)
        logits_tile(qb, 1)

    @pl.when(qb == 0)
    def _():
        logits_tile(qb, 1)

    m_fin = [jnp.max(m_ref[i], axis=0, keepdims=True) for i in range(len(streams))]
    acc_ref[...] = jnp.zeros(acc_ref.shape, F32)

    def values_tile(kt):
        for i, (_, head, _, _) in enumerate(streams):
            p = jnp.exp2(logit_ref[kt, i] - m_fin[i])
            acc_ref[i] += _dot(vt_ref[0, kt, pl.ds(head * VT_ROWS, VT_ROWS), :], p.astype(BF16))

    _for_tiles(qb + 1, values_tile)


def _normalised(acc_ref, i):
    a = acc_ref[i]
    return a[:HEAD_DIM] / a[HEAD_DIM:HEAD_DIM + 1]


_DSA_STREAMS = tuple((h // PAIR, h, (h % PAIR) * HEAD_DIM, (h % PAIR + 1) * HEAD_DIM) for h in range(DSA_HEADS))


def _f32_to_key(x):
    bits = lax.bitcast_convert_type(x, jnp.int32)
    return jnp.where(bits < 0, ~bits, bits ^ jnp.int32(-2 ** 31))


def _key_to_f32(key):
    bits = jnp.where(key < 0, key ^ jnp.int32(-2 ** 31), ~key)
    return lax.bitcast_convert_type(bits, F32)


def _dsa_kernel(tab_ref, qt_ref, k_ref, vt_ref, iqt_ref, ik_ref, iwt_ref, bidx_ref, tril_ref,
                o_ref, band_ref, score_ref, q_scr, logit_ref, m_ref, acc_ref, tiecnt_ref, tie_ref, stat_ref):
    qb = pl.program_id(1)
    n_tiles = qb + 1

    @pl.when((pl.program_id(0) == 0) & (qb == 0))
    def _():
        _build_band(bidx_ref, tab_ref, band_ref, DSA_HEADS)

    iw = iwt_ref[...] * (IDX_HEADS ** -0.5 * IDX_DIM ** -0.5)
    row = lax.broadcasted_iota(jnp.int32, (LANES, TQ), 0)
    iq = [jnp.where(row // HEAD_DIM == (h % PAIR), iqt_ref[pl.ds((h // PAIR) * LANES, LANES), :], 0)
          for h in range(IDX_HEADS)]

    def score_tile(kt):
        ki = ik_ref[0, pl.ds(pl.multiple_of(kt * TK, TK), TK), :]
        s = jnp.zeros((TK, TQ), F32)
        for h in range(IDX_HEADS):
            s = s + jnp.maximum(_dot(ki, iq[h]), 0.0) * iw[h:h + 1, :]
        return s

    stat_ref[0] = jnp.full((SUBLANES, TQ), -jnp.inf, F32)
    stat_ref[1] = jnp.full((SUBLANES, TQ), jnp.inf, F32)
    stat_ref[2] = jnp.zeros((SUBLANES, TQ), F32)
    stat_ref[3] = jnp.zeros((SUBLANES, TQ), F32)

    def add_stats(s_for_max, s_for_min):
        hi_part = s_for_max.reshape(TK // SUBLANES, SUBLANES, TQ)
        stat_ref[0] = jnp.maximum(stat_ref[0], jnp.max(hi_part, axis=0))
        stat_ref[1] = jnp.minimum(stat_ref[1], jnp.min(s_for_min.reshape(TK // SUBLANES, SUBLANES, TQ), axis=0))
        stat_ref[2] += jnp.sum(jnp.where(hi_part > 0.0, 1.0, 0.0), axis=0)
        stat_ref[3] += jnp.sum(jnp.where(hi_part >= 0.0, 1.0, 0.0), axis=0)

    def past_scores(kt):
        s = score_tile(kt)
        score_ref[kt] = s
        add_stats(s, s)

    _for_tiles(qb, past_scores)
    s_diag = score_tile(qb)
    visible = band_ref[0, 1] > 0.5 * NEG
    score_ref[qb] = jnp.where(visible, s_diag, -jnp.inf)
    add_stats(jnp.where(visible, s_diag, -jnp.inf), jnp.where(visible, s_diag, jnp.inf))

    k_sel = float(DSA_TOPK_MAX)

    def count(pred):
        def body(kt, c):
            xs = score_ref[kt].reshape(TK // SUBLANES, SUBLANES, TQ)
            accs = [c[a] for a in range(N_ACC)]
            for g in range(TK // SUBLANES):
                a = accs[g % N_ACC]
                accs[g % N_ACC] = jnp.where(pred(xs[g]), a + 1.0, a)
            return jnp.stack(accs)
        c = lax.fori_loop(0, n_tiles, body, jnp.zeros((N_ACC, SUBLANES, TQ), F32))
        return jnp.sum(jnp.sum(c, axis=0), axis=0, keepdims=True)

    col = lax.broadcasted_iota(jnp.int32, (1, TQ), 1)
    limit = qb * TQ + (col // CHUNK + 1) * CHUNK
    select_all = limit <= DSA_TOPK_MAX
    s_max = jnp.max(stat_ref[0], axis=0, keepdims=True)
    s_min = jnp.min(stat_ref[1], axis=0, keepdims=True)
    n_pos = jnp.sum(stat_ref[2], axis=0, keepdims=True)
    n_nonneg = jnp.sum(stat_ref[3], axis=0, keepdims=True)

    positive = n_pos > k_sel
    zero_thr = (n_pos <= k_sel) & (k_sel <= n_nonneg)
    lo0 = jnp.where(positive, 0.0, _key_to_f32(_f32_to_key(s_min) - 1))
    hi0 = jnp.where(positive, _key_to_f32(_f32_to_key(s_max) + 1), 0.0)
    flo0 = jnp.where(positive, n_pos, limit.astype(F32)) - k_sel
    fhi0 = jnp.where(positive, 0.0, n_nonneg) - k_sel
    done0 = jnp.where(select_all | zero_thr, 1.0, 0.0)
    thr0 = jnp.where(select_all, -jnp.inf, 0.0)

    def search_step(vec):
        lo, hi, flo, fhi, side, done, thr = vec
        w = flo / (flo - fhi)
        t = lo * (1.0 - w) + hi * w
        t = jnp.where((t > lo) & (t < hi), t, 0.5 * lo + 0.5 * hi)
        splits = (t > lo) & (t < hi)
        f = count(lambda s: s >= t) - k_sel
        active = done == 0.0
        hit = active & splits & (f == 0.0)
        adjacent = active & ~splits
        thr = jnp.where(hit, t, jnp.where(adjacent, lo, thr))
        done = jnp.where(hit | adjacent, 1.0, done)
        up = active & splits & (f > 0.0)
        down = active & splits & (f < 0.0)
        fhi_new = jnp.where(up & (side > 0.0), 0.5 * fhi, jnp.where(down, f, fhi))
        flo_new = jnp.where(down & (side < 0.0), 0.5 * flo, jnp.where(up, f, flo))
        lo = jnp.where(up, t, lo)
        hi = jnp.where(down, t, hi)
        side = jnp.where(up, 1.0, jnp.where(down, -1.0, side))
        return lo, hi, flo_new, fhi_new, side, done, thr

    def open_queries(done):
        return (jnp.min(done) == 0.0).astype(jnp.int32)

    def search_round(state):
        rounds, _, vec = state
        for _ in range(SEARCH_STEPS_PER_CHECK):
            vec = search_step(vec)
        return rounds + 1, open_queries(vec[5]), vec

    vec0 = (lo0, hi0, flo0, fhi0, jnp.zeros((1, TQ), F32), done0, thr0)
    state = lax.while_loop(lambda st: (st[0] < SEARCH_ROUNDS) & (st[1] > 0), search_round,
                           (jnp.int32(0), open_queries(done0), vec0))
    done, thr = state[2][5], state[2][6]

    def bisect(i, prefix):
        cand = prefix | lax.shift_left(jnp.int32(1), 31 - i)
        cand_f = _key_to_f32(cand)
        take = count(lambda s: s >= cand_f) >= k_sel
        return jnp.where(take, cand, prefix)

    def exact_thr():
        prefix = lax.fori_loop(0, 32, bisect, jnp.zeros((1, TQ), jnp.int32))
        return jnp.where(done == 0.0, _key_to_f32(prefix), thr)

    thr = lax.cond(state[1] > 0, exact_thr, lambda: thr)
    n_gt = count(lambda s: s > thr)
    need = jnp.where(select_all, 0.0, k_sel - n_gt)

    def tie_counts(kt, carry):
        tied = jnp.where(score_ref[kt] == thr, 1.0, 0.0).reshape(TK // SUBLANES, SUBLANES, TQ)
        tiecnt_ref[kt] = jnp.sum(jnp.sum(tied, axis=0), axis=0, keepdims=True)
        return carry

    lax.fori_loop(0, n_tiles, tie_counts, 0)
    tie_ref[...] = jnp.zeros(tie_ref.shape, F32)
    tril = tril_ref[...]

    def selection_mask(kt):
        s = score_ref[kt]
        tied = s == thr
        tied_before = tie_ref[...]
        tie_ref[...] = tied_before + tiecnt_ref[kt]
        tied_upto = tied_before + _dot(tril, jnp.where(tied, 1.0, 0.0).astype(BF16))
        return jnp.where((s > thr) | (tied & (tied_upto <= need)), 0.0, NEG)

    _load_queries(qt_ref, q_scr, _DSA_STREAMS)
    _softmax_passes(qb, k_ref, vt_ref, q_scr, band_ref, selection_mask, logit_ref, m_ref, acc_ref, _DSA_STREAMS)

    for p in range(N_PAIRS):
        halves = [_normalised(acc_ref, PAIR * p + j) if PAIR * p + j < DSA_HEADS
                  else jnp.zeros((HEAD_DIM, TQ), F32) for j in range(PAIR)]
        o_ref[0, :, pl.ds(p * LANES, LANES)] = jnp.concatenate(halves, axis=0).T.astype(o_ref.dtype)


def _dsa(aqT, ak, avT, iqT, ik, iwT, bias_t, batch, seq):
    bidx = jnp.asarray(_band_bucket_index_t())
    tril = jnp.asarray(_prefix_tril(), BF16)
    nq = seq // TQ
    n_streams = len(_DSA_STREAMS)
    qcols = lambda w: pl.BlockSpec((w, TQ), lambda b, n: (0, b * nq + n))
    ktok = lambda w: pl.BlockSpec((1, seq, w), lambda b, n: (b, 0, 0))
    const = lambda a: pl.BlockSpec(a.shape, lambda b, n: (0,) * a.ndim)
    out = pl.pallas_call(
        _dsa_kernel,
        grid=(batch, nq),
        in_specs=[pl.BlockSpec(memory_space=pltpu.SMEM),
                  qcols(D_HEADS_PAD), ktok(D_HEADS_PAD),
                  pl.BlockSpec((1, seq // TK, DSA_HEADS * VT_ROWS, TK), lambda b, n: (b, 0, 0, 0)),
                  qcols(IDX_HEADS * IDX_DIM), ktok(LANES), qcols(_IW_ROWS), const(bidx), const(tril)],
        out_specs=pl.BlockSpec((1, TQ, D_HEADS_PAD), lambda b, n: (b, n, 0)),
        out_shape=jax.ShapeDtypeStruct((batch, seq, D_HEADS_PAD), BF16),
        scratch_shapes=[pltpu.VMEM((DSA_HEADS, 2, TK, TQ), F32),
                        pltpu.VMEM((seq // TK, TK, TQ), F32),
                        pltpu.VMEM((n_streams, LANES, TQ), BF16),
                        pltpu.VMEM((seq // TK, n_streams, TK, TQ), F32),
                        pltpu.VMEM((n_streams, SUBLANES, TQ), F32),
                        pltpu.VMEM((n_streams, VT_ROWS, TQ), F32),
                        pltpu.VMEM((seq // TK, 1, TQ), F32),
                        pltpu.VMEM((1, TQ), F32),
                        pltpu.VMEM((4, SUBLANES, TQ), F32)],
        compiler_params=_cparams(2),
        name="dsa_attention",
    )(bias_t, aqT, ak.reshape(batch, seq, D_HEADS_PAD),
      avT.reshape(batch, seq // TK, DSA_HEADS * VT_ROWS, TK), iqT, ik.reshape(batch, seq, LANES), iwT, bidx, tril)
    return out.reshape(batch * seq, D_HEADS_PAD)


_DIFF_STREAMS = tuple(
    (h // PAIR, h, (h % PAIR) * HEAD_DIM + mm * DIFF_QK_DIM, (h % PAIR) * HEAD_DIM + (mm + 1) * DIFF_QK_DIM)
    for h in range(DIFF_HEADS) for mm in range(2))


def _diff_kernel(lambda_init, tab_ref, qt_ref, k_ref, vt_ref, lam_ref, g_ref, bidx_ref, avg_ref,
                 o_ref, band_ref, q_scr, logit_ref, m_ref, acc_ref):
    qb = pl.program_id(1)

    @pl.when((pl.program_id(0) == 0) & (qb == 0))
    def _():
        _build_band(bidx_ref, tab_ref, band_ref, DIFF_HEADS)

    lv = lam_ref[...]
    lam = (jnp.exp(jnp.sum(lv[0:1] * lv[1:2], axis=-1, keepdims=True))
           - jnp.exp(jnp.sum(lv[2:3] * lv[3:4], axis=-1, keepdims=True)) + lambda_init)

    _load_queries(qt_ref, q_scr, _DIFF_STREAMS)
    _softmax_passes(qb, k_ref, vt_ref, q_scr, band_ref, lambda kt: None, logit_ref, m_ref, acc_ref,
                    _DIFF_STREAMS)

    avg = avg_ref[...]
    for p in range(N_PAIRS):
        halves = []
        for j in range(PAIR):
            h = PAIR * p + j
            if h < DIFF_HEADS:
                halves.append(_normalised(acc_ref, 2 * h) - lam * _normalised(acc_ref, 2 * h + 1))
            else:
                halves.append(jnp.zeros((HEAD_DIM, TQ), F32))
        out = jnp.concatenate(halves, axis=0).T
        ms = _group_mean(out * out, avg)
        out = out * lax.rsqrt(ms + LN_EPS) * g_ref[...] * (1.0 - lambda_init)
        o_ref[0, :, pl.ds(p * LANES, LANES)] = out.astype(o_ref.dtype)


def _diff(dqT, dk, dvT, lam_vecs, subln_g, bias_t, lambda_init, batch, seq):
    bidx = jnp.asarray(_band_bucket_index_t())
    avg = jnp.asarray(_group_avg(), BF16)
    g2 = jnp.concatenate([subln_g, subln_g]).reshape(1, LANES)
    nq = seq // TQ
    n_streams = len(_DIFF_STREAMS)
    const = lambda a: pl.BlockSpec(a.shape, lambda b, n: (0,) * a.ndim)
    out = pl.pallas_call(
        functools.partial(_diff_kernel, lambda_init),
        grid=(batch, nq),
        in_specs=[pl.BlockSpec(memory_space=pltpu.SMEM),
                  pl.BlockSpec((D_HEADS_PAD, TQ), lambda b, n: (0, b * nq + n)),
                  pl.BlockSpec((1, seq, D_HEADS_PAD), lambda b, n: (b, 0, 0)),
                  pl.BlockSpec((1, seq // TK, DIFF_HEADS * VT_ROWS, TK), lambda b, n: (b, 0, 0, 0)),
                  const(lam_vecs), const(g2), const(bidx), const(avg)],
        out_specs=pl.BlockSpec((1, TQ, D_HEADS_PAD), lambda b, n: (b, n, 0)),
        out_shape=jax.ShapeDtypeStruct((batch, seq, D_HEADS_PAD), BF16),
        scratch_shapes=[pltpu.VMEM((DIFF_HEADS, 2, TK, TQ), F32),
                        pltpu.VMEM((n_streams, LANES, TQ), BF16),
                        pltpu.VMEM((seq // TK, n_streams, TK, TQ), F32),
                        pltpu.VMEM((n_streams, SUBLANES, TQ), F32),
                        pltpu.VMEM((n_streams, VT_ROWS, TQ), F32)],
        compiler_params=_cparams(2),
        name="diff_attention",
    )(bias_t, dqT, dk.reshape(batch, seq, D_HEADS_PAD),
      dvT.reshape(batch, seq // TK, DIFF_HEADS * VT_ROWS, TK), lam_vecs, g2, bidx, avg)
    return out.reshape(batch * seq, D_HEADS_PAD)


def _out_ffn_kernel(x_ref, ret_ref, dsa_ref, dif_ref, wo_ref, g1_ref, b1_ref, wg_ref, wu_ref, wd_ref,
                    g2_ref, b2_ref, o_ref):
    mixed = _dot(ret_ref[...], wo_ref[pl.ds(0, D_HEADS_PAD), :])
    mixed = mixed + _dot(dsa_ref[...], wo_ref[pl.ds(D_HEADS_PAD, D_HEADS_PAD), :])
    mixed = mixed + _dot(dif_ref[...], wo_ref[pl.ds(2 * D_HEADS_PAD, D_HEADS_PAD), :])
    x = _layer_norm(DEEPNORM_ALPHA * x_ref[...] + mixed, g1_ref[...], b1_ref[...])
    o_ref[...] = _ffn_ln_body(x, wg_ref, wu_ref, wd_ref, g2_ref[...], b2_ref[...])


def _pad_w_out(w_out):
    zeros = jnp.zeros((D_HEADS_PAD - D_DSA, w_out.shape[1]), w_out.dtype)
    return jnp.concatenate([w_out[:D_RET], w_out[D_RET:D_RET + D_DSA], zeros,
                            w_out[D_RET + D_DSA:], zeros], axis=0)


def _out_ffn(x, ret, dsa, dif, w_pad, g1, b1, wg, wu, wd, g2, b2, layer):
    t = x.shape[0]
    stacked = lambda shape: pl.BlockSpec((None,) + shape, lambda i: (layer, 0, 0), pipeline_mode=pl.Buffered(1))
    const = lambda shape: pl.BlockSpec(shape, lambda i: (0, 0), pipeline_mode=pl.Buffered(1))
    tok = lambda w: pl.BlockSpec((TM_FFN, w), lambda i: (i, 0))
    return pl.pallas_call(
        _out_ffn_kernel,
        grid=(t // TM_FFN,),
        in_specs=[tok(D_MODEL), tok(D_HEADS_PAD), tok(D_HEADS_PAD), tok(D_HEADS_PAD),
                  const((3 * D_HEADS_PAD, D_MODEL)), const((1, D_MODEL)), const((1, D_MODEL)),
                  stacked((D_MODEL, D_FF)), stacked((D_MODEL, D_FF)), stacked((D_FF, D_MODEL)),
                  const((1, D_MODEL)), const((1, D_MODEL))],
        out_specs=tok(D_MODEL),
        out_shape=jax.ShapeDtypeStruct((t, D_MODEL), F32),
        compiler_params=_cparams(1),
        name="out_proj_ffn_ln",
    )(x, ret, dsa, dif, w_pad, g1, b1, wg, wu, wd, g2, b2)


def kernel(x, w_in, w_out, ffn1_wg, ffn1_wu, ffn1_wd, ffn2_wg, ffn2_wu, ffn2_wd,
           ln_g, ln_b, diff_lambda, diff_subln_g, rel_bias):
    batch, seq, d = x.shape
    assert d == D_MODEL and seq % TQ == 0 and seq % TR == 0 and (batch * seq) % TM_FFN == 0
    assert min(DSA_TOPK_MAX, seq // 4) == DSA_TOPK_MAX
    h = x.reshape(batch * seq, d)
    bias_t = rel_bias.T
    row = lambda a: a.reshape(1, D_MODEL)
    for l in range(DEPTH):
        lambda_init = 0.8 - 0.6 * math.exp(-0.3 * l)
        h = _ffn_ln(h, ffn1_wg, ffn1_wu, ffn1_wd, row(ln_g[l, 0]), row(ln_b[l, 0]), l)
        w_tok, w_t = _split_w_in(w_in[l])
        (rq, rk, rv, rg, ak, dk, ik, aqT, dqT, iqT, iwT, avT, dvT) = _proj(
            h, w_tok.astype(BF16), w_t.astype(BF16))
        ret = _retention(rq, rk, rv, rg, batch, seq)
        dsa = _dsa(aqT, ak, avT, iqT, ik, iwT, bias_t[:DSA_HEADS], batch, seq)
        dif = _diff(dqT, dk, dvT, diff_lambda[l], diff_subln_g[l], bias_t[DSA_HEADS:], lambda_init, batch, seq)
        h = _out_ffn(h, ret, dsa, dif, _pad_w_out(w_out[l]).astype(BF16), row(ln_g[l, 1]), row(ln_b[l, 1]),
                     ffn2_wg, ffn2_wu, ffn2_wd, row(ln_g[l, 2]), row(ln_b[l, 2]), l)
    return h.reshape(batch, seq, d)
```

```python
import functools
import math

import numpy as np
import jax
import jax.numpy as jnp
from jax import lax
from jax.experimental import pallas as pl
from jax.experimental.pallas import tpu as pltpu

D_MODEL = 1024
DEPTH = 2
CHUNK = 64
HEAD_DIM = 64
RET_HEADS = 6
DSA_HEADS = 5
IDX_HEADS = 4
IDX_DIM = 64
DSA_TOPK_MAX = 256
DIFF_HEADS = 5
DIFF_QK_DIM = HEAD_DIM // 2
D_RET = RET_HEADS * HEAD_DIM
D_DSA = DSA_HEADS * HEAD_DIM
D_DIFF = DIFF_HEADS * HEAD_DIM
D_FF = 2816
N_BUCKETS = 32
MAX_DISTANCE = 128
ROPE_BASE = 10000.0
LN_EPS = 1e-5
HEAD_NORM_EPS = 1e-6
DEEPNORM_ALPHA = (2 * DEPTH) ** 0.25

LANES = 128
SUBLANES = 8
BF16_ROWS = 16
PAIR = LANES // HEAD_DIM
VMEM_LIMIT_BYTES = 56 * 1024 * 1024

D_HEADS_PAD = 3 * LANES
N_PAIRS = D_HEADS_PAD // LANES
VT_ROWS = HEAD_DIM + BF16_ROWS
TQ = 256
TK = 256
TR = 256
TM_FFN = 512
TF = 256
TM_PROJ = 512
TN_PROJ = 256
N_ACC = 4
SEARCH_STEPS_PER_CHECK = 4
SEARCH_ROUNDS = 6
NEG = -1e30
LOG2E = math.log2(math.e)

F32 = jnp.float32
BF16 = jnp.bfloat16


def _dot(a, b):
    return jnp.dot(a, b, preferred_element_type=F32)


def _dot_nt(a, b):
    return lax.dot_general(a, b, (((1,), (1,)), ((), ())), preferred_element_type=F32)


def _layer_norm(r, g, b):
    mu = jnp.mean(r, axis=-1, keepdims=True)
    d = r - mu
    var = jnp.mean(d * d, axis=-1, keepdims=True)
    return d * lax.rsqrt(var + LN_EPS) * g + b


def _group_mean(x, avg_bf16):
    hi = x.astype(BF16)
    lo = (x - hi.astype(F32)).astype(BF16)
    return _dot(hi, avg_bf16) + _dot(lo, avg_bf16)


def _cparams(n_axes):
    return pltpu.CompilerParams(
        dimension_semantics=("arbitrary",) * n_axes,
        vmem_limit_bytes=VMEM_LIMIT_BYTES,
    )


def _rel_bucket_static(rel):
    rel = np.asarray(rel, dtype=np.int64)
    half = N_BUCKETS // 2
    max_exact = half // 2
    n = np.abs(rel)
    large = np.full(n.shape, max_exact, dtype=np.int64)
    for k in range(1, 64):
        large = np.where(64 * (2 ** k) <= n * n, max_exact + k, large)
    large = np.minimum(large, half - 1)
    return (np.where(rel > 0, half, 0) + np.where(n < max_exact, n, large)).astype(np.int32)


FAR_BUCKET = int(_rel_bucket_static(-(TK + 1)))


@functools.lru_cache(maxsize=None)
def _band_bucket_index_t():
    j = np.arange(TK)[:, None]
    i = np.arange(TQ)[None, :]
    prev = _rel_bucket_static(j - TK - i)
    diag = _rel_bucket_static(j - i)
    visible = j < (i // CHUNK + 1) * CHUNK
    diag = np.where(visible, diag, -1)
    return np.stack([prev, diag]).astype(np.int32)


@functools.lru_cache(maxsize=None)
def _rotary_tables(seq):
    lane = np.arange(LANES)
    d = lane % HEAD_DIM
    f = d % (HEAD_DIM // 2)
    inv = ROPE_BASE ** (-(2.0 * f) / HEAD_DIM)
    ang = np.arange(seq, dtype=np.float64)[:, None] * inv[None, :]
    cos = np.cos(ang)
    sin = np.sin(ang) * np.where(d < HEAD_DIM // 2, -1.0, 1.0)[None, :]
    return cos.astype(np.float32), sin.astype(np.float32)


@functools.lru_cache(maxsize=None)
def _retention_tables():
    gamma = 1.0 - 2.0 ** (-5.0 - np.arange(RET_HEADS, dtype=np.float64))
    t = np.arange(TR)
    dist = np.abs(t[:, None] - t[None, :]).astype(np.float64)
    visible = t[None, :] < (t[:, None] // CHUNK + 1) * CHUNK
    intra = np.stack([np.where(visible, g ** dist, 0.0) for g in gamma])
    lane_head = np.arange(D_HEADS_PAD) // HEAD_DIM
    g_lane = gamma[lane_head]
    qdec = g_lane[None, :] ** (t[:, None] + 1.0)
    kdec = g_lane[None, :] ** (TR - 1.0 - t[:, None])
    a = np.arange(LANES)
    same = (a[:, None] // HEAD_DIM) == (a[None, :] // HEAD_DIM)
    sdec = np.stack([np.where(same, (gamma[PAIR * p + a // HEAD_DIM] ** TR)[:, None], 0.0)
                     for p in range(RET_HEADS // PAIR)])
    return (intra.astype(np.float32), qdec.astype(np.float32), kdec.astype(np.float32),
            sdec.astype(np.float32), same.astype(np.float32))


@functools.lru_cache(maxsize=None)
def _group_avg():
    a = np.arange(LANES)
    same = (a[:, None] // HEAD_DIM) == (a[None, :] // HEAD_DIM)
    return np.where(same, 1.0 / HEAD_DIM, 0.0).astype(np.float32)


@functools.lru_cache(maxsize=None)
def _prefix_tril():
    a = np.arange(TK)
    return (a[None, :] <= a[:, None]).astype(np.float32)


def _lane_group(shape):
    return lax.broadcasted_iota(jnp.int32, shape, len(shape) - 1) // HEAD_DIM


def _ffn_ln_body(x, wg_ref, wu_ref, wd_ref, g, b):
    xb = x.astype(BF16)
    acc = jnp.zeros(x.shape, F32)
    for c in range(D_FF // TF):
        cols = pl.ds(c * TF, TF)
        gate = _dot(xb, wg_ref[:, cols].astype(BF16))
        up = _dot(xb, wu_ref[:, cols].astype(BF16))
        h = gate * jax.nn.sigmoid(gate) * up
        acc = acc + _dot(h.astype(BF16), wd_ref[cols, :].astype(BF16))
    return _layer_norm(DEEPNORM_ALPHA * x + 0.5 * acc, g, b)


def _ffn_ln_kernel(x_ref, wg_ref, wu_ref, wd_ref, g_ref, b_ref, o_ref):
    o_ref[...] = _ffn_ln_body(x_ref[...], wg_ref, wu_ref, wd_ref, g_ref[...], b_ref[...])


def _ffn_ln(x, wg, wu, wd, g, b, layer):
    t = x.shape[0]
    stacked = lambda shape: pl.BlockSpec((None,) + shape, lambda i: (layer, 0, 0), pipeline_mode=pl.Buffered(1))
    full = lambda shape: pl.BlockSpec(shape, lambda i: (0, 0))
    return pl.pallas_call(
        _ffn_ln_kernel,
        grid=(t // TM_FFN,),
        in_specs=[pl.BlockSpec((TM_FFN, D_MODEL), lambda i: (i, 0)),
                  stacked((D_MODEL, D_FF)), stacked((D_MODEL, D_FF)), stacked((D_FF, D_MODEL)),
                  full((1, D_MODEL)), full((1, D_MODEL))],
        out_specs=pl.BlockSpec((TM_FFN, D_MODEL), lambda i: (i, 0)),
        out_shape=jax.ShapeDtypeStruct((t, D_MODEL), F32),
        compiler_params=_cparams(1),
        name="ffn_ln",
    )(x, wg, wu, wd, g, b)


IN_SPLITS = (D_RET, D_RET, D_RET, D_RET, D_DSA, D_DSA, D_DSA,
             IDX_HEADS * IDX_DIM, IDX_DIM, IDX_HEADS, D_DIFF, D_DIFF, D_DIFF)
N_IN = sum(IN_SPLITS)
N_IN_PAD = -(-N_IN // LANES) * LANES
_PROJ_OUT = (("rq", D_HEADS_PAD, F32, None), ("rk", D_HEADS_PAD, F32, None), ("rv", D_HEADS_PAD, BF16, None),
             ("rg", D_HEADS_PAD, F32, None),
             ("aq", D_HEADS_PAD, BF16, HEAD_DIM ** -0.5 * LOG2E), ("ak", D_HEADS_PAD, BF16, None),
             ("av", D_HEADS_PAD, BF16, None),
             ("iq", IDX_HEADS * IDX_DIM, BF16, None), ("ik", LANES, BF16, None), ("iw", LANES, F32, None),
             ("dq", D_HEADS_PAD, BF16, DIFF_QK_DIM ** -0.5 * LOG2E), ("dk", D_HEADS_PAD, BF16, None),
             ("dv", D_HEADS_PAD, BF16, None))


def _proj_kernel(x_ref, w_ref, *refs):
    o_refs, h_scr = refs[:-1], refs[-1]
    xb = x_ref[...].astype(BF16)
    for lo in range(0, N_IN, TN_PROJ):
        cols = pl.ds(lo, min(TN_PROJ, N_IN - lo))
        h_scr[:, cols] = _dot(xb, w_ref[:, cols].astype(BF16))
    off = 0
    for width, (name, out_width, dtype, scale), o_ref in zip(IN_SPLITS, _PROJ_OUT, o_refs):
        base = off // LANES * LANES
        end = min(-(-(off + width) // LANES) * LANES, N_IN_PAD)
        seg = h_scr[:, pl.ds(base, end - base)][:, off - base:off - base + width]
        if scale is not None:
            seg = seg * scale
        if name == "ik":
            seg = jnp.concatenate([seg, seg], axis=1)
        elif out_width > width:
            seg = jnp.concatenate([seg, jnp.zeros((TM_PROJ, out_width - width), F32)], axis=1)
        o_ref[...] = seg.astype(dtype)
        off += width


def _proj(x, w_in, layer):
    t = x.shape[0]
    return pl.pallas_call(
        _proj_kernel,
        grid=(t // TM_PROJ,),
        in_specs=[pl.BlockSpec((TM_PROJ, D_MODEL), lambda i: (i, 0)),
                  pl.BlockSpec((None, D_MODEL, N_IN), lambda i: (layer, 0, 0), pipeline_mode=pl.Buffered(1))],
        out_specs=[pl.BlockSpec((TM_PROJ, w), lambda i: (i, 0)) for _, w, _, _ in _PROJ_OUT],
        out_shape=[jax.ShapeDtypeStruct((t, w), dt) for _, w, dt, _ in _PROJ_OUT],
        scratch_shapes=[pltpu.VMEM((TM_PROJ, N_IN_PAD), F32)],
        compiler_params=_cparams(1),
        name="in_proj",
    )(x, w_in)


def _swap_halves(x):
    lane = lax.broadcasted_iota(jnp.int32, x.shape, 1)
    first_half = (lane % HEAD_DIM) < (HEAD_DIM // 2)
    from_right = pltpu.roll(x, LANES - HEAD_DIM // 2, 1)
    from_left = pltpu.roll(x, HEAD_DIM // 2, 1)
    return jnp.where(first_half, from_right, from_left)


def _retention_kernel(q_ref, k_ref, v_ref, g_ref, cos_ref, sin_ref, intra_ref, qdec_ref, kdec_ref,
                      sdec_ref, same_ref, avg_ref, o_ref, state_ref):
    @pl.when(pl.program_id(1) == 0)
    def _():
        state_ref[...] = jnp.zeros(state_ref.shape, F32)

    cos = cos_ref[...]
    sin = sin_ref[...]
    avg = avg_ref[...]
    group = _lane_group((TR, LANES))
    for p in range(RET_HEADS // PAIR):
        lanes = pl.ds(p * LANES, LANES)
        q = q_ref[0, :, lanes]
        k = k_ref[0, :, lanes]
        q = (q * cos + _swap_halves(q) * sin) * (HEAD_DIM ** -0.5)
        k = k * cos + _swap_halves(k) * sin
        v = v_ref[0, :, lanes]
        kb = k.astype(BF16)
        intra = jnp.zeros((TR, LANES), F32)
        for j in range(PAIR):
            qh = jnp.where(group == j, q, 0.0).astype(BF16)
            scores = _dot_nt(qh, kb) * intra_ref[PAIR * p + j]
            oh = _dot(scores.astype(BF16), v)
            intra = jnp.where(group == j, oh, intra)
        state = state_ref[p]
        cross = _dot((q * qdec_ref[:, lanes]).astype(BF16), state.astype(BF16))
        o = intra + cross
        kd_t = (k * kdec_ref[:, lanes]).T.astype(BF16)
        state_ref[p] = state * sdec_ref[p] + _dot(kd_t, v) * same_ref[...]
        mu = _group_mean(o, avg)
        d = o - mu
        var = _group_mean(d * d, avg)
        normed = d * lax.rsqrt(var + HEAD_NORM_EPS)
        gate = g_ref[0, :, lanes]
        o_ref[0, :, lanes] = (gate * jax.nn.sigmoid(gate) * normed).astype(o_ref.dtype)


def _retention(rq, rk, rv, rg, batch, seq):
    cos, sin = _rotary_tables(seq)
    intra, qdec, kdec, sdec, same = _retention_tables()
    avg = jnp.asarray(_group_avg(), BF16)
    r3 = lambda a: a.reshape(batch, seq, D_HEADS_PAD)
    tok = pl.BlockSpec((1, TR, D_HEADS_PAD), lambda b, n: (b, n, 0))
    const = lambda a: pl.BlockSpec(a.shape, lambda b, n: (0,) * a.ndim)
    pos = pl.BlockSpec((TR, LANES), lambda b, n: (n, 0))
    out = pl.pallas_call(
        _retention_kernel,
        grid=(batch, seq // TR),
        in_specs=[tok, tok, tok, tok, pos, pos, const(intra), const(qdec), const(kdec),
                  const(sdec), const(same), const(avg)],
        out_specs=tok,
        out_shape=jax.ShapeDtypeStruct((batch, seq, D_HEADS_PAD), BF16),
        scratch_shapes=[pltpu.VMEM((RET_HEADS // PAIR, LANES, LANES), F32)],
        compiler_params=_cparams(2),
        name="retention",
    )(r3(rq), r3(rk), r3(rv), r3(rg), cos, sin, intra, qdec, kdec, sdec, same, avg)
    return out.reshape(batch * seq, D_HEADS_PAD)


def _build_band(bidx_ref, tab_ref, band_ref, n_heads):
    for h in range(n_heads):
        far = tab_ref[h, FAR_BUCKET]
        for t in range(2):
            idx = bidx_ref[t]
            val = jnp.full((TK, TQ), NEG, F32)
            for bkt in range(N_BUCKETS):
                val = jnp.where(idx == bkt, (tab_ref[h, bkt] - far) * LOG2E, val)
            band_ref[h, t] = val


def _transposed_slab(ref, pair):
    return ref[0, :, pl.ds(pair * LANES, LANES)].astype(F32).T


def _load_queries(q_ref, q_scr, streams):
    row = lax.broadcasted_iota(jnp.int32, (LANES, TQ), 0)
    slabs = {}
    for i, (pair, _, lo, hi) in enumerate(streams):
        if pair not in slabs:
            slabs[pair] = _transposed_slab(q_ref, pair).astype(BF16)
        q_scr[i] = jnp.where((row >= lo) & (row < hi), slabs[pair], 0)


def _load_values(v_ref, vt_scr, n_heads, n_tiles):
    ones = jnp.ones((BF16_ROWS, TK), BF16)

    def tile(kt, carry):
        rows = pl.ds(pl.multiple_of(kt * TK, TK), TK)
        for pair in range(N_PAIRS):
            vt = v_ref[0, rows, pl.ds(pair * LANES, LANES)].astype(F32).T.astype(BF16)
            for j in range(PAIR):
                h = PAIR * pair + j
                if h < n_heads:
                    vt_scr[kt, pl.ds(h * VT_ROWS, HEAD_DIM), :] = vt[j * HEAD_DIM:(j + 1) * HEAD_DIM]
                    vt_scr[kt, pl.ds(h * VT_ROWS + HEAD_DIM, BF16_ROWS), :] = ones
        return carry

    lax.fori_loop(0, n_tiles, tile, 0)


def _for_tiles(n, fn):
    def pair(j, carry):
        fn(2 * j)
        fn(2 * j + 1)
        return carry

    lax.fori_loop(0, lax.shift_right_logical(n, 1), pair, 0)

    @pl.when(lax.rem(n, 2) == 1)
    def _():
        fn(n - 1)


def _softmax_passes(qb, k_ref, vt_ref, q_scr, band_ref, mask_fn, logit_ref, m_ref, acc_ref, streams):
    m_ref[...] = jnp.full(m_ref.shape, NEG, F32)

    def logits_tile(kt, band_t):
        rows = pl.ds(pl.multiple_of(kt * TK, TK), TK)
        mask = mask_fn(kt)
        for i, (pair, head, _, _) in enumerate(streams):
            lg = _dot(k_ref[0, rows, pl.ds(pair * LANES, LANES)], q_scr[i])
            if band_t is not None:
                lg = lg + band_ref[head, band_t]
            if mask is not None:
                lg = lg + mask
            logit_ref[kt, i] = lg
            m_ref[i] = jnp.maximum(m_ref[i], jnp.max(lg.reshape(TK // SUBLANES, SUBLANES, TQ), axis=0))

    _for_tiles(jnp.maximum(qb - 1, 0), lambda kt: logits_tile(kt, None))

    @pl.when(qb >= 1)
    def _():
        logits_tile(qb - 1, 0)
        logits_tile(qb, 1)

    @pl.when(qb == 0)
    def _():
        logits_tile(qb, 1)

    m_fin = [jnp.max(m_ref[i], axis=0, keepdims=True) for i in range(len(streams))]
    acc_ref[...] = jnp.zeros(acc_ref.shape, F32)

    def values_tile(kt):
        for i, (_, head, _, _) in enumerate(streams):
            p = jnp.exp2(logit_ref[kt, i] - m_fin[i])
            acc_ref[i] += _dot(vt_ref[kt, pl.ds(head * VT_ROWS, VT_ROWS), :], p.astype(BF16))

    _for_tiles(qb + 1, values_tile)


def _normalised(acc_ref, i):
    a = acc_ref[i]
    return a[:HEAD_DIM] / a[HEAD_DIM:HEAD_DIM + 1]


_DSA_STREAMS = tuple((h // PAIR, h, (h % PAIR) * HEAD_DIM, (h % PAIR + 1) * HEAD_DIM) for h in range(DSA_HEADS))


def _f32_to_key(x):
    bits = lax.bitcast_convert_type(x, jnp.int32)
    return jnp.where(bits < 0, ~bits, bits ^ jnp.int32(-2 ** 31))


def _key_to_f32(key):
    bits = jnp.where(key < 0, key ^ jnp.int32(-2 ** 31), ~key)
    return lax.bitcast_convert_type(bits, F32)


def _dsa_kernel(tab_ref, q_ref, k_ref, v_ref, iq_ref, ik_ref, iw_ref, bidx_ref, tril_ref,
                o_ref, band_ref, score_ref, q_scr, logit_ref, m_ref, acc_ref, tiecnt_ref, tie_ref, stat_ref, vt_ref):
    qb = pl.program_id(1)
    n_tiles = qb + 1

    @pl.when((pl.program_id(0) == 0) & (qb == 0))
    def _():
        _build_band(bidx_ref, tab_ref, band_ref, DSA_HEADS)

    @pl.when(qb == 0)
    def _():
        _load_values(v_ref, vt_ref, DSA_HEADS, k_ref.shape[1] // TK)

    iw = iw_ref[0].T[:SUBLANES] * (IDX_HEADS ** -0.5 * IDX_DIM ** -0.5)
    row = lax.broadcasted_iota(jnp.int32, (LANES, TQ), 0)
    iq_t = [_transposed_slab(iq_ref, p).astype(BF16) for p in range(IDX_HEADS // PAIR)]
    iq = [jnp.where(row // HEAD_DIM == (h % PAIR), iq_t[h // PAIR], 0) for h in range(IDX_HEADS)]

    def score_tile(kt):
        ki = ik_ref[0, pl.ds(pl.multiple_of(kt * TK, TK), TK), :]
        s = jnp.zeros((TK, TQ), F32)
        for h in range(IDX_HEADS):
            s = s + jnp.maximum(_dot(ki, iq[h]), 0.0) * iw[h:h + 1, :]
        return s

    stat_ref[0] = jnp.full((SUBLANES, TQ), -jnp.inf, F32)
    stat_ref[1] = jnp.full((SUBLANES, TQ), jnp.inf, F32)
    stat_ref[2] = jnp.zeros((SUBLANES, TQ), F32)
    stat_ref[3] = jnp.zeros((SUBLANES, TQ), F32)

    def add_stats(s_for_max, s_for_min):
        hi_part = s_for_max.reshape(TK // SUBLANES, SUBLANES, TQ)
        stat_ref[0] = jnp.maximum(stat_ref[0], jnp.max(hi_part, axis=0))
        stat_ref[1] = jnp.minimum(stat_ref[1], jnp.min(s_for_min.reshape(TK // SUBLANES, SUBLANES, TQ), axis=0))
        stat_ref[2] += jnp.sum(jnp.where(hi_part > 0.0, 1.0, 0.0), axis=0)
        stat_ref[3] += jnp.sum(jnp.where(hi_part >= 0.0, 1.0, 0.0), axis=0)

    def past_scores(kt):
        s = score_tile(kt)
        score_ref[kt] = s
        add_stats(s, s)

    _for_tiles(qb, past_scores)
    s_diag = score_tile(qb)
    visible = band_ref[0, 1] > 0.5 * NEG
    score_ref[qb] = jnp.where(visible, s_diag, -jnp.inf)
    add_stats(jnp.where(visible, s_diag, -jnp.inf), jnp.where(visible, s_diag, jnp.inf))

    k_sel = float(DSA_TOPK_MAX)

    def count(pred):
        def body(kt, c):
            xs = score_ref[kt].reshape(TK // SUBLANES, SUBLANES, TQ)
            accs = [c[a] for a in range(N_ACC)]
            for g in range(TK // SUBLANES):
                a = accs[g % N_ACC]
                accs[g % N_ACC] = jnp.where(pred(xs[g]), a + 1.0, a)
            return jnp.stack(accs)
        c = lax.fori_loop(0, n_tiles, body, jnp.zeros((N_ACC, SUBLANES, TQ), F32))
        return jnp.sum(jnp.sum(c, axis=0), axis=0, keepdims=True)

    col = lax.broadcasted_iota(jnp.int32, (1, TQ), 1)
    limit = qb * TQ + (col // CHUNK + 1) * CHUNK
    select_all = limit <= DSA_TOPK_MAX
    s_max = jnp.max(stat_ref[0], axis=0, keepdims=True)
    s_min = jnp.min(stat_ref[1], axis=0, keepdims=True)
    n_pos = jnp.sum(stat_ref[2], axis=0, keepdims=True)
    n_nonneg = jnp.sum(stat_ref[3], axis=0, keepdims=True)

    positive = n_pos > k_sel
    zero_thr = (n_pos <= k_sel) & (k_sel <= n_nonneg)
    lo0 = jnp.where(positive, 0.0, _key_to_f32(_f32_to_key(s_min) - 1))
    hi0 = jnp.where(positive, _key_to_f32(_f32_to_key(s_max) + 1), 0.0)
    flo0 = jnp.where(positive, n_pos, limit.astype(F32)) - k_sel
    fhi0 = jnp.where(positive, 0.0, n_nonneg) - k_sel
    done0 = jnp.where(select_all | zero_thr, 1.0, 0.0)
    thr0 = jnp.where(select_all, -jnp.inf, 0.0)

    def search_step(vec):
        lo, hi, flo, fhi, side, done, thr = vec
        w = flo / (flo - fhi)
        t = lo * (1.0 - w) + hi * w
        t = jnp.where((t > lo) & (t < hi), t, 0.5 * lo + 0.5 * hi)
        splits = (t > lo) & (t < hi)
        f = count(lambda s: s >= t) - k_sel
        active = done == 0.0
        hit = active & splits & (f == 0.0)
        adjacent = active & ~splits
        thr = jnp.where(hit, t, jnp.where(adjacent, lo, thr))
        done = jnp.where(hit | adjacent, 1.0, done)
        up = active & splits & (f > 0.0)
        down = active & splits & (f < 0.0)
        fhi_new = jnp.where(up & (side > 0.0), 0.5 * fhi, jnp.where(down, f, fhi))
        flo_new = jnp.where(down & (side < 0.0), 0.5 * flo, jnp.where(up, f, flo))
        lo = jnp.where(up, t, lo)
        hi = jnp.where(down, t, hi)
        side = jnp.where(up, 1.0, jnp.where(down, -1.0, side))
        return lo, hi, flo_new, fhi_new, side, done, thr

    def open_queries(done):
        return (jnp.min(done) == 0.0).astype(jnp.int32)

    def search_round(state):
        rounds, _, vec = state
        for _ in range(SEARCH_STEPS_PER_CHECK):
            vec = search_step(vec)
        return rounds + 1, open_queries(vec[5]), vec

    vec0 = (lo0, hi0, flo0, fhi0, jnp.zeros((1, TQ), F32), done0, thr0)
    state = lax.while_loop(lambda st: (st[0] < SEARCH_ROUNDS) & (st[1] > 0), search_round,
                           (jnp.int32(0), open_queries(done0), vec0))
    done, thr = state[2][5], state[2][6]

    def bisect(i, prefix):
        cand = prefix | lax.shift_left(jnp.int32(1), 31 - i)
        cand_f = _key_to_f32(cand)
        take = count(lambda s: s >= cand_f) >= k_sel
        return jnp.where(take, cand, prefix)

    def exact_thr():
        prefix = lax.fori_loop(0, 32, bisect, jnp.zeros((1, TQ), jnp.int32))
        return jnp.where(done == 0.0, _key_to_f32(prefix), thr)

    thr = lax.cond(state[1] > 0, exact_thr, lambda: thr)
    n_gt = count(lambda s: s > thr)
    need = jnp.where(select_all, 0.0, k_sel - n_gt)

    def tie_counts(kt, carry):
        tied = jnp.where(score_ref[kt] == thr, 1.0, 0.0).reshape(TK // SUBLANES, SUBLANES, TQ)
        tiecnt_ref[kt] = jnp.sum(jnp.sum(tied, axis=0), axis=0, keepdims=True)
        return carry

    lax.fori_loop(0, n_tiles, tie_counts, 0)
    tie_ref[...] = jnp.zeros(tie_ref.shape, F32)
    tril = tril_ref[...]

    def selection_mask(kt):
        s = score_ref[kt]
        tied = s == thr
        tied_before = tie_ref[...]
        tie_ref[...] = tied_before + tiecnt_ref[kt]
        tied_upto = tied_before + _dot(tril, jnp.where(tied, 1.0, 0.0).astype(BF16))
        return jnp.where((s > thr) | (tied & (tied_upto <= need)), 0.0, NEG)

    _load_queries(q_ref, q_scr, _DSA_STREAMS)
    _softmax_passes(qb, k_ref, vt_ref, q_scr, band_ref, selection_mask, logit_ref, m_ref, acc_ref, _DSA_STREAMS)

    for p in range(N_PAIRS):
        halves = [_normalised(acc_ref, PAIR * p + j) if PAIR * p + j < DSA_HEADS
                  else jnp.zeros((HEAD_DIM, TQ), F32) for j in range(PAIR)]
        o_ref[0, :, pl.ds(p * LANES, LANES)] = jnp.concatenate(halves, axis=0).T.astype(o_ref.dtype)


def _dsa(aq, ak, av, iq, ik, iw, bias_t, batch, seq):
    bidx = jnp.asarray(_band_bucket_index_t())
    tril = jnp.asarray(_prefix_tril(), BF16)
    n_streams = len(_DSA_STREAMS)
    r3 = lambda a: a.reshape(batch, seq, a.shape[-1])
    qblk = lambda w: pl.BlockSpec((1, TQ, w), lambda b, n: (b, n, 0))
    kblk = lambda w: pl.BlockSpec((1, seq, w), lambda b, n: (b, 0, 0))
    const = lambda a: pl.BlockSpec(a.shape, lambda b, n: (0,) * a.ndim)
    out = pl.pallas_call(
        _dsa_kernel,
        grid=(batch, seq // TQ),
        in_specs=[pl.BlockSpec(memory_space=pltpu.SMEM),
                  qblk(D_HEADS_PAD), kblk(D_HEADS_PAD), kblk(D_HEADS_PAD),
                  qblk(IDX_HEADS * IDX_DIM), kblk(LANES), qblk(LANES), const(bidx), const(tril)],
        out_specs=qblk(D_HEADS_PAD),
        out_shape=jax.ShapeDtypeStruct((batch, seq, D_HEADS_PAD), BF16),
        scratch_shapes=[pltpu.VMEM((DSA_HEADS, 2, TK, TQ), F32),
                        pltpu.VMEM((seq // TK, TK, TQ), F32),
                        pltpu.VMEM((n_streams, LANES, TQ), BF16),
                        pltpu.VMEM((seq // TK, n_streams, TK, TQ), F32),
                        pltpu.VMEM((n_streams, SUBLANES, TQ), F32),
                        pltpu.VMEM((n_streams, VT_ROWS, TQ), F32),
                        pltpu.VMEM((seq // TK, 1, TQ), F32),
                        pltpu.VMEM((1, TQ), F32),
                        pltpu.VMEM((4, SUBLANES, TQ), F32),
                        pltpu.VMEM((seq // TK, DSA_HEADS * VT_ROWS, TK), BF16)],
        compiler_params=_cparams(2),
        name="dsa_attention",
    )(bias_t, r3(aq), r3(ak), r3(av), r3(iq), r3(ik), r3(iw), bidx, tril)
    return out.reshape(batch * seq, D_HEADS_PAD)


_DIFF_STREAMS = tuple(
    (h // PAIR, h, (h % PAIR) * HEAD_DIM + mm * DIFF_QK_DIM, (h % PAIR) * HEAD_DIM + (mm + 1) * DIFF_QK_DIM)
    for h in range(DIFF_HEADS) for mm in range(2))


def _diff_kernel(lambda_init, tab_ref, q_ref, k_ref, v_ref, lam_ref, g_ref, bidx_ref, avg_ref,
                 o_ref, band_ref, q_scr, logit_ref, m_ref, acc_ref, vt_ref):
    qb = pl.program_id(1)

    @pl.when((pl.program_id(0) == 0) & (qb == 0))
    def _():
        _build_band(bidx_ref, tab_ref, band_ref, DIFF_HEADS)

    @pl.when(qb == 0)
    def _():
        _load_values(v_ref, vt_ref, DIFF_HEADS, k_ref.shape[1] // TK)

    lv = lam_ref[...]
    lam = (jnp.exp(jnp.sum(lv[0:1] * lv[1:2], axis=-1, keepdims=True))
           - jnp.exp(jnp.sum(lv[2:3] * lv[3:4], axis=-1, keepdims=True)) + lambda_init)

    _load_queries(q_ref, q_scr, _DIFF_STREAMS)
    _softmax_passes(qb, k_ref, vt_ref, q_scr, band_ref, lambda kt: None, logit_ref, m_ref, acc_ref,
                    _DIFF_STREAMS)

    avg = avg_ref[...]
    for p in range(N_PAIRS):
        halves = []
        for j in range(PAIR):
            h = PAIR * p + j
            if h < DIFF_HEADS:
                halves.append(_normalised(acc_ref, 2 * h) - lam * _normalised(acc_ref, 2 * h + 1))
            else:
                halves.append(jnp.zeros((HEAD_DIM, TQ), F32))
        out = jnp.concatenate(halves, axis=0).T
        ms = _group_mean(out * out, avg)
        out = out * lax.rsqrt(ms + LN_EPS) * g_ref[...] * (1.0 - lambda_init)
        o_ref[0, :, pl.ds(p * LANES, LANES)] = out.astype(o_ref.dtype)


def _diff(dq, dk, dv, lam_vecs, subln_g, bias_t, lambda_init, batch, seq):
    bidx = jnp.asarray(_band_bucket_index_t())
    avg = jnp.asarray(_group_avg(), BF16)
    g2 = jnp.concatenate([subln_g, subln_g]).reshape(1, LANES)
    n_streams = len(_DIFF_STREAMS)
    r3 = lambda a: a.reshape(batch, seq, a.shape[-1])
    qblk = pl.BlockSpec((1, TQ, D_HEADS_PAD), lambda b, n: (b, n, 0))
    kblk = pl.BlockSpec((1, seq, D_HEADS_PAD), lambda b, n: (b, 0, 0))
    const = lambda a: pl.BlockSpec(a.shape, lambda b, n: (0,) * a.ndim)
    out = pl.pallas_call(
        functools.partial(_diff_kernel, lambda_init),
        grid=(batch, seq // TQ),
        in_specs=[pl.BlockSpec(memory_space=pltpu.SMEM), qblk, kblk, kblk,
                  const(lam_vecs), const(g2), const(bidx), const(avg)],
        out_specs=qblk,
        out_shape=jax.ShapeDtypeStruct((batch, seq, D_HEADS_PAD), BF16),
        scratch_shapes=[pltpu.VMEM((DIFF_HEADS, 2, TK, TQ), F32),
                        pltpu.VMEM((n_streams, LANES, TQ), BF16),
                        pltpu.VMEM((seq // TK, n_streams, TK, TQ), F32),
                        pltpu.VMEM((n_streams, SUBLANES, TQ), F32),
                        pltpu.VMEM((n_streams, VT_ROWS, TQ), F32),
                        pltpu.VMEM((seq // TK, DIFF_HEADS * VT_ROWS, TK), BF16)],
        compiler_params=_cparams(2),
        name="diff_attention",
    )(bias_t, r3(dq), r3(dk), r3(dv), lam_vecs, g2, bidx, avg)
    return out.reshape(batch * seq, D_HEADS_PAD)


def _out_ffn_kernel(x_ref, ret_ref, dsa_ref, dif_ref, wo_ref, g1_ref, b1_ref, wg_ref, wu_ref, wd_ref,
                    g2_ref, b2_ref, o_ref):
    mixed = _dot(ret_ref[...], wo_ref[pl.ds(0, D_HEADS_PAD), :])
    mixed = mixed + _dot(dsa_ref[...], wo_ref[pl.ds(D_HEADS_PAD, D_HEADS_PAD), :])
    mixed = mixed + _dot(dif_ref[...], wo_ref[pl.ds(2 * D_HEADS_PAD, D_HEADS_PAD), :])
    x = _layer_norm(DEEPNORM_ALPHA * x_ref[...] + mixed, g1_ref[...], b1_ref[...])
    o_ref[...] = _ffn_ln_body(x, wg_ref, wu_ref, wd_ref, g2_ref[...], b2_ref[...])


def _pad_w_out(w_out):
    zeros = jnp.zeros((D_HEADS_PAD - D_DSA, w_out.shape[1]), w_out.dtype)
    return jnp.concatenate([w_out[:D_RET], w_out[D_RET:D_RET + D_DSA], zeros,
                            w_out[D_RET + D_DSA:], zeros], axis=0)


def _out_ffn(x, ret, dsa, dif, w_pad, g1, b1, wg, wu, wd, g2, b2, layer):
    t = x.shape[0]
    stacked = lambda shape: pl.BlockSpec((None,) + shape, lambda i: (layer, 0, 0), pipeline_mode=pl.Buffered(1))
    const = lambda shape: pl.BlockSpec(shape, lambda i: (0, 0), pipeline_mode=pl.Buffered(1))
    tok = lambda w: pl.BlockSpec((TM_FFN, w), lambda i: (i, 0))
    return pl.pallas_call(
        _out_ffn_kernel,
        grid=(t // TM_FFN,),
        in_specs=[tok(D_MODEL), tok(D_HEADS_PAD), tok(D_HEADS_PAD), tok(D_HEADS_PAD),
                  const((3 * D_HEADS_PAD, D_MODEL)), const((1, D_MODEL)), const((1, D_MODEL)),
                  stacked((D_MODEL, D_FF)), stacked((D_MODEL, D_FF)), stacked((D_FF, D_MODEL)),
                  const((1, D_MODEL)), const((1, D_MODEL))],
        out_specs=tok(D_MODEL),
        out_shape=jax.ShapeDtypeStruct((t, D_MODEL), F32),
        compiler_params=_cparams(1),
        name="out_proj_ffn_ln",
    )(x, ret, dsa, dif, w_pad, g1, b1, wg, wu, wd, g2, b2)


def kernel(x, w_in, w_out, ffn1_wg, ffn1_wu, ffn1_wd, ffn2_wg, ffn2_wu, ffn2_wd,
           ln_g, ln_b, diff_lambda, diff_subln_g, rel_bias):
    batch, seq, d = x.shape
    assert d == D_MODEL and seq % TQ == 0 and seq % TR == 0 and (batch * seq) % TM_FFN == 0
    assert min(DSA_TOPK_MAX, seq // 4) == DSA_TOPK_MAX
    h = x.reshape(batch * seq, d)
    bias_t = rel_bias.T
    row = lambda a: a.reshape(1, D_MODEL)
    for l in range(DEPTH):
        lambda_init = 0.8 - 0.6 * math.exp(-0.3 * l)
        h = _ffn_ln(h, ffn1_wg, ffn1_wu, ffn1_wd, row(ln_g[l, 0]), row(ln_b[l, 0]), l)
        (rq, rk, rv, rg, aq, ak, av, iq, ik, iw, dq, dk, dv) = _proj(h, w_in, l)
        ret = _retention(rq, rk, rv, rg, batch, seq)
        dsa = _dsa(aq, ak, av, iq, ik, iw, bias_t[:DSA_HEADS], batch, seq)
        dif = _diff(dq, dk, dv, diff_lambda[l], diff_subln_g[l], bias_t[DSA_HEADS:], lambda_init, batch, seq)
        h = _out_ffn(h, ret, dsa, dif, _pad_w_out(w_out[l]).astype(BF16), row(ln_g[l, 1]), row(ln_b[l, 1]),
                     ffn2_wg, ffn2_wu, ffn2_wd, row(ln_g[l, 2]), row(ln_b[l, 2]), l)
    return h.reshape(batch, seq, d)
```

```python
import functools
import math

import numpy as np
import jax
import jax.numpy as jnp
from jax import lax
from jax.experimental import pallas as pl
from jax.experimental.pallas import tpu as pltpu

D_MODEL = 1024
DEPTH = 2
CHUNK = 64
HEAD_DIM = 64
RET_HEADS = 6
DSA_HEADS = 5
IDX_HEADS = 4
IDX_DIM = 64
DSA_TOPK_MAX = 256
DIFF_HEADS = 5
DIFF_QK_DIM = HEAD_DIM // 2
D_RET = RET_HEADS * HEAD_DIM
D_DSA = DSA_HEADS * HEAD_DIM
D_DIFF = DIFF_HEADS * HEAD_DIM
D_FF = 2816
N_BUCKETS = 32
MAX_DISTANCE = 128
ROPE_BASE = 10000.0
LN_EPS = 1e-5
HEAD_NORM_EPS = 1e-6
DEEPNORM_ALPHA = (2 * DEPTH) ** 0.25

LANES = 128
SUBLANES = 8
BF16_ROWS = 16
PAIR = LANES // HEAD_DIM
VMEM_LIMIT_BYTES = 56 * 1024 * 1024

D_HEADS_PAD = 3 * LANES
N_PAIRS = D_HEADS_PAD // LANES
VT_ROWS = HEAD_DIM + BF16_ROWS
TQ = 256
TK = 256
TR = 256
RET_BLOCKS = 4
TM_FFN = 512
TF = 256
TM_PROJ = 512
TN_PROJ = 256
N_ACC = 4
SEARCH_STEPS_PER_CHECK = 4
SEARCH_ROUNDS = 6
NEG = -1e30
LOG2E = math.log2(math.e)

F32 = jnp.float32
BF16 = jnp.bfloat16


def _dot(a, b):
    return jnp.dot(a, b, preferred_element_type=F32)


def _dot_nt(a, b):
    return lax.dot_general(a, b, (((1,), (1,)), ((), ())), preferred_element_type=F32)


def _layer_norm(r, g, b):
    mu = jnp.mean(r, axis=-1, keepdims=True)
    d = r - mu
    var = jnp.mean(d * d, axis=-1, keepdims=True)
    return d * lax.rsqrt(var + LN_EPS) * g + b


def _group_mean(x, avg_bf16):
    hi = x.astype(BF16)
    lo = (x - hi.astype(F32)).astype(BF16)
    return _dot(hi, avg_bf16) + _dot(lo, avg_bf16)


def _cparams(n_axes):
    return pltpu.CompilerParams(
        dimension_semantics=("arbitrary",) * n_axes,
        vmem_limit_bytes=VMEM_LIMIT_BYTES,
    )


def _rel_bucket_static(rel):
    rel = np.asarray(rel, dtype=np.int64)
    half = N_BUCKETS // 2
    max_exact = half // 2
    n = np.abs(rel)
    large = np.full(n.shape, max_exact, dtype=np.int64)
    for k in range(1, 64):
        large = np.where(64 * (2 ** k) <= n * n, max_exact + k, large)
    large = np.minimum(large, half - 1)
    return (np.where(rel > 0, half, 0) + np.where(n < max_exact, n, large)).astype(np.int32)


FAR_BUCKET = int(_rel_bucket_static(-(TK + 1)))


@functools.lru_cache(maxsize=None)
def _band_bucket_index_t():
    j = np.arange(TK)[:, None]
    i = np.arange(TQ)[None, :]
    prev = _rel_bucket_static(j - TK - i)
    diag = _rel_bucket_static(j - i)
    visible = j < (i // CHUNK + 1) * CHUNK
    diag = np.where(visible, diag, -1)
    return np.stack([prev, diag]).astype(np.int32)


@functools.lru_cache(maxsize=None)
def _rotary_tables(seq):
    lane = np.arange(LANES)
    d = lane % HEAD_DIM
    f = d % (HEAD_DIM // 2)
    inv = ROPE_BASE ** (-(2.0 * f) / HEAD_DIM)
    ang = np.arange(seq, dtype=np.float64)[:, None] * inv[None, :]
    cos = np.cos(ang)
    sin = np.sin(ang) * np.where(d < HEAD_DIM // 2, -1.0, 1.0)[None, :]
    return cos.astype(np.float32), sin.astype(np.float32)


@functools.lru_cache(maxsize=None)
def _retention_tables():
    gamma = 1.0 - 2.0 ** (-5.0 - np.arange(RET_HEADS, dtype=np.float64))
    t = np.arange(TR)
    dist = np.abs(t[:, None] - t[None, :]).astype(np.float64)
    visible = t[None, :] < (t[:, None] // CHUNK + 1) * CHUNK
    intra = np.stack([np.where(visible, g ** dist, 0.0) for g in gamma])
    lane_head = np.arange(D_HEADS_PAD) // HEAD_DIM
    g_lane = gamma[lane_head]
    qdec = g_lane[None, :] ** (t[:, None] + 1.0)
    kdec = g_lane[None, :] ** (TR - 1.0 - t[:, None])
    a = np.arange(LANES)
    same = (a[:, None] // HEAD_DIM) == (a[None, :] // HEAD_DIM)
    sdec = np.stack([np.where(same, (gamma[PAIR * p + a // HEAD_DIM] ** TR)[:, None], 0.0)
                     for p in range(RET_HEADS // PAIR)])
    return (intra.astype(np.float32), qdec.astype(np.float32), kdec.astype(np.float32),
            sdec.astype(np.float32), same.astype(np.float32))


@functools.lru_cache(maxsize=None)
def _group_avg():
    a = np.arange(LANES)
    same = (a[:, None] // HEAD_DIM) == (a[None, :] // HEAD_DIM)
    return np.where(same, 1.0 / HEAD_DIM, 0.0).astype(np.float32)


@functools.lru_cache(maxsize=None)
def _prefix_tril():
    a = np.arange(TK)
    return (a[None, :] <= a[:, None]).astype(np.float32)


def _lane_group(shape):
    return lax.broadcasted_iota(jnp.int32, shape, len(shape) - 1) // HEAD_DIM


def _ffn_ln_body(x, wg_ref, wu_ref, wd_ref, g, b):
    xb = x.astype(BF16)
    acc = jnp.zeros(x.shape, F32)
    for c in range(D_FF // TF):
        cols = pl.ds(c * TF, TF)
        gate = _dot(xb, wg_ref[:, cols].astype(BF16))
        up = _dot(xb, wu_ref[:, cols].astype(BF16))
        h = gate * jax.nn.sigmoid(gate) * up
        acc = acc + _dot(h.astype(BF16), wd_ref[cols, :].astype(BF16))
    return _layer_norm(DEEPNORM_ALPHA * x + 0.5 * acc, g, b)


def _ffn_ln_kernel(x_ref, wg_ref, wu_ref, wd_ref, g_ref, b_ref, o_ref):
    o_ref[...] = _ffn_ln_body(x_ref[...], wg_ref, wu_ref, wd_ref, g_ref[...], b_ref[...])


def _ffn_ln(x, wg, wu, wd, g, b, layer):
    t = x.shape[0]
    stacked = lambda shape: pl.BlockSpec((None,) + shape, lambda i: (layer, 0, 0), pipeline_mode=pl.Buffered(1))
    full = lambda shape: pl.BlockSpec(shape, lambda i: (0, 0))
    return pl.pallas_call(
        _ffn_ln_kernel,
        grid=(t // TM_FFN,),
        in_specs=[pl.BlockSpec((TM_FFN, D_MODEL), lambda i: (i, 0)),
                  stacked((D_MODEL, D_FF)), stacked((D_MODEL, D_FF)), stacked((D_FF, D_MODEL)),
                  full((1, D_MODEL)), full((1, D_MODEL))],
        out_specs=pl.BlockSpec((TM_FFN, D_MODEL), lambda i: (i, 0)),
        out_shape=jax.ShapeDtypeStruct((t, D_MODEL), F32),
        compiler_params=_cparams(1),
        name="ffn_ln",
    )(x, wg, wu, wd, g, b)


IN_SPLITS = (D_RET, D_RET, D_RET, D_RET, D_DSA, D_DSA, D_DSA,
             IDX_HEADS * IDX_DIM, IDX_DIM, IDX_HEADS, D_DIFF, D_DIFF, D_DIFF)
N_IN = sum(IN_SPLITS)
N_IN_PAD = -(-N_IN // LANES) * LANES
_PROJ_OUT = (("rq", D_HEADS_PAD, F32, None), ("rk", D_HEADS_PAD, F32, None), ("rv", D_HEADS_PAD, BF16, None),
             ("rg", D_HEADS_PAD, F32, None),
             ("aq", D_HEADS_PAD, BF16, HEAD_DIM ** -0.5 * LOG2E), ("ak", D_HEADS_PAD, BF16, None),
             ("av", D_HEADS_PAD, BF16, None),
             ("iq", IDX_HEADS * IDX_DIM, BF16, None), ("ik", LANES, BF16, None), ("iw", LANES, F32, None),
             ("dq", D_HEADS_PAD, BF16, DIFF_QK_DIM ** -0.5 * LOG2E), ("dk", D_HEADS_PAD, BF16, None),
             ("dv", D_HEADS_PAD, BF16, None))


def _proj_kernel(x_ref, w_ref, *refs):
    o_refs, h_scr = refs[:-1], refs[-1]
    xb = x_ref[...].astype(BF16)
    for lo in range(0, N_IN, TN_PROJ):
        cols = pl.ds(lo, min(TN_PROJ, N_IN - lo))
        h_scr[:, cols] = _dot(xb, w_ref[:, cols].astype(BF16))
    off = 0
    for width, (name, out_width, dtype, scale), o_ref in zip(IN_SPLITS, _PROJ_OUT, o_refs):
        base = off // LANES * LANES
        end = min(-(-(off + width) // LANES) * LANES, N_IN_PAD)
        seg = h_scr[:, pl.ds(base, end - base)][:, off - base:off - base + width]
        if scale is not None:
            seg = seg * scale
        if name == "ik":
            seg = jnp.concatenate([seg, seg], axis=1)
        elif out_width > width:
            seg = jnp.concatenate([seg, jnp.zeros((TM_PROJ, out_width - width), F32)], axis=1)
        o_ref[...] = seg.astype(dtype)
        off += width


def _proj(x, w_in, layer):
    t = x.shape[0]
    return pl.pallas_call(
        _proj_kernel,
        grid=(t // TM_PROJ,),
        in_specs=[pl.BlockSpec((TM_PROJ, D_MODEL), lambda i: (i, 0)),
                  pl.BlockSpec((None, D_MODEL, N_IN), lambda i: (layer, 0, 0), pipeline_mode=pl.Buffered(1))],
        out_specs=[pl.BlockSpec((TM_PROJ, w), lambda i: (i, 0)) for _, w, _, _ in _PROJ_OUT],
        out_shape=[jax.ShapeDtypeStruct((t, w), dt) for _, w, dt, _ in _PROJ_OUT],
        scratch_shapes=[pltpu.VMEM((TM_PROJ, N_IN_PAD), F32)],
        compiler_params=_cparams(1),
        name="in_proj",
    )(x, w_in)


def _swap_halves(x):
    lane = lax.broadcasted_iota(jnp.int32, x.shape, 1)
    first_half = (lane % HEAD_DIM) < (HEAD_DIM // 2)
    from_right = pltpu.roll(x, LANES - HEAD_DIM // 2, 1)
    from_left = pltpu.roll(x, HEAD_DIM // 2, 1)
    return jnp.where(first_half, from_right, from_left)


def _retention_kernel(q_ref, k_ref, v_ref, g_ref, cos_ref, sin_ref, intra_ref, qdec_ref, kdec_ref,
                      sdec_ref, same_ref, avg_ref, o_ref, state_ref):
    @pl.when(pl.program_id(1) == 0)
    def _():
        state_ref[...] = jnp.zeros(state_ref.shape, F32)

    avg = avg_ref[...]
    group = _lane_group((TR, LANES))
    for p in range(RET_HEADS // PAIR):
        lanes = pl.ds(p * LANES, LANES)
        for blk in range(RET_BLOCKS):
            rows = pl.ds(blk * TR, TR)
            cos = cos_ref[rows, :]
            sin = sin_ref[rows, :]
            q = q_ref[0, rows, lanes]
            k = k_ref[0, rows, lanes]
            q = (q * cos + _swap_halves(q) * sin) * (HEAD_DIM ** -0.5)
            k = k * cos + _swap_halves(k) * sin
            v = v_ref[0, rows, lanes]
            kb = k.astype(BF16)
            intra = jnp.zeros((TR, LANES), F32)
            for j in range(PAIR):
                qh = jnp.where(group == j, q, 0.0).astype(BF16)
                scores = _dot_nt(qh, kb) * intra_ref[PAIR * p + j]
                oh = _dot(scores.astype(BF16), v)
                intra = jnp.where(group == j, oh, intra)
            state = state_ref[p]
            cross = _dot((q * qdec_ref[:, lanes]).astype(BF16), state.astype(BF16))
            o = intra + cross
            kd_t = (k * kdec_ref[:, lanes]).T.astype(BF16)
            state_ref[p] = state * sdec_ref[p] + _dot(kd_t, v) * same_ref[...]
            mu = _group_mean(o, avg)
            d = o - mu
            var = _group_mean(d * d, avg)
            normed = d * lax.rsqrt(var + HEAD_NORM_EPS)
            gate = g_ref[0, rows, lanes]
            o_ref[0, rows, lanes] = (gate * jax.nn.sigmoid(gate) * normed).astype(o_ref.dtype)


def _retention(rq, rk, rv, rg, batch, seq):
    cos, sin = _rotary_tables(seq)
    intra, qdec, kdec, sdec, same = _retention_tables()
    avg = jnp.asarray(_group_avg(), BF16)
    step = RET_BLOCKS * TR
    r3 = lambda a: a.reshape(batch, seq, D_HEADS_PAD)
    tok = pl.BlockSpec((1, step, D_HEADS_PAD), lambda b, n: (b, n, 0))
    const = lambda a: pl.BlockSpec(a.shape, lambda b, n: (0,) * a.ndim)
    pos = pl.BlockSpec((step, LANES), lambda b, n: (n, 0))
    out = pl.pallas_call(
        _retention_kernel,
        grid=(batch, seq // step),
        in_specs=[tok, tok, tok, tok, pos, pos, const(intra), const(qdec), const(kdec),
                  const(sdec), const(same), const(avg)],
        out_specs=tok,
        out_shape=jax.ShapeDtypeStruct((batch, seq, D_HEADS_PAD), BF16),
        scratch_shapes=[pltpu.VMEM((RET_HEADS // PAIR, LANES, LANES), F32)],
        compiler_params=_cparams(2),
        name="retention",
    )(r3(rq), r3(rk), r3(rv), r3(rg), cos, sin, intra, qdec, kdec, sdec, same, avg)
    return out.reshape(batch * seq, D_HEADS_PAD)


def _build_band(bidx_ref, tab_ref, band_ref, n_heads):
    for h in range(n_heads):
        far = tab_ref[h, FAR_BUCKET]
        for t in range(2):
            idx = bidx_ref[t]
            val = jnp.full((TK, TQ), NEG, F32)
            for bkt in range(N_BUCKETS):
                val = jnp.where(idx == bkt, (tab_ref[h, bkt] - far) * LOG2E, val)
            band_ref[h, t] = val


def _transposed_slab(ref, pair):
    return ref[0, :, pl.ds(pair * LANES, LANES)].astype(F32).T


def _load_queries(q_ref, q_scr, streams):
    row = lax.broadcasted_iota(jnp.int32, (LANES, TQ), 0)
    slabs = {}
    for i, (pair, _, lo, hi) in enumerate(streams):
        if pair not in slabs:
            slabs[pair] = _transposed_slab(q_ref, pair).astype(BF16)
        q_scr[i] = jnp.where((row >= lo) & (row < hi), slabs[pair], 0)


def _load_values(v_ref, vt_scr, n_heads, n_tiles):
    ones = jnp.ones((BF16_ROWS, TK), BF16)

    def tile(kt, carry):
        rows = pl.ds(pl.multiple_of(kt * TK, TK), TK)
        for pair in range(N_PAIRS):
            vt = v_ref[0, rows, pl.ds(pair * LANES, LANES)].astype(F32).T.astype(BF16)
            for j in range(PAIR):
                h = PAIR * pair + j
                if h < n_heads:
                    vt_scr[kt, pl.ds(h * VT_ROWS, HEAD_DIM), :] = vt[j * HEAD_DIM:(j + 1) * HEAD_DIM]
                    vt_scr[kt, pl.ds(h * VT_ROWS + HEAD_DIM, BF16_ROWS), :] = ones
        return carry

    lax.fori_loop(0, n_tiles, tile, 0)


def _for_tiles(n, fn):
    def pair(j, carry):
        fn(2 * j)
        fn(2 * j + 1)
        return carry

    lax.fori_loop(0, lax.shift_right_logical(n, 1), pair, 0)

    @pl.when(lax.rem(n, 2) == 1)
    def _():
        fn(n - 1)


def _softmax_passes(qb, k_ref, vt_ref, q_scr, band_ref, mask_fn, logit_ref, m_ref, acc_ref, streams):
    m_ref[...] = jnp.full(m_ref.shape, NEG, F32)

    def logits_tile(kt, band_t):
        rows = pl.ds(pl.multiple_of(kt * TK, TK), TK)
        mask = mask_fn(kt)
        for i, (pair, head, _, _) in enumerate(streams):
            lg = _dot(k_ref[0, rows, pl.ds(pair * LANES, LANES)], q_scr[i])
            if band_t is not None:
                lg = lg + band_ref[head, band_t]
            if mask is not None:
                lg = lg + mask
            logit_ref[kt, i] = lg
            m_ref[i] = jnp.maximum(m_ref[i], jnp.max(lg.reshape(TK // SUBLANES, SUBLANES, TQ), axis=0))

    _for_tiles(jnp.maximum(qb - 1, 0), lambda kt: logits_tile(kt, None))

    @pl.when(qb >= 1)
    def _():
        logits_tile(qb - 1, 0)
        logits_tile(qb, 1)

    @pl.when(qb == 0)
    def _():
        logits_tile(qb, 1)

    m_fin = [jnp.max(m_ref[i], axis=0, keepdims=True) for i in range(len(streams))]
    acc_ref[...] = jnp.zeros(acc_ref.shape, F32)

    def values_tile(kt):
        for i, (_, head, _, _) in enumerate(streams):
            p = jnp.exp2(logit_ref[kt, i] - m_fin[i])
            acc_ref[i] += _dot(vt_ref[kt, pl.ds(head * VT_ROWS, VT_ROWS), :], p.astype(BF16))

    _for_tiles(qb + 1, values_tile)


def _normalised(acc_ref, i):
    a = acc_ref[i]
    return a[:HEAD_DIM] / a[HEAD_DIM:HEAD_DIM + 1]


_DSA_STREAMS = tuple((h // PAIR, h, (h % PAIR) * HEAD_DIM, (h % PAIR + 1) * HEAD_DIM) for h in range(DSA_HEADS))


def _f32_to_key(x):
    bits = lax.bitcast_convert_type(x, jnp.int32)
    return jnp.where(bits < 0, ~bits, bits ^ jnp.int32(-2 ** 31))


def _key_to_f32(key):
    bits = jnp.where(key < 0, key ^ jnp.int32(-2 ** 31), ~key)
    return lax.bitcast_convert_type(bits, F32)


def _dsa_kernel(tab_ref, q_ref, k_ref, v_ref, iq_ref, ik_ref, iw_ref, bidx_ref, tril_ref,
                o_ref, band_ref, score_ref, q_scr, logit_ref, m_ref, acc_ref, tiecnt_ref, tie_ref, stat_ref, vt_ref):
    qb = pl.program_id(1)
    n_tiles = qb + 1

    @pl.when((pl.program_id(0) == 0) & (qb == 0))
    def _():
        _build_band(bidx_ref, tab_ref, band_ref, DSA_HEADS)

    @pl.when(qb == 0)
    def _():
        _load_values(v_ref, vt_ref, DSA_HEADS, k_ref.shape[1] // TK)

    iw = iw_ref[0].T[:SUBLANES] * (IDX_HEADS ** -0.5 * IDX_DIM ** -0.5)
    row = lax.broadcasted_iota(jnp.int32, (LANES, TQ), 0)
    iq_t = [_transposed_slab(iq_ref, p).astype(BF16) for p in range(IDX_HEADS // PAIR)]
    iq = [jnp.where(row // HEAD_DIM == (h % PAIR), iq_t[h // PAIR], 0) for h in range(IDX_HEADS)]

    def score_tile(kt):
        ki = ik_ref[0, pl.ds(pl.multiple_of(kt * TK, TK), TK), :]
        s = jnp.zeros((TK, TQ), F32)
        for h in range(IDX_HEADS):
            s = s + jnp.maximum(_dot(ki, iq[h]), 0.0) * iw[h:h + 1, :]
        return s

    stat_ref[0] = jnp.full((SUBLANES, TQ), -jnp.inf, F32)
    stat_ref[1] = jnp.full((SUBLANES, TQ), jnp.inf, F32)
    stat_ref[2] = jnp.zeros((SUBLANES, TQ), F32)
    stat_ref[3] = jnp.zeros((SUBLANES, TQ), F32)

    def add_stats(s_for_max, s_for_min):
        hi_part = s_for_max.reshape(TK // SUBLANES, SUBLANES, TQ)
        stat_ref[0] = jnp.maximum(stat_ref[0], jnp.max(hi_part, axis=0))
        stat_ref[1] = jnp.minimum(stat_ref[1], jnp.min(s_for_min.reshape(TK // SUBLANES, SUBLANES, TQ), axis=0))
        stat_ref[2] += jnp.sum(jnp.where(hi_part > 0.0, 1.0, 0.0), axis=0)
        stat_ref[3] += jnp.sum(jnp.where(hi_part >= 0.0, 1.0, 0.0), axis=0)

    def past_scores(kt):
        s = score_tile(kt)
        score_ref[kt] = s
        add_stats(s, s)

    _for_tiles(qb, past_scores)
    s_diag = score_tile(qb)
    visible = band_ref[0, 1] > 0.5 * NEG
    score_ref[qb] = jnp.where(visible, s_diag, -jnp.inf)
    add_stats(jnp.where(visible, s_diag, -jnp.inf), jnp.where(visible, s_diag, jnp.inf))

    k_sel = float(DSA_TOPK_MAX)

    def count(pred):
        def body(kt, c):
            xs = score_ref[kt].reshape(TK // SUBLANES, SUBLANES, TQ)
            accs = [c[a] for a in range(N_ACC)]
            for g in range(TK // SUBLANES):
                a = accs[g % N_ACC]
                accs[g % N_ACC] = jnp.where(pred(xs[g]), a + 1.0, a)
            return jnp.stack(accs)
        c = lax.fori_loop(0, n_tiles, body, jnp.zeros((N_ACC, SUBLANES, TQ), F32))
        return jnp.sum(jnp.sum(c, axis=0), axis=0, keepdims=True)

    col = lax.broadcasted_iota(jnp.int32, (1, TQ), 1)
    limit = qb * TQ + (col // CHUNK + 1) * CHUNK
    select_all = limit <= DSA_TOPK_MAX
    s_max = jnp.max(stat_ref[0], axis=0, keepdims=True)
    s_min = jnp.min(stat_ref[1], axis=0, keepdims=True)
    n_pos = jnp.sum(stat_ref[2], axis=0, keepdims=True)
    n_nonneg = jnp.sum(stat_ref[3], axis=0, keepdims=True)

    positive = n_pos > k_sel
    zero_thr = (n_pos <= k_sel) & (k_sel <= n_nonneg)
    lo0 = jnp.where(positive, 0.0, _key_to_f32(_f32_to_key(s_min) - 1))
    hi0 = jnp.where(positive, _key_to_f32(_f32_to_key(s_max) + 1), 0.0)
    flo0 = jnp.where(positive, n_pos, limit.astype(F32)) - k_sel
    fhi0 = jnp.where(positive, 0.0, n_nonneg) - k_sel
    done0 = jnp.where(select_all | zero_thr, 1.0, 0.0)
    thr0 = jnp.where(select_all, -jnp.inf, 0.0)

    def search_step(vec):
        lo, hi, flo, fhi, side, done, thr = vec
        w = flo / (flo - fhi)
        t = lo * (1.0 - w) + hi * w
        t = jnp.where((t > lo) & (t < hi), t, 0.5 * lo + 0.5 * hi)
        splits = (t > lo) & (t < hi)
        f = count(lambda s: s >= t) - k_sel
        active = done == 0.0
        hit = active & splits & (f == 0.0)
        adjacent = active & ~splits
        thr = jnp.where(hit, t, jnp.where(adjacent, lo, thr))
        done = jnp.where(hit | adjacent, 1.0, done)
        up = active & splits & (f > 0.0)
        down = active & splits & (f < 0.0)
        fhi_new = jnp.where(up & (side > 0.0), 0.5 * fhi, jnp.where(down, f, fhi))
        flo_new = jnp.where(down & (side < 0.0), 0.5 * flo, jnp.where(up, f, flo))
        lo = jnp.where(up, t, lo)
        hi = jnp.where(down, t, hi)
        side = jnp.where(up, 1.0, jnp.where(down, -1.0, side))
        return lo, hi, flo_new, fhi_new, side, done, thr

    def open_queries(done):
        return (jnp.min(done) == 0.0).astype(jnp.int32)

    def search_round(state):
        rounds, _, vec = state
        for _ in range(SEARCH_STEPS_PER_CHECK):
            vec = search_step(vec)
        return rounds + 1, open_queries(vec[5]), vec

    vec0 = (lo0, hi0, flo0, fhi0, jnp.zeros((1, TQ), F32), done0, thr0)
    state = lax.while_loop(lambda st: (st[0] < SEARCH_ROUNDS) & (st[1] > 0), search_round,
                           (jnp.int32(0), open_queries(done0), vec0))
    done, thr = state[2][5], state[2][6]

    def bisect(i, prefix):
        cand = prefix | lax.shift_left(jnp.int32(1), 31 - i)
        cand_f = _key_to_f32(cand)
        take = count(lambda s: s >= cand_f) >= k_sel
        return jnp.where(take, cand, prefix)

    def exact_thr():
        prefix = lax.fori_loop(0, 32, bisect, jnp.zeros((1, TQ), jnp.int32))
        return jnp.where(done == 0.0, _key_to_f32(prefix), thr)

    thr = lax.cond(state[1] > 0, exact_thr, lambda: thr)
    n_gt = count(lambda s: s > thr)
    need = jnp.where(select_all, 0.0, k_sel - n_gt)

    def tie_counts(kt, carry):
        tied = jnp.where(score_ref[kt] == thr, 1.0, 0.0).reshape(TK // SUBLANES, SUBLANES, TQ)
        tiecnt_ref[kt] = jnp.sum(jnp.sum(tied, axis=0), axis=0, keepdims=True)
        return carry

    lax.fori_loop(0, n_tiles, tie_counts, 0)
    tie_ref[...] = jnp.zeros(tie_ref.shape, F32)
    tril = tril_ref[...]

    def selection_mask(kt):
        s = score_ref[kt]
        tied = s == thr
        tied_before = tie_ref[...]
        tie_ref[...] = tied_before + tiecnt_ref[kt]
        tied_upto = tied_before + _dot(tril, jnp.where(tied, 1.0, 0.0).astype(BF16))
        return jnp.where((s > thr) | (tied & (tied_upto <= need)), 0.0, NEG)

    _load_queries(q_ref, q_scr, _DSA_STREAMS)
    _softmax_passes(qb, k_ref, vt_ref, q_scr, band_ref, selection_mask, logit_ref, m_ref, acc_ref, _DSA_STREAMS)

    for p in range(N_PAIRS):
        halves = [_normalised(acc_ref, PAIR * p + j) if PAIR * p + j < DSA_HEADS
                  else jnp.zeros((HEAD_DIM, TQ), F32) for j in range(PAIR)]
        o_ref[0, :, pl.ds(p * LANES, LANES)] = jnp.concatenate(halves, axis=0).T.astype(o_ref.dtype)


def _dsa(aq, ak, av, iq, ik, iw, bias_t, batch, seq):
    bidx = jnp.asarray(_band_bucket_index_t())
    tril = jnp.asarray(_prefix_tril(), BF16)
    n_streams = len(_DSA_STREAMS)
    r3 = lambda a: a.reshape(batch, seq, a.shape[-1])
    qblk = lambda w: pl.BlockSpec((1, TQ, w), lambda b, n: (b, n, 0))
    kblk = lambda w: pl.BlockSpec((1, seq, w), lambda b, n: (b, 0, 0))
    const = lambda a: pl.BlockSpec(a.shape, lambda b, n: (0,) * a.ndim)
    out = pl.pallas_call(
        _dsa_kernel,
        grid=(batch, seq // TQ),
        in_specs=[pl.BlockSpec(memory_space=pltpu.SMEM),
                  qblk(D_HEADS_PAD), kblk(D_HEADS_PAD), kblk(D_HEADS_PAD),
                  qblk(IDX_HEADS * IDX_DIM), kblk(LANES), qblk(LANES), const(bidx), const(tril)],
        out_specs=qblk(D_HEADS_PAD),
        out_shape=jax.ShapeDtypeStruct((batch, seq, D_HEADS_PAD), BF16),
        scratch_shapes=[pltpu.VMEM((DSA_HEADS, 2, TK, TQ), F32),
                        pltpu.VMEM((seq // TK, TK, TQ), F32),
                        pltpu.VMEM((n_streams, LANES, TQ), BF16),
                        pltpu.VMEM((seq // TK, n_streams, TK, TQ), F32),
                        pltpu.VMEM((n_streams, SUBLANES, TQ), F32),
                        pltpu.VMEM((n_streams, VT_ROWS, TQ), F32),
                        pltpu.VMEM((seq // TK, 1, TQ), F32),
                        pltpu.VMEM((1, TQ), F32),
                        pltpu.VMEM((4, SUBLANES, TQ), F32),
                        pltpu.VMEM((seq // TK, DSA_HEADS * VT_ROWS, TK), BF16)],
        compiler_params=_cparams(2),
        name="dsa_attention",
    )(bias_t, r3(aq), r3(ak), r3(av), r3(iq), r3(ik), r3(iw), bidx, tril)
    return out.reshape(batch * seq, D_HEADS_PAD)


_DIFF_STREAMS = tuple(
    (h // PAIR, h, (h % PAIR) * HEAD_DIM + mm * DIFF_QK_DIM, (h % PAIR) * HEAD_DIM + (mm + 1) * DIFF_QK_DIM)
    for h in range(DIFF_HEADS) for mm in range(2))


def _diff_kernel(lambda_init, tab_ref, q_ref, k_ref, v_ref, lam_ref, g_ref, bidx_ref, avg_ref,
                 o_ref, band_ref, q_scr, logit_ref, m_ref, acc_ref, vt_ref):
    qb = pl.program_id(1)

    @pl.when((pl.program_id(0) == 0) & (qb == 0))
    def _():
        _build_band(bidx_ref, tab_ref, band_ref, DIFF_HEADS)

    @pl.when(qb == 0)
    def _():
        _load_values(v_ref, vt_ref, DIFF_HEADS, k_ref.shape[1] // TK)

    lv = lam_ref[...]
    lam = (jnp.exp(jnp.sum(lv[0:1] * lv[1:2], axis=-1, keepdims=True))
           - jnp.exp(jnp.sum(lv[2:3] * lv[3:4], axis=-1, keepdims=True)) + lambda_init)

    _load_queries(q_ref, q_scr, _DIFF_STREAMS)
    _softmax_passes(qb, k_ref, vt_ref, q_scr, band_ref, lambda kt: None, logit_ref, m_ref, acc_ref,
                    _DIFF_STREAMS)

    avg = avg_ref[...]
    for p in range(N_PAIRS):
        halves = []
        for j in range(PAIR):
            h = PAIR * p + j
            if h < DIFF_HEADS:
                halves.append(_normalised(acc_ref, 2 * h) - lam * _normalised(acc_ref, 2 * h + 1))
            else:
                halves.append(jnp.zeros((HEAD_DIM, TQ), F32))
        out = jnp.concatenate(halves, axis=0).T
        ms = _group_mean(out * out, avg)
        out = out * lax.rsqrt(ms + LN_EPS) * g_ref[...] * (1.0 - lambda_init)
        o_ref[0, :, pl.ds(p * LANES, LANES)] = out.astype(o_ref.dtype)


def _diff(dq, dk, dv, lam_vecs, subln_g, bias_t, lambda_init, batch, seq):
    bidx = jnp.asarray(_band_bucket_index_t())
    avg = jnp.asarray(_group_avg(), BF16)
    g2 = jnp.concatenate([subln_g, subln_g]).reshape(1, LANES)
    n_streams = len(_DIFF_STREAMS)
    r3 = lambda a: a.reshape(batch, seq, a.shape[-1])
    qblk = pl.BlockSpec((1, TQ, D_HEADS_PAD), lambda b, n: (b, n, 0))
    kblk = pl.BlockSpec((1, seq, D_HEADS_PAD), lambda b, n: (b, 0, 0))
    const = lambda a: pl.BlockSpec(a.shape, lambda b, n: (0,) * a.ndim)
    out = pl.pallas_call(
        functools.partial(_diff_kernel, lambda_init),
        grid=(batch, seq // TQ),
        in_specs=[pl.BlockSpec(memory_space=pltpu.SMEM), qblk, kblk, kblk,
                  const(lam_vecs), const(g2), const(bidx), const(avg)],
        out_specs=qblk,
        out_shape=jax.ShapeDtypeStruct((batch, seq, D_HEADS_PAD), BF16),
        scratch_shapes=[pltpu.VMEM((DIFF_HEADS, 2, TK, TQ), F32),
                        pltpu.VMEM((n_streams, LANES, TQ), BF16),
                        pltpu.VMEM((seq // TK, n_streams, TK, TQ), F32),
                        pltpu.VMEM((n_streams, SUBLANES, TQ), F32),
                        pltpu.VMEM((n_streams, VT_ROWS, TQ), F32),
                        pltpu.VMEM((seq // TK, DIFF_HEADS * VT_ROWS, TK), BF16)],
        compiler_params=_cparams(2),
        name="diff_attention",
    )(bias_t, r3(dq), r3(dk), r3(dv), lam_vecs, g2, bidx, avg)
    return out.reshape(batch * seq, D_HEADS_PAD)


def _out_ffn_kernel(x_ref, ret_ref, dsa_ref, dif_ref, wo_ref, g1_ref, b1_ref, wg_ref, wu_ref, wd_ref,
                    g2_ref, b2_ref, o_ref):
    mixed = _dot(ret_ref[...], wo_ref[pl.ds(0, D_HEADS_PAD), :])
    mixed = mixed + _dot(dsa_ref[...], wo_ref[pl.ds(D_HEADS_PAD, D_HEADS_PAD), :])
    mixed = mixed + _dot(dif_ref[...], wo_ref[pl.ds(2 * D_HEADS_PAD, D_HEADS_PAD), :])
    x = _layer_norm(DEEPNORM_ALPHA * x_ref[...] + mixed, g1_ref[...], b1_ref[...])
    o_ref[...] = _ffn_ln_body(x, wg_ref, wu_ref, wd_ref, g2_ref[...], b2_ref[...])


def _pad_w_out(w_out):
    zeros = jnp.zeros((D_HEADS_PAD - D_DSA, w_out.shape[1]), w_out.dtype)
    return jnp.concatenate([w_out[:D_RET], w_out[D_RET:D_RET + D_DSA], zeros,
                            w_out[D_RET + D_DSA:], zeros], axis=0)


def _out_ffn(x, ret, dsa, dif, w_pad, g1, b1, wg, wu, wd, g2, b2, layer):
    t = x.shape[0]
    stacked = lambda shape: pl.BlockSpec((None,) + shape, lambda i: (layer, 0, 0), pipeline_mode=pl.Buffered(1))
    const = lambda shape: pl.BlockSpec(shape, lambda i: (0, 0), pipeline_mode=pl.Buffered(1))
    tok = lambda w: pl.BlockSpec((TM_FFN, w), lambda i: (i, 0))
    return pl.pallas_call(
        _out_ffn_kernel,
        grid=(t // TM_FFN,),
        in_specs=[tok(D_MODEL), tok(D_HEADS_PAD), tok(D_HEADS_PAD), tok(D_HEADS_PAD),
                  const((3 * D_HEADS_PAD, D_MODEL)), const((1, D_MODEL)), const((1, D_MODEL)),
                  stacked((D_MODEL, D_FF)), stacked((D_MODEL, D_FF)), stacked((D_FF, D_MODEL)),
                  const((1, D_MODEL)), const((1, D_MODEL))],
        out_specs=tok(D_MODEL),
        out_shape=jax.ShapeDtypeStruct((t, D_MODEL), F32),
        compiler_params=_cparams(1),
        name="out_proj_ffn_ln",
    )(x, ret, dsa, dif, w_pad, g1, b1, wg, wu, wd, g2, b2)


def kernel(x, w_in, w_out, ffn1_wg, ffn1_wu, ffn1_wd, ffn2_wg, ffn2_wu, ffn2_wd,
           ln_g, ln_b, diff_lambda, diff_subln_g, rel_bias):
    batch, seq, d = x.shape
    assert d == D_MODEL and seq % TQ == 0 and seq % (RET_BLOCKS * TR) == 0 and (batch * seq) % TM_FFN == 0
    assert min(DSA_TOPK_MAX, seq // 4) == DSA_TOPK_MAX
    h = x.reshape(batch * seq, d)
    bias_t = rel_bias.T
    row = lambda a: a.reshape(1, D_MODEL)
    for l in range(DEPTH):
        lambda_init = 0.8 - 0.6 * math.exp(-0.3 * l)
        h = _ffn_ln(h, ffn1_wg, ffn1_wu, ffn1_wd, row(ln_g[l, 0]), row(ln_b[l, 0]), l)
        (rq, rk, rv, rg, aq, ak, av, iq, ik, iw, dq, dk, dv) = _proj(h, w_in, l)
        ret = _retention(rq, rk, rv, rg, batch, seq)
        dsa = _dsa(aq, ak, av, iq, ik, iw, bias_t[:DSA_HEADS], batch, seq)
        dif = _diff(dq, dk, dv, diff_lambda[l], diff_subln_g[l], bias_t[DSA_HEADS:], lambda_init, batch, seq)
        h = _out_ffn(h, ret, dsa, dif, _pad_w_out(w_out[l]).astype(BF16), row(ln_g[l, 1]), row(ln_b[l, 1]),
                     ffn2_wg, ffn2_wu, ffn2_wd, row(ln_g[l, 2]), row(ln_b[l, 2]), l)
    return h.reshape(batch, seq, d)
```

```python
import functools
import math

import numpy as np
import jax
import jax.numpy as jnp
from jax import lax
from jax.experimental import pallas as pl
from jax.experimental.pallas import tpu as pltpu

D_MODEL = 1024
DEPTH = 2
CHUNK = 64
HEAD_DIM = 64
RET_HEADS = 6
DSA_HEADS = 5
IDX_HEADS = 4
IDX_DIM = 64
DSA_TOPK_MAX = 256
DIFF_HEADS = 5
DIFF_QK_DIM = HEAD_DIM // 2
D_RET = RET_HEADS * HEAD_DIM
D_DSA = DSA_HEADS * HEAD_DIM
D_DIFF = DIFF_HEADS * HEAD_DIM
D_FF = 2816
N_BUCKETS = 32
MAX_DISTANCE = 128
ROPE_BASE = 10000.0
LN_EPS = 1e-5
HEAD_NORM_EPS = 1e-6
DEEPNORM_ALPHA = (2 * DEPTH) ** 0.25

LANES = 128
SUBLANES = 8
BF16_ROWS = 16
PAIR = LANES // HEAD_DIM
VMEM_LIMIT_BYTES = 56 * 1024 * 1024

D_HEADS_PAD = 3 * LANES
N_PAIRS = D_HEADS_PAD // LANES
VT_ROWS = HEAD_DIM + BF16_ROWS
TQ = 256
TK = 256
TR = 256
RET_BLOCKS = 4
TM_FFN = 512
TF = 256
TM_PROJ = 512
TN_PROJ = 256
N_ACC = 4
SEARCH_STEPS_PER_CHECK = 4
SEARCH_ROUNDS = 6
NEG = -1e30
LOG2E = math.log2(math.e)

F32 = jnp.float32
BF16 = jnp.bfloat16


def _dot(a, b):
    return jnp.dot(a, b, preferred_element_type=F32)


def _dot_nt(a, b):
    return lax.dot_general(a, b, (((1,), (1,)), ((), ())), preferred_element_type=F32)


def _layer_norm(r, g, b):
    mu = jnp.mean(r, axis=-1, keepdims=True)
    d = r - mu
    var = jnp.mean(d * d, axis=-1, keepdims=True)
    return d * lax.rsqrt(var + LN_EPS) * g + b


def _group_mean(x, avg_bf16):
    hi = x.astype(BF16)
    lo = (x - hi.astype(F32)).astype(BF16)
    return _dot(hi, avg_bf16) + _dot(lo, avg_bf16)


def _cparams(n_axes):
    return pltpu.CompilerParams(
        dimension_semantics=("arbitrary",) * n_axes,
        vmem_limit_bytes=VMEM_LIMIT_BYTES,
    )


def _rel_bucket_static(rel):
    rel = np.asarray(rel, dtype=np.int64)
    half = N_BUCKETS // 2
    max_exact = half // 2
    n = np.abs(rel)
    large = np.full(n.shape, max_exact, dtype=np.int64)
    for k in range(1, 64):
        large = np.where(64 * (2 ** k) <= n * n, max_exact + k, large)
    large = np.minimum(large, half - 1)
    return (np.where(rel > 0, half, 0) + np.where(n < max_exact, n, large)).astype(np.int32)


FAR_BUCKET = int(_rel_bucket_static(-(TK + 1)))


@functools.lru_cache(maxsize=None)
def _band_bucket_index_t():
    j = np.arange(TK)[:, None]
    i = np.arange(TQ)[None, :]
    prev = _rel_bucket_static(j - TK - i)
    diag = _rel_bucket_static(j - i)
    visible = j < (i // CHUNK + 1) * CHUNK
    diag = np.where(visible, diag, -1)
    return np.stack([prev, diag]).astype(np.int32)


@functools.lru_cache(maxsize=None)
def _rotary_tables(seq):
    lane = np.arange(LANES)
    d = lane % HEAD_DIM
    f = d % (HEAD_DIM // 2)
    inv = ROPE_BASE ** (-(2.0 * f) / HEAD_DIM)
    ang = np.arange(seq, dtype=np.float64)[:, None] * inv[None, :]
    cos = np.cos(ang)
    sin = np.sin(ang) * np.where(d < HEAD_DIM // 2, -1.0, 1.0)[None, :]
    return cos.astype(np.float32), sin.astype(np.float32)


@functools.lru_cache(maxsize=None)
def _retention_tables():
    gamma = 1.0 - 2.0 ** (-5.0 - np.arange(RET_HEADS, dtype=np.float64))
    t = np.arange(TR)
    dist = np.abs(t[:, None] - t[None, :]).astype(np.float64)
    visible = t[None, :] < (t[:, None] // CHUNK + 1) * CHUNK
    intra = np.stack([np.where(visible, g ** dist, 0.0) for g in gamma])
    lane_head = np.arange(D_HEADS_PAD) // HEAD_DIM
    g_lane = gamma[lane_head]
    qdec = g_lane[None, :] ** (t[:, None] + 1.0)
    kdec = g_lane[None, :] ** (TR - 1.0 - t[:, None])
    a = np.arange(LANES)
    same = (a[:, None] // HEAD_DIM) == (a[None, :] // HEAD_DIM)
    sdec = np.stack([np.where(same, (gamma[PAIR * p + a // HEAD_DIM] ** TR)[:, None], 0.0)
                     for p in range(RET_HEADS // PAIR)])
    return (intra.astype(np.float32), qdec.astype(np.float32), kdec.astype(np.float32),
            sdec.astype(np.float32), same.astype(np.float32))


@functools.lru_cache(maxsize=None)
def _group_avg():
    a = np.arange(LANES)
    same = (a[:, None] // HEAD_DIM) == (a[None, :] // HEAD_DIM)
    return np.where(same, 1.0 / HEAD_DIM, 0.0).astype(np.float32)


@functools.lru_cache(maxsize=None)
def _prefix_tril():
    a = np.arange(TK)
    return (a[None, :] <= a[:, None]).astype(np.float32)


def _lane_group(shape):
    return lax.broadcasted_iota(jnp.int32, shape, len(shape) - 1) // HEAD_DIM


def _ffn_ln_body(x, wg_ref, wu_ref, wd_ref, g, b):
    xb = x.astype(BF16)
    acc = jnp.zeros(x.shape, F32)
    for c in range(D_FF // TF):
        cols = pl.ds(c * TF, TF)
        gate = _dot(xb, wg_ref[:, cols].astype(BF16))
        up = _dot(xb, wu_ref[:, cols].astype(BF16))
        h = gate * jax.nn.sigmoid(gate) * up
        acc = acc + _dot(h.astype(BF16), wd_ref[cols, :].astype(BF16))
    return _layer_norm(DEEPNORM_ALPHA * x + 0.5 * acc, g, b)


def _ffn_ln_kernel(x_ref, wg_ref, wu_ref, wd_ref, g_ref, b_ref, o_ref):
    o_ref[...] = _ffn_ln_body(x_ref[...], wg_ref, wu_ref, wd_ref, g_ref[...], b_ref[...])


def _ffn_ln(x, wg, wu, wd, g, b, layer):
    t = x.shape[0]
    stacked = lambda shape: pl.BlockSpec((None,) + shape, lambda i: (layer, 0, 0), pipeline_mode=pl.Buffered(1))
    full = lambda shape: pl.BlockSpec(shape, lambda i: (0, 0))
    return pl.pallas_call(
        _ffn_ln_kernel,
        grid=(t // TM_FFN,),
        in_specs=[pl.BlockSpec((TM_FFN, D_MODEL), lambda i: (i, 0)),
                  stacked((D_MODEL, D_FF)), stacked((D_MODEL, D_FF)), stacked((D_FF, D_MODEL)),
                  full((1, D_MODEL)), full((1, D_MODEL))],
        out_specs=pl.BlockSpec((TM_FFN, D_MODEL), lambda i: (i, 0)),
        out_shape=jax.ShapeDtypeStruct((t, D_MODEL), F32),
        compiler_params=_cparams(1),
        name="ffn_ln",
    )(x, wg, wu, wd, g, b)


IN_SPLITS = (D_RET, D_RET, D_RET, D_RET, D_DSA, D_DSA, D_DSA,
             IDX_HEADS * IDX_DIM, IDX_DIM, IDX_HEADS, D_DIFF, D_DIFF, D_DIFF)
N_IN = sum(IN_SPLITS)
N_IN_PAD = -(-N_IN // LANES) * LANES
_PROJ_OUT = (("rq", D_HEADS_PAD, F32, None), ("rk", D_HEADS_PAD, F32, None), ("rv", D_HEADS_PAD, BF16, None),
             ("rg", D_HEADS_PAD, F32, None),
             ("aq", D_HEADS_PAD, BF16, HEAD_DIM ** -0.5 * LOG2E), ("ak", D_HEADS_PAD, BF16, None),
             ("av", D_HEADS_PAD, BF16, None),
             ("iq", IDX_HEADS * IDX_DIM, BF16, None), ("ik", LANES, BF16, None), ("iw", LANES, F32, None),
             ("dq", D_HEADS_PAD, BF16, DIFF_QK_DIM ** -0.5 * LOG2E), ("dk", D_HEADS_PAD, BF16, None),
             ("dv", D_HEADS_PAD, BF16, None))


def _proj_kernel(x_ref, w_ref, *refs):
    o_refs, h_scr = refs[:-1], refs[-1]
    xb = x_ref[...].astype(BF16)
    for lo in range(0, N_IN, TN_PROJ):
        cols = pl.ds(lo, min(TN_PROJ, N_IN - lo))
        h_scr[:, cols] = _dot(xb, w_ref[:, cols].astype(BF16))
    off = 0
    for width, (name, out_width, dtype, scale), o_ref in zip(IN_SPLITS, _PROJ_OUT, o_refs):
        base = off // LANES * LANES
        end = min(-(-(off + width) // LANES) * LANES, N_IN_PAD)
        seg = h_scr[:, pl.ds(base, end - base)][:, off - base:off - base + width]
        if scale is not None:
            seg = seg * scale
        if name == "ik":
            seg = jnp.concatenate([seg, seg], axis=1)
        elif out_width > width:
            seg = jnp.concatenate([seg, jnp.zeros((TM_PROJ, out_width - width), F32)], axis=1)
        o_ref[...] = seg.astype(dtype)
        off += width


def _proj(x, w_in, layer):
    t = x.shape[0]
    return pl.pallas_call(
        _proj_kernel,
        grid=(t // TM_PROJ,),
        in_specs=[pl.BlockSpec((TM_PROJ, D_MODEL), lambda i: (i, 0)),
                  pl.BlockSpec((None, D_MODEL, N_IN), lambda i: (layer, 0, 0), pipeline_mode=pl.Buffered(1))],
        out_specs=[pl.BlockSpec((TM_PROJ, w), lambda i: (i, 0)) for _, w, _, _ in _PROJ_OUT],
        out_shape=[jax.ShapeDtypeStruct((t, w), dt) for _, w, dt, _ in _PROJ_OUT],
        scratch_shapes=[pltpu.VMEM((TM_PROJ, N_IN_PAD), F32)],
        compiler_params=_cparams(1),
        name="in_proj",
    )(x, w_in)


def _swap_halves(x):
    lane = lax.broadcasted_iota(jnp.int32, x.shape, 1)
    first_half = (lane % HEAD_DIM) < (HEAD_DIM // 2)
    from_right = pltpu.roll(x, LANES - HEAD_DIM // 2, 1)
    from_left = pltpu.roll(x, HEAD_DIM // 2, 1)
    return jnp.where(first_half, from_right, from_left)


def _retention_kernel(q_ref, k_ref, v_ref, g_ref, cos_ref, sin_ref, intra_ref, qdec_ref, kdec_ref,
                      sdec_ref, same_ref, avg_ref, o_ref, state_ref):
    @pl.when(pl.program_id(1) == 0)
    def _():
        state_ref[...] = jnp.zeros(state_ref.shape, F32)

    avg = avg_ref[...]
    group = _lane_group((TR, LANES))
    for p in range(RET_HEADS // PAIR):
        lanes = pl.ds(p * LANES, LANES)
        for blk in range(RET_BLOCKS):
            rows = pl.ds(blk * TR, TR)
            cos = cos_ref[rows, :]
            sin = sin_ref[rows, :]
            q = q_ref[0, rows, lanes]
            k = k_ref[0, rows, lanes]
            q = (q * cos + _swap_halves(q) * sin) * (HEAD_DIM ** -0.5)
            k = k * cos + _swap_halves(k) * sin
            v = v_ref[0, rows, lanes]
            kb = k.astype(BF16)
            intra = jnp.zeros((TR, LANES), F32)
            for j in range(PAIR):
                qh = jnp.where(group == j, q, 0.0).astype(BF16)
                scores = _dot_nt(qh, kb) * intra_ref[PAIR * p + j]
                oh = _dot(scores.astype(BF16), v)
                intra = jnp.where(group == j, oh, intra)
            state = state_ref[p]
            cross = _dot((q * qdec_ref[:, lanes]).astype(BF16), state.astype(BF16))
            o = intra + cross
            kd_t = (k * kdec_ref[:, lanes]).T.astype(BF16)
            state_ref[p] = state * sdec_ref[p] + _dot(kd_t, v) * same_ref[...]
            mu = _group_mean(o, avg)
            d = o - mu
            var = _group_mean(d * d, avg)
            normed = d * lax.rsqrt(var + HEAD_NORM_EPS)
            gate = g_ref[0, rows, lanes]
            o_ref[0, rows, lanes] = (gate * jax.nn.sigmoid(gate) * normed).astype(o_ref.dtype)


def _retention(rq, rk, rv, rg, batch, seq):
    cos, sin = _rotary_tables(seq)
    intra, qdec, kdec, sdec, same = _retention_tables()
    avg = jnp.asarray(_group_avg(), BF16)
    step = RET_BLOCKS * TR
    r3 = lambda a: a.reshape(batch, seq, D_HEADS_PAD)
    tok = pl.BlockSpec((1, step, D_HEADS_PAD), lambda b, n: (b, n, 0))
    const = lambda a: pl.BlockSpec(a.shape, lambda b, n: (0,) * a.ndim)
    pos = pl.BlockSpec((step, LANES), lambda b, n: (n, 0))
    out = pl.pallas_call(
        _retention_kernel,
        grid=(batch, seq // step),
        in_specs=[tok, tok, tok, tok, pos, pos, const(intra), const(qdec), const(kdec),
                  const(sdec), const(same), const(avg)],
        out_specs=tok,
        out_shape=jax.ShapeDtypeStruct((batch, seq, D_HEADS_PAD), BF16),
        scratch_shapes=[pltpu.VMEM((RET_HEADS // PAIR, LANES, LANES), F32)],
        compiler_params=_cparams(2),
        name="retention",
    )(r3(rq), r3(rk), r3(rv), r3(rg), cos, sin, intra, qdec, kdec, sdec, same, avg)
    return out.reshape(batch * seq, D_HEADS_PAD)


def _build_band(bidx_ref, tab_ref, band_ref, n_heads):
    for h in range(n_heads):
        far = tab_ref[h, FAR_BUCKET]
        for t in range(2):
            idx = bidx_ref[t]
            val = jnp.full((TK, TQ), NEG, F32)
            for bkt in range(N_BUCKETS):
                val = jnp.where(idx == bkt, (tab_ref[h, bkt] - far) * LOG2E, val)
            band_ref[h, t] = val


def _transposed_slab(ref, pair):
    return ref[0, :, pl.ds(pair * LANES, LANES)].astype(F32).T


def _load_queries(q_ref, q_scr, streams):
    row = lax.broadcasted_iota(jnp.int32, (LANES, TQ), 0)
    slabs = {}
    for i, (pair, _, lo, hi) in enumerate(streams):
        if pair not in slabs:
            slabs[pair] = _transposed_slab(q_ref, pair).astype(BF16)
        q_scr[i] = jnp.where((row >= lo) & (row < hi), slabs[pair], 0)


def _load_values(v_ref, vt_scr, n_heads, n_tiles):
    ones = jnp.ones((BF16_ROWS, TK), BF16)

    def tile(kt, carry):
        rows = pl.ds(pl.multiple_of(kt * TK, TK), TK)
        for pair in range(N_PAIRS):
            vt = v_ref[0, rows, pl.ds(pair * LANES, LANES)].astype(F32).T.astype(BF16)
            for j in range(PAIR):
                h = PAIR * pair + j
                if h < n_heads:
                    vt_scr[kt, pl.ds(h * VT_ROWS, HEAD_DIM), :] = vt[j * HEAD_DIM:(j + 1) * HEAD_DIM]
                    vt_scr[kt, pl.ds(h * VT_ROWS + HEAD_DIM, BF16_ROWS), :] = ones
        return carry

    lax.fori_loop(0, n_tiles, tile, 0)


def _for_tiles(n, fn):
    def quad(j, carry):
        for u in range(4):
            fn(4 * j + u)
        return carry

    lax.fori_loop(0, lax.shift_right_logical(n, 2), quad, 0)
    rest = lax.rem(n, 4)

    @pl.when(rest >= 2)
    def _():
        fn(n - rest)
        fn(n - rest + 1)

    @pl.when(lax.rem(rest, 2) == 1)
    def _():
        fn(n - 1)


def _softmax_passes(qb, k_ref, vt_ref, q_scr, band_ref, mask_fn, logit_ref, m_ref, acc_ref, streams):
    m_ref[...] = jnp.full(m_ref.shape, NEG, F32)

    def logits_tile(kt, band_t):
        rows = pl.ds(pl.multiple_of(kt * TK, TK), TK)
        mask = mask_fn(kt)
        for i, (pair, head, _, _) in enumerate(streams):
            lg = _dot(k_ref[0, rows, pl.ds(pair * LANES, LANES)], q_scr[i])
            if band_t is not None:
                lg = lg + band_ref[head, band_t]
            if mask is not None:
                lg = lg + mask
            logit_ref[kt, i] = lg
            m_ref[i] = jnp.maximum(m_ref[i], jnp.max(lg.reshape(TK // SUBLANES, SUBLANES, TQ), axis=0))

    _for_tiles(jnp.maximum(qb - 1, 0), lambda kt: logits_tile(kt, None))

    @pl.when(qb >= 1)
    def _():
        logits_tile(qb - 1, 0)
        logits_tile(qb, 1)

    @pl.when(qb == 0)
    def _():
        logits_tile(qb, 1)

    m_fin = [jnp.max(m_ref[i], axis=0, keepdims=True) for i in range(len(streams))]
    acc_ref[...] = jnp.zeros(acc_ref.shape, F32)

    def values_tile(kt):
        for i, (_, head, _, _) in enumerate(streams):
            p = jnp.exp2(logit_ref[kt, i] - m_fin[i])
            acc_ref[i] += _dot(vt_ref[kt, pl.ds(head * VT_ROWS, VT_ROWS), :], p.astype(BF16))

    _for_tiles(qb + 1, values_tile)


def _normalised(acc_ref, i):
    a = acc_ref[i]
    return a[:HEAD_DIM] / a[HEAD_DIM:HEAD_DIM + 1]


_DSA_STREAMS = tuple((h // PAIR, h, (h % PAIR) * HEAD_DIM, (h % PAIR + 1) * HEAD_DIM) for h in range(DSA_HEADS))


def _f32_to_key(x):
    bits = lax.bitcast_convert_type(x, jnp.int32)
    return jnp.where(bits < 0, ~bits, bits ^ jnp.int32(-2 ** 31))


def _key_to_f32(key):
    bits = jnp.where(key < 0, key ^ jnp.int32(-2 ** 31), ~key)
    return lax.bitcast_convert_type(bits, F32)


def _dsa_kernel(tab_ref, q_ref, k_ref, v_ref, iq_ref, ik_ref, iw_ref, bidx_ref, tril_ref,
                o_ref, band_ref, score_ref, q_scr, logit_ref, m_ref, acc_ref, tiecnt_ref, tie_ref, stat_ref, vt_ref):
    qb = pl.program_id(1)
    n_tiles = qb + 1

    @pl.when((pl.program_id(0) == 0) & (qb == 0))
    def _():
        _build_band(bidx_ref, tab_ref, band_ref, DSA_HEADS)

    @pl.when(qb == 0)
    def _():
        _load_values(v_ref, vt_ref, DSA_HEADS, k_ref.shape[1] // TK)

    iw = iw_ref[0].T[:SUBLANES] * (IDX_HEADS ** -0.5 * IDX_DIM ** -0.5)
    row = lax.broadcasted_iota(jnp.int32, (LANES, TQ), 0)
    iq_t = [_transposed_slab(iq_ref, p).astype(BF16) for p in range(IDX_HEADS // PAIR)]
    iq = [jnp.where(row // HEAD_DIM == (h % PAIR), iq_t[h // PAIR], 0) for h in range(IDX_HEADS)]

    def score_tile(kt):
        ki = ik_ref[0, pl.ds(pl.multiple_of(kt * TK, TK), TK), :]
        s = jnp.zeros((TK, TQ), F32)
        for h in range(IDX_HEADS):
            s = s + jnp.maximum(_dot(ki, iq[h]), 0.0) * iw[h:h + 1, :]
        return s

    stat_ref[0] = jnp.full((SUBLANES, TQ), -jnp.inf, F32)
    stat_ref[1] = jnp.full((SUBLANES, TQ), jnp.inf, F32)
    stat_ref[2] = jnp.zeros((SUBLANES, TQ), F32)
    stat_ref[3] = jnp.zeros((SUBLANES, TQ), F32)

    def add_stats(s_for_max, s_for_min):
        hi_part = s_for_max.reshape(TK // SUBLANES, SUBLANES, TQ)
        stat_ref[0] = jnp.maximum(stat_ref[0], jnp.max(hi_part, axis=0))
        stat_ref[1] = jnp.minimum(stat_ref[1], jnp.min(s_for_min.reshape(TK // SUBLANES, SUBLANES, TQ), axis=0))
        stat_ref[2] += jnp.sum(jnp.where(hi_part > 0.0, 1.0, 0.0), axis=0)
        stat_ref[3] += jnp.sum(jnp.where(hi_part >= 0.0, 1.0, 0.0), axis=0)

    def past_scores(kt):
        s = score_tile(kt)
        score_ref[kt] = s
        add_stats(s, s)

    _for_tiles(qb, past_scores)
    s_diag = score_tile(qb)
    visible = band_ref[0, 1] > 0.5 * NEG
    score_ref[qb] = jnp.where(visible, s_diag, -jnp.inf)
    add_stats(jnp.where(visible, s_diag, -jnp.inf), jnp.where(visible, s_diag, jnp.inf))

    k_sel = float(DSA_TOPK_MAX)

    def count(pred):
        def body(kt, c):
            xs = score_ref[kt].reshape(TK // SUBLANES, SUBLANES, TQ)
            accs = [c[a] for a in range(N_ACC)]
            for g in range(TK // SUBLANES):
                a = accs[g % N_ACC]
                accs[g % N_ACC] = jnp.where(pred(xs[g]), a + 1.0, a)
            return jnp.stack(accs)
        c = lax.fori_loop(0, n_tiles, body, jnp.zeros((N_ACC, SUBLANES, TQ), F32))
        return jnp.sum(jnp.sum(c, axis=0), axis=0, keepdims=True)

    col = lax.broadcasted_iota(jnp.int32, (1, TQ), 1)
    limit = qb * TQ + (col // CHUNK + 1) * CHUNK
    select_all = limit <= DSA_TOPK_MAX
    s_max = jnp.max(stat_ref[0], axis=0, keepdims=True)
    s_min = jnp.min(stat_ref[1], axis=0, keepdims=True)
    n_pos = jnp.sum(stat_ref[2], axis=0, keepdims=True)
    n_nonneg = jnp.sum(stat_ref[3], axis=0, keepdims=True)

    positive = n_pos > k_sel
    zero_thr = (n_pos <= k_sel) & (k_sel <= n_nonneg)
    lo0 = jnp.where(positive, 0.0, _key_to_f32(_f32_to_key(s_min) - 1))
    hi0 = jnp.where(positive, _key_to_f32(_f32_to_key(s_max) + 1), 0.0)
    flo0 = jnp.where(positive, n_pos, limit.astype(F32)) - k_sel
    fhi0 = jnp.where(positive, 0.0, n_nonneg) - k_sel
    done0 = jnp.where(select_all | zero_thr, 1.0, 0.0)
    thr0 = jnp.where(select_all, -jnp.inf, 0.0)

    def search_step(vec):
        lo, hi, flo, fhi, side, done, thr = vec
        w = flo / (flo - fhi)
        t = lo * (1.0 - w) + hi * w
        t = jnp.where((t > lo) & (t < hi), t, 0.5 * lo + 0.5 * hi)
        splits = (t > lo) & (t < hi)
        f = count(lambda s: s >= t) - k_sel
        active = done == 0.0
        hit = active & splits & (f == 0.0)
        adjacent = active & ~splits
        thr = jnp.where(hit, t, jnp.where(adjacent, lo, thr))
        done = jnp.where(hit | adjacent, 1.0, done)
        up = active & splits & (f > 0.0)
        down = active & splits & (f < 0.0)
        fhi_new = jnp.where(up & (side > 0.0), 0.5 * fhi, jnp.where(down, f, fhi))
        flo_new = jnp.where(down & (side < 0.0), 0.5 * flo, jnp.where(up, f, flo))
        lo = jnp.where(up, t, lo)
        hi = jnp.where(down, t, hi)
        side = jnp.where(up, 1.0, jnp.where(down, -1.0, side))
        return lo, hi, flo_new, fhi_new, side, done, thr

    def open_queries(done):
        return (jnp.min(done) == 0.0).astype(jnp.int32)

    def search_round(state):
        rounds, _, vec = state
        for _ in range(SEARCH_STEPS_PER_CHECK):
            vec = search_step(vec)
        return rounds + 1, open_queries(vec[5]), vec

    vec0 = (lo0, hi0, flo0, fhi0, jnp.zeros((1, TQ), F32), done0, thr0)
    state = lax.while_loop(lambda st: (st[0] < SEARCH_ROUNDS) & (st[1] > 0), search_round,
                           (jnp.int32(0), open_queries(done0), vec0))
    done, thr = state[2][5], state[2][6]

    def bisect(i, prefix):
        cand = prefix | lax.shift_left(jnp.int32(1), 31 - i)
        cand_f = _key_to_f32(cand)
        take = count(lambda s: s >= cand_f) >= k_sel
        return jnp.where(take, cand, prefix)

    def exact_thr():
        prefix = lax.fori_loop(0, 32, bisect, jnp.zeros((1, TQ), jnp.int32))
        return jnp.where(done == 0.0, _key_to_f32(prefix), thr)

    thr = lax.cond(state[1] > 0, exact_thr, lambda: thr)
    n_gt = count(lambda s: s > thr)
    need = jnp.where(select_all, 0.0, k_sel - n_gt)

    def tie_counts(kt, carry):
        tied = jnp.where(score_ref[kt] == thr, 1.0, 0.0).reshape(TK // SUBLANES, SUBLANES, TQ)
        tiecnt_ref[kt] = jnp.sum(jnp.sum(tied, axis=0), axis=0, keepdims=True)
        return carry

    lax.fori_loop(0, n_tiles, tie_counts, 0)
    tie_ref[...] = jnp.zeros(tie_ref.shape, F32)
    tril = tril_ref[...]

    def selection_mask(kt):
        s = score_ref[kt]
        tied = s == thr
        tied_before = tie_ref[...]
        tie_ref[...] = tied_before + tiecnt_ref[kt]
        tied_upto = tied_before + _dot(tril, jnp.where(tied, 1.0, 0.0).astype(BF16))
        return jnp.where((s > thr) | (tied & (tied_upto <= need)), 0.0, NEG)

    _load_queries(q_ref, q_scr, _DSA_STREAMS)
    _softmax_passes(qb, k_ref, vt_ref, q_scr, band_ref, selection_mask, logit_ref, m_ref, acc_ref, _DSA_STREAMS)

    for p in range(N_PAIRS):
        halves = [_normalised(acc_ref, PAIR * p + j) if PAIR * p + j < DSA_HEADS
                  else jnp.zeros((HEAD_DIM, TQ), F32) for j in range(PAIR)]
        o_ref[0, :, pl.ds(p * LANES, LANES)] = jnp.concatenate(halves, axis=0).T.astype(o_ref.dtype)


def _dsa(aq, ak, av, iq, ik, iw, bias_t, batch, seq):
    bidx = jnp.asarray(_band_bucket_index_t())
    tril = jnp.asarray(_prefix_tril(), BF16)
    n_streams = len(_DSA_STREAMS)
    r3 = lambda a: a.reshape(batch, seq, a.shape[-1])
    qblk = lambda w: pl.BlockSpec((1, TQ, w), lambda b, n: (b, n, 0))
    kblk = lambda w: pl.BlockSpec((1, seq, w), lambda b, n: (b, 0, 0))
    const = lambda a: pl.BlockSpec(a.shape, lambda b, n: (0,) * a.ndim)
    out = pl.pallas_call(
        _dsa_kernel,
        grid=(batch, seq // TQ),
        in_specs=[pl.BlockSpec(memory_space=pltpu.SMEM),
                  qblk(D_HEADS_PAD), kblk(D_HEADS_PAD), kblk(D_HEADS_PAD),
                  qblk(IDX_HEADS * IDX_DIM), kblk(LANES), qblk(LANES), const(bidx), const(tril)],
        out_specs=qblk(D_HEADS_PAD),
        out_shape=jax.ShapeDtypeStruct((batch, seq, D_HEADS_PAD), BF16),
        scratch_shapes=[pltpu.VMEM((DSA_HEADS, 2, TK, TQ), F32),
                        pltpu.VMEM((seq // TK, TK, TQ), F32),
                        pltpu.VMEM((n_streams, LANES, TQ), BF16),
                        pltpu.VMEM((seq // TK, n_streams, TK, TQ), F32),
                        pltpu.VMEM((n_streams, SUBLANES, TQ), F32),
                        pltpu.VMEM((n_streams, VT_ROWS, TQ), F32),
                        pltpu.VMEM((seq // TK, 1, TQ), F32),
                        pltpu.VMEM((1, TQ), F32),
                        pltpu.VMEM((4, SUBLANES, TQ), F32),
                        pltpu.VMEM((seq // TK, DSA_HEADS * VT_ROWS, TK), BF16)],
        compiler_params=_cparams(2),
        name="dsa_attention",
    )(bias_t, r3(aq), r3(ak), r3(av), r3(iq), r3(ik), r3(iw), bidx, tril)
    return out.reshape(batch * seq, D_HEADS_PAD)


_DIFF_STREAMS = tuple(
    (h // PAIR, h, (h % PAIR) * HEAD_DIM + mm * DIFF_QK_DIM, (h % PAIR) * HEAD_DIM + (mm + 1) * DIFF_QK_DIM)
    for h in range(DIFF_HEADS) for mm in range(2))


def _diff_kernel(lambda_init, tab_ref, q_ref, k_ref, v_ref, lam_ref, g_ref, bidx_ref, avg_ref,
                 o_ref, band_ref, q_scr, logit_ref, m_ref, acc_ref, vt_ref):
    qb = pl.program_id(1)

    @pl.when((pl.program_id(0) == 0) & (qb == 0))
    def _():
        _build_band(bidx_ref, tab_ref, band_ref, DIFF_HEADS)

    @pl.when(qb == 0)
    def _():
        _load_values(v_ref, vt_ref, DIFF_HEADS, k_ref.shape[1] // TK)

    lv = lam_ref[...]
    lam = (jnp.exp(jnp.sum(lv[0:1] * lv[1:2], axis=-1, keepdims=True))
           - jnp.exp(jnp.sum(lv[2:3] * lv[3:4], axis=-1, keepdims=True)) + lambda_init)

    _load_queries(q_ref, q_scr, _DIFF_STREAMS)
    _softmax_passes(qb, k_ref, vt_ref, q_scr, band_ref, lambda kt: None, logit_ref, m_ref, acc_ref,
                    _DIFF_STREAMS)

    avg = avg_ref[...]
    for p in range(N_PAIRS):
        halves = []
        for j in range(PAIR):
            h = PAIR * p + j
            if h < DIFF_HEADS:
                halves.append(_normalised(acc_ref, 2 * h) - lam * _normalised(acc_ref, 2 * h + 1))
            else:
                halves.append(jnp.zeros((HEAD_DIM, TQ), F32))
        out = jnp.concatenate(halves, axis=0).T
        ms = _group_mean(out * out, avg)
        out = out * lax.rsqrt(ms + LN_EPS) * g_ref[...] * (1.0 - lambda_init)
        o_ref[0, :, pl.ds(p * LANES, LANES)] = out.astype(o_ref.dtype)


def _diff(dq, dk, dv, lam_vecs, subln_g, bias_t, lambda_init, batch, seq):
    bidx = jnp.asarray(_band_bucket_index_t())
    avg = jnp.asarray(_group_avg(), BF16)
    g2 = jnp.concatenate([subln_g, subln_g]).reshape(1, LANES)
    n_streams = len(_DIFF_STREAMS)
    r3 = lambda a: a.reshape(batch, seq, a.shape[-1])
    qblk = pl.BlockSpec((1, TQ, D_HEADS_PAD), lambda b, n: (b, n, 0))
    kblk = pl.BlockSpec((1, seq, D_HEADS_PAD), lambda b, n: (b, 0, 0))
    const = lambda a: pl.BlockSpec(a.shape, lambda b, n: (0,) * a.ndim)
    out = pl.pallas_call(
        functools.partial(_diff_kernel, lambda_init),
        grid=(batch, seq // TQ),
        in_specs=[pl.BlockSpec(memory_space=pltpu.SMEM), qblk, kblk, kblk,
                  const(lam_vecs), const(g2), const(bidx), const(avg)],
        out_specs=qblk,
        out_shape=jax.ShapeDtypeStruct((batch, seq, D_HEADS_PAD), BF16),
        scratch_shapes=[pltpu.VMEM((DIFF_HEADS, 2, TK, TQ), F32),
                        pltpu.VMEM((n_streams, LANES, TQ), BF16),
                        pltpu.VMEM((seq // TK, n_streams, TK, TQ), F32),
                        pltpu.VMEM((n_streams, SUBLANES, TQ), F32),
                        pltpu.VMEM((n_streams, VT_ROWS, TQ), F32),
                        pltpu.VMEM((seq // TK, DIFF_HEADS * VT_ROWS, TK), BF16)],
        compiler_params=_cparams(2),
        name="diff_attention",
    )(bias_t, r3(dq), r3(dk), r3(dv), lam_vecs, g2, bidx, avg)
    return out.reshape(batch * seq, D_HEADS_PAD)


def _out_ffn_kernel(x_ref, ret_ref, dsa_ref, dif_ref, wo_ref, g1_ref, b1_ref, wg_ref, wu_ref, wd_ref,
                    g2_ref, b2_ref, o_ref):
    mixed = _dot(ret_ref[...], wo_ref[pl.ds(0, D_HEADS_PAD), :])
    mixed = mixed + _dot(dsa_ref[...], wo_ref[pl.ds(D_HEADS_PAD, D_HEADS_PAD), :])
    mixed = mixed + _dot(dif_ref[...], wo_ref[pl.ds(2 * D_HEADS_PAD, D_HEADS_PAD), :])
    x = _layer_norm(DEEPNORM_ALPHA * x_ref[...] + mixed, g1_ref[...], b1_ref[...])
    o_ref[...] = _ffn_ln_body(x, wg_ref, wu_ref, wd_ref, g2_ref[...], b2_ref[...])


def _pad_w_out(w_out):
    zeros = jnp.zeros((D_HEADS_PAD - D_DSA, w_out.shape[1]), w_out.dtype)
    return jnp.concatenate([w_out[:D_RET], w_out[D_RET:D_RET + D_DSA], zeros,
                            w_out[D_RET + D_DSA:], zeros], axis=0)


def _out_ffn(x, ret, dsa, dif, w_pad, g1, b1, wg, wu, wd, g2, b2, layer):
    t = x.shape[0]
    stacked = lambda shape: pl.BlockSpec((None,) + shape, lambda i: (layer, 0, 0), pipeline_mode=pl.Buffered(1))
    const = lambda shape: pl.BlockSpec(shape, lambda i: (0, 0), pipeline_mode=pl.Buffered(1))
    tok = lambda w: pl.BlockSpec((TM_FFN, w), lambda i: (i, 0))
    return pl.pallas_call(
        _out_ffn_kernel,
        grid=(t // TM_FFN,),
        in_specs=[tok(D_MODEL), tok(D_HEADS_PAD), tok(D_HEADS_PAD), tok(D_HEADS_PAD),
                  const((3 * D_HEADS_PAD, D_MODEL)), const((1, D_MODEL)), const((1, D_MODEL)),
                  stacked((D_MODEL, D_FF)), stacked((D_MODEL, D_FF)), stacked((D_FF, D_MODEL)),
                  const((1, D_MODEL)), const((1, D_MODEL))],
        out_specs=tok(D_MODEL),
        out_shape=jax.ShapeDtypeStruct((t, D_MODEL), F32),
        compiler_params=_cparams(1),
        name="out_proj_ffn_ln",
    )(x, ret, dsa, dif, w_pad, g1, b1, wg, wu, wd, g2, b2)


def kernel(x, w_in, w_out, ffn1_wg, ffn1_wu, ffn1_wd, ffn2_wg, ffn2_wu, ffn2_wd,
           ln_g, ln_b, diff_lambda, diff_subln_g, rel_bias):
    batch, seq, d = x.shape
    assert d == D_MODEL and seq % TQ == 0 and seq % (RET_BLOCKS * TR) == 0 and (batch * seq) % TM_FFN == 0
    assert min(DSA_TOPK_MAX, seq // 4) == DSA_TOPK_MAX
    h = x.reshape(batch * seq, d)
    bias_t = rel_bias.T
    row = lambda a: a.reshape(1, D_MODEL)
    for l in range(DEPTH):
        lambda_init = 0.8 - 0.6 * math.exp(-0.3 * l)
        h = _ffn_ln(h, ffn1_wg, ffn1_wu, ffn1_wd, row(ln_g[l, 0]), row(ln_b[l, 0]), l)
        (rq, rk, rv, rg, aq, ak, av, iq, ik, iw, dq, dk, dv) = _proj(h, w_in, l)
        ret = _retention(rq, rk, rv, rg, batch, seq)
        dsa = _dsa(aq, ak, av, iq, ik, iw, bias_t[:DSA_HEADS], batch, seq)
        dif = _diff(dq, dk, dv, diff_lambda[l], diff_subln_g[l], bias_t[DSA_HEADS:], lambda_init, batch, seq)
        h = _out_ffn(h, ret, dsa, dif, _pad_w_out(w_out[l]).astype(BF16), row(ln_g[l, 1]), row(ln_b[l, 1]),
                     ffn2_wg, ffn2_wu, ffn2_wd, row(ln_g[l, 2]), row(ln_b[l, 2]), l)
    return h.reshape(batch, seq, d)
```

```python
import functools
import math

import numpy as np
import jax
import jax.numpy as jnp
from jax import lax
from jax.experimental import pallas as pl
from jax.experimental.pallas import tpu as pltpu

D_MODEL = 1024
DEPTH = 2
CHUNK = 64
HEAD_DIM = 64
RET_HEADS = 6
DSA_HEADS = 5
IDX_HEADS = 4
IDX_DIM = 64
DSA_TOPK_MAX = 256
DIFF_HEADS = 5
DIFF_QK_DIM = HEAD_DIM // 2
D_RET = RET_HEADS * HEAD_DIM
D_DSA = DSA_HEADS * HEAD_DIM
D_DIFF = DIFF_HEADS * HEAD_DIM
D_FF = 2816
N_BUCKETS = 32
MAX_DISTANCE = 128
ROPE_BASE = 10000.0
LN_EPS = 1e-5
HEAD_NORM_EPS = 1e-6
DEEPNORM_ALPHA = (2 * DEPTH) ** 0.25

LANES = 128
SUBLANES = 8
BF16_ROWS = 16
PAIR = LANES // HEAD_DIM
VMEM_LIMIT_BYTES = 56 * 1024 * 1024

D_HEADS_PAD = 3 * LANES
N_PAIRS = D_HEADS_PAD // LANES
VT_ROWS = HEAD_DIM + BF16_ROWS
TQ = 256
TK = 256
TR = 256
RET_BLOCKS = 4
TM_FFN = 512
TF = 256
TM_PROJ = 512
TN_PROJ = 256
N_ACC = 4
SEARCH_STEPS_PER_CHECK = 4
SEARCH_ROUNDS = 6
NEG = -1e30
LOG2E = math.log2(math.e)

F32 = jnp.float32
BF16 = jnp.bfloat16


def _dot(a, b):
    return jnp.dot(a, b, preferred_element_type=F32)


def _dot_nt(a, b):
    return lax.dot_general(a, b, (((1,), (1,)), ((), ())), preferred_element_type=F32)


def _layer_norm(r, g, b):
    mu = jnp.mean(r, axis=-1, keepdims=True)
    d = r - mu
    var = jnp.mean(d * d, axis=-1, keepdims=True)
    return d * lax.rsqrt(var + LN_EPS) * g + b


def _group_mean(x, avg_bf16):
    hi = x.astype(BF16)
    lo = (x - hi.astype(F32)).astype(BF16)
    return _dot(hi, avg_bf16) + _dot(lo, avg_bf16)


def _cparams(n_axes):
    return pltpu.CompilerParams(
        dimension_semantics=("arbitrary",) * n_axes,
        vmem_limit_bytes=VMEM_LIMIT_BYTES,
    )


def _rel_bucket_static(rel):
    rel = np.asarray(rel, dtype=np.int64)
    half = N_BUCKETS // 2
    max_exact = half // 2
    n = np.abs(rel)
    large = np.full(n.shape, max_exact, dtype=np.int64)
    for k in range(1, 64):
        large = np.where(64 * (2 ** k) <= n * n, max_exact + k, large)
    large = np.minimum(large, half - 1)
    return (np.where(rel > 0, half, 0) + np.where(n < max_exact, n, large)).astype(np.int32)


FAR_BUCKET = int(_rel_bucket_static(-(TK + 1)))


@functools.lru_cache(maxsize=None)
def _band_bucket_index_t():
    j = np.arange(TK)[:, None]
    i = np.arange(TQ)[None, :]
    prev = _rel_bucket_static(j - TK - i)
    diag = _rel_bucket_static(j - i)
    visible = j < (i // CHUNK + 1) * CHUNK
    diag = np.where(visible, diag, -1)
    return np.stack([prev, diag]).astype(np.int32)


@functools.lru_cache(maxsize=None)
def _rotary_tables(seq):
    lane = np.arange(LANES)
    d = lane % HEAD_DIM
    f = d % (HEAD_DIM // 2)
    inv = ROPE_BASE ** (-(2.0 * f) / HEAD_DIM)
    ang = np.arange(seq, dtype=np.float64)[:, None] * inv[None, :]
    cos = np.cos(ang)
    sin = np.sin(ang) * np.where(d < HEAD_DIM // 2, -1.0, 1.0)[None, :]
    return cos.astype(np.float32), sin.astype(np.float32)


@functools.lru_cache(maxsize=None)
def _retention_tables():
    gamma = 1.0 - 2.0 ** (-5.0 - np.arange(RET_HEADS, dtype=np.float64))
    t = np.arange(TR)
    dist = np.abs(t[:, None] - t[None, :]).astype(np.float64)
    visible = t[None, :] < (t[:, None] // CHUNK + 1) * CHUNK
    intra = np.stack([np.where(visible, g ** dist, 0.0) for g in gamma])
    lane_head = np.arange(D_HEADS_PAD) // HEAD_DIM
    g_lane = gamma[lane_head]
    qdec = g_lane[None, :] ** (t[:, None] + 1.0)
    kdec = g_lane[None, :] ** (TR - 1.0 - t[:, None])
    a = np.arange(LANES)
    same = (a[:, None] // HEAD_DIM) == (a[None, :] // HEAD_DIM)
    sdec = np.stack([np.where(same, (gamma[PAIR * p + a // HEAD_DIM] ** TR)[:, None], 0.0)
                     for p in range(RET_HEADS // PAIR)])
    return (intra.astype(np.float32), qdec.astype(np.float32), kdec.astype(np.float32),
            sdec.astype(np.float32), same.astype(np.float32))


@functools.lru_cache(maxsize=None)
def _group_avg():
    a = np.arange(LANES)
    same = (a[:, None] // HEAD_DIM) == (a[None, :] // HEAD_DIM)
    return np.where(same, 1.0 / HEAD_DIM, 0.0).astype(np.float32)


@functools.lru_cache(maxsize=None)
def _prefix_tril():
    a = np.arange(TK)
    return (a[None, :] <= a[:, None]).astype(np.float32)


def _lane_group(shape):
    return lax.broadcasted_iota(jnp.int32, shape, len(shape) - 1) // HEAD_DIM


def _ffn_ln_body(x, wg_ref, wu_ref, wd_ref, g, b):
    xb = x.astype(BF16)
    acc = jnp.zeros(x.shape, F32)
    for c in range(D_FF // TF):
        cols = pl.ds(c * TF, TF)
        gate = _dot(xb, wg_ref[:, cols].astype(BF16))
        up = _dot(xb, wu_ref[:, cols].astype(BF16))
        h = gate * jax.nn.sigmoid(gate) * up
        acc = acc + _dot(h.astype(BF16), wd_ref[cols, :].astype(BF16))
    return _layer_norm(DEEPNORM_ALPHA * x + 0.5 * acc, g, b)


def _ffn_ln_kernel(x_ref, wg_ref, wu_ref, wd_ref, g_ref, b_ref, o_ref):
    o_ref[...] = _ffn_ln_body(x_ref[...], wg_ref, wu_ref, wd_ref, g_ref[...], b_ref[...])


def _ffn_ln(x, wg, wu, wd, g, b, layer):
    t = x.shape[0]
    stacked = lambda shape: pl.BlockSpec((None,) + shape, lambda i: (layer, 0, 0), pipeline_mode=pl.Buffered(1))
    full = lambda shape: pl.BlockSpec(shape, lambda i: (0, 0))
    return pl.pallas_call(
        _ffn_ln_kernel,
        grid=(t // TM_FFN,),
        in_specs=[pl.BlockSpec((TM_FFN, D_MODEL), lambda i: (i, 0)),
                  stacked((D_MODEL, D_FF)), stacked((D_MODEL, D_FF)), stacked((D_FF, D_MODEL)),
                  full((1, D_MODEL)), full((1, D_MODEL))],
        out_specs=pl.BlockSpec((TM_FFN, D_MODEL), lambda i: (i, 0)),
        out_shape=jax.ShapeDtypeStruct((t, D_MODEL), F32),
        compiler_params=_cparams(1),
        name="ffn_ln",
    )(x, wg, wu, wd, g, b)


IN_SPLITS = (D_RET, D_RET, D_RET, D_RET, D_DSA, D_DSA, D_DSA,
             IDX_HEADS * IDX_DIM, IDX_DIM, IDX_HEADS, D_DIFF, D_DIFF, D_DIFF)
N_IN = sum(IN_SPLITS)
N_IN_PAD = -(-N_IN // LANES) * LANES
_PROJ_OUT = (("rq", D_HEADS_PAD, F32, None, "tok"), ("rk", D_HEADS_PAD, F32, None, "tok"),
             ("rv", D_HEADS_PAD, BF16, None, "tok"), ("rg", D_HEADS_PAD, F32, None, "tok"),
             ("aq", D_HEADS_PAD, BF16, HEAD_DIM ** -0.5 * LOG2E, "T"), ("ak", D_HEADS_PAD, BF16, None, "tok"),
             ("av", DSA_HEADS, BF16, None, "VT"),
             ("iq", IDX_HEADS * IDX_DIM, BF16, None, "T"), ("ik", LANES, BF16, None, "tok"),
             ("iw", LANES, F32, None, "T"),
             ("dq", D_HEADS_PAD, BF16, DIFF_QK_DIM ** -0.5 * LOG2E, "T"), ("dk", D_HEADS_PAD, BF16, None, "tok"),
             ("dv", DIFF_HEADS, BF16, None, "VT"))
_IW_ROWS = SUBLANES


def _proj_kernel(x_ref, w_ref, *refs):
    o_refs, h_scr = refs[:-1], refs[-1]
    xb = x_ref[...].astype(BF16)
    for lo in range(0, N_IN, TN_PROJ):
        cols = pl.ds(lo, min(TN_PROJ, N_IN - lo))
        h_scr[:, cols] = _dot(xb, w_ref[:, cols].astype(BF16))
    ones = jnp.ones((BF16_ROWS, TK), BF16)
    off = 0
    for width, (name, out_width, dtype, scale, kind), o_ref in zip(IN_SPLITS, _PROJ_OUT, o_refs):
        base = off // LANES * LANES
        end = min(-(-(off + width) // LANES) * LANES, N_IN_PAD)
        seg = h_scr[:, pl.ds(base, end - base)][:, off - base:off - base + width]
        off += width
        if scale is not None:
            seg = seg * scale
        padded = -(-width // LANES) * LANES
        if name == "ik":
            seg = jnp.concatenate([seg, seg], axis=1)
        elif padded > width:
            seg = jnp.concatenate([seg, jnp.zeros((TM_PROJ, padded - width), F32)], axis=1)
        if kind == "tok":
            o_ref[...] = seg.astype(dtype)
        elif kind == "T":
            o_ref[...] = seg.T[:o_ref.shape[0]].astype(dtype)
        else:
            seg_t = seg.T.astype(dtype)
            for t in range(TM_PROJ // TK):
                for h in range(out_width):
                    o_ref[t, pl.ds(h * VT_ROWS, HEAD_DIM), :] = seg_t[h * HEAD_DIM:(h + 1) * HEAD_DIM,
                                                                      t * TK:(t + 1) * TK]
                    o_ref[t, pl.ds(h * VT_ROWS + HEAD_DIM, BF16_ROWS), :] = ones


def _proj(x, w_in, layer):
    t = x.shape[0]
    out_specs, out_shape = [], []
    for name, w, dt, _, kind in _PROJ_OUT:
        if kind == "tok":
            out_specs.append(pl.BlockSpec((TM_PROJ, w), lambda i: (i, 0)))
            out_shape.append(jax.ShapeDtypeStruct((t, w), dt))
        elif kind == "T":
            rows = _IW_ROWS if name == "iw" else w
            out_specs.append(pl.BlockSpec((rows, TM_PROJ), lambda i: (0, i)))
            out_shape.append(jax.ShapeDtypeStruct((rows, t), dt))
        else:
            out_specs.append(pl.BlockSpec((TM_PROJ // TK, w * VT_ROWS, TK), lambda i: (i, 0, 0)))
            out_shape.append(jax.ShapeDtypeStruct((t // TK, w * VT_ROWS, TK), dt))
    return pl.pallas_call(
        _proj_kernel,
        grid=(t // TM_PROJ,),
        in_specs=[pl.BlockSpec((TM_PROJ, D_MODEL), lambda i: (i, 0)),
                  pl.BlockSpec((None, D_MODEL, N_IN), lambda i: (layer, 0, 0), pipeline_mode=pl.Buffered(1))],
        out_specs=out_specs,
        out_shape=out_shape,
        scratch_shapes=[pltpu.VMEM((TM_PROJ, N_IN_PAD), F32)],
        compiler_params=_cparams(1),
        name="in_proj",
    )(x, w_in)


def _swap_halves(x):
    lane = lax.broadcasted_iota(jnp.int32, x.shape, 1)
    first_half = (lane % HEAD_DIM) < (HEAD_DIM // 2)
    from_right = pltpu.roll(x, LANES - HEAD_DIM // 2, 1)
    from_left = pltpu.roll(x, HEAD_DIM // 2, 1)
    return jnp.where(first_half, from_right, from_left)


def _retention_kernel(q_ref, k_ref, v_ref, g_ref, cos_ref, sin_ref, intra_ref, qdec_ref, kdec_ref,
                      sdec_ref, same_ref, avg_ref, o_ref, state_ref):
    @pl.when(pl.program_id(1) == 0)
    def _():
        state_ref[...] = jnp.zeros(state_ref.shape, F32)

    avg = avg_ref[...]
    group = _lane_group((TR, LANES))
    for p in range(RET_HEADS // PAIR):
        lanes = pl.ds(p * LANES, LANES)
        for blk in range(RET_BLOCKS):
            rows = pl.ds(blk * TR, TR)
            cos = cos_ref[rows, :]
            sin = sin_ref[rows, :]
            q = q_ref[0, rows, lanes]
            k = k_ref[0, rows, lanes]
            q = (q * cos + _swap_halves(q) * sin) * (HEAD_DIM ** -0.5)
            k = k * cos + _swap_halves(k) * sin
            v = v_ref[0, rows, lanes]
            kb = k.astype(BF16)
            intra = jnp.zeros((TR, LANES), F32)
            for j in range(PAIR):
                qh = jnp.where(group == j, q, 0.0).astype(BF16)
                scores = _dot_nt(qh, kb) * intra_ref[PAIR * p + j]
                oh = _dot(scores.astype(BF16), v)
                intra = jnp.where(group == j, oh, intra)
            state = state_ref[p]
            cross = _dot((q * qdec_ref[:, lanes]).astype(BF16), state.astype(BF16))
            o = intra + cross
            kd_t = (k * kdec_ref[:, lanes]).T.astype(BF16)
            state_ref[p] = state * sdec_ref[p] + _dot(kd_t, v) * same_ref[...]
            mu = _group_mean(o, avg)
            d = o - mu
            var = _group_mean(d * d, avg)
            normed = d * lax.rsqrt(var + HEAD_NORM_EPS)
            gate = g_ref[0, rows, lanes]
            o_ref[0, rows, lanes] = (gate * jax.nn.sigmoid(gate) * normed).astype(o_ref.dtype)


def _retention(rq, rk, rv, rg, batch, seq):
    cos, sin = _rotary_tables(seq)
    intra, qdec, kdec, sdec, same = _retention_tables()
    avg = jnp.asarray(_group_avg(), BF16)
    step = RET_BLOCKS * TR
    r3 = lambda a: a.reshape(batch, seq, D_HEADS_PAD)
    tok = pl.BlockSpec((1, step, D_HEADS_PAD), lambda b, n: (b, n, 0))
    const = lambda a: pl.BlockSpec(a.shape, lambda b, n: (0,) * a.ndim)
    pos = pl.BlockSpec((step, LANES), lambda b, n: (n, 0))
    out = pl.pallas_call(
        _retention_kernel,
        grid=(batch, seq // step),
        in_specs=[tok, tok, tok, tok, pos, pos, const(intra), const(qdec), const(kdec),
                  const(sdec), const(same), const(avg)],
        out_specs=tok,
        out_shape=jax.ShapeDtypeStruct((batch, seq, D_HEADS_PAD), BF16),
        scratch_shapes=[pltpu.VMEM((RET_HEADS // PAIR, LANES, LANES), F32)],
        compiler_params=_cparams(2),
        name="retention",
    )(r3(rq), r3(rk), r3(rv), r3(rg), cos, sin, intra, qdec, kdec, sdec, same, avg)
    return out.reshape(batch * seq, D_HEADS_PAD)


def _build_band(bidx_ref, tab_ref, band_ref, n_heads):
    for h in range(n_heads):
        far = tab_ref[h, FAR_BUCKET]
        for t in range(2):
            idx = bidx_ref[t]
            val = jnp.full((TK, TQ), NEG, F32)
            for bkt in range(N_BUCKETS):
                val = jnp.where(idx == bkt, (tab_ref[h, bkt] - far) * LOG2E, val)
            band_ref[h, t] = val


def _load_queries(qt_ref, q_scr, streams):
    row = lax.broadcasted_iota(jnp.int32, (LANES, TQ), 0)
    for i, (pair, _, lo, hi) in enumerate(streams):
        q_scr[i] = jnp.where((row >= lo) & (row < hi), qt_ref[pl.ds(pair * LANES, LANES), :], 0)


def _for_tiles(n, fn):
    def quad(j, carry):
        for u in range(4):
            fn(4 * j + u)
        return carry

    lax.fori_loop(0, lax.shift_right_logical(n, 2), quad, 0)
    rest = lax.rem(n, 4)

    @pl.when(rest >= 2)
    def _():
        fn(n - rest)
        fn(n - rest + 1)

    @pl.when(lax.rem(rest, 2) == 1)
    def _():
        fn(n - 1)


def _softmax_passes(qb, k_ref, vt_ref, q_scr, band_ref, mask_fn, logit_ref, m_ref, acc_ref, streams):
    m_ref[...] = jnp.full(m_ref.shape, NEG, F32)

    def logits_tile(kt, band_t):
        rows = pl.ds(pl.multiple_of(kt * TK, TK), TK)
        mask = mask_fn(kt)
        for i, (pair, head, _, _) in enumerate(streams):
            lg = _dot(k_ref[0, rows, pl.ds(pair * LANES, LANES)], q_scr[i])
            if band_t is not None:
                lg = lg + band_ref[head, band_t]
            if mask is not None:
                lg = lg + mask
            logit_ref[kt, i] = lg
            m_ref[i] = jnp.maximum(m_ref[i], jnp.max(lg.reshape(TK // SUBLANES, SUBLANES, TQ), axis=0))

    _for_tiles(jnp.maximum(qb - 1, 0), lambda kt: logits_tile(kt, None))

    @pl.when(qb >= 1)
    def _():
        logits_tile(qb - 1, 0)
        logits_tile(qb, 1)

    @pl.when(qb == 0)
    def _():
        logits_tile(qb, 1)

    m_fin = [jnp.max(m_ref[i], axis=0, keepdims=True) for i in range(len(streams))]
    acc_ref[...] = jnp.zeros(acc_ref.shape, F32)

    def values_tile(kt):
        for i, (_, head, _, _) in enumerate(streams):
            p = jnp.exp2(logit_ref[kt, i] - m_fin[i])
            acc_ref[i] += _dot(vt_ref[0, kt, pl.ds(head * VT_ROWS, VT_ROWS), :], p.astype(BF16))

    _for_tiles(qb + 1, values_tile)


def _normalised(acc_ref, i):
    a = acc_ref[i]
    return a[:HEAD_DIM] / a[HEAD_DIM:HEAD_DIM + 1]


_DSA_STREAMS = tuple((h // PAIR, h, (h % PAIR) * HEAD_DIM, (h % PAIR + 1) * HEAD_DIM) for h in range(DSA_HEADS))


def _f32_to_key(x):
    bits = lax.bitcast_convert_type(x, jnp.int32)
    return jnp.where(bits < 0, ~bits, bits ^ jnp.int32(-2 ** 31))


def _key_to_f32(key):
    bits = jnp.where(key < 0, key ^ jnp.int32(-2 ** 31), ~key)
    return lax.bitcast_convert_type(bits, F32)


def _dsa_kernel(tab_ref, qt_ref, k_ref, vt_ref, iqt_ref, ik_ref, iwt_ref, bidx_ref, tril_ref,
                o_ref, band_ref, score_ref, q_scr, logit_ref, m_ref, acc_ref, tiecnt_ref, tie_ref, stat_ref):
    qb = pl.program_id(1)
    n_tiles = qb + 1

    @pl.when((pl.program_id(0) == 0) & (qb == 0))
    def _():
        _build_band(bidx_ref, tab_ref, band_ref, DSA_HEADS)

    iw = iwt_ref[...] * (IDX_HEADS ** -0.5 * IDX_DIM ** -0.5)
    row = lax.broadcasted_iota(jnp.int32, (LANES, TQ), 0)
    iq = [jnp.where(row // HEAD_DIM == (h % PAIR), iqt_ref[pl.ds((h // PAIR) * LANES, LANES), :], 0)
          for h in range(IDX_HEADS)]

    def score_tile(kt):
        ki = ik_ref[0, pl.ds(pl.multiple_of(kt * TK, TK), TK), :]
        s = jnp.zeros((TK, TQ), F32)
        for h in range(IDX_HEADS):
            s = s + jnp.maximum(_dot(ki, iq[h]), 0.0) * iw[h:h + 1, :]
        return s

    stat_ref[0] = jnp.full((SUBLANES, TQ), -jnp.inf, F32)
    stat_ref[1] = jnp.full((SUBLANES, TQ), jnp.inf, F32)
    stat_ref[2] = jnp.zeros((SUBLANES, TQ), F32)
    stat_ref[3] = jnp.zeros((SUBLANES, TQ), F32)

    def add_stats(s_for_max, s_for_min):
        hi_part = s_for_max.reshape(TK // SUBLANES, SUBLANES, TQ)
        stat_ref[0] = jnp.maximum(stat_ref[0], jnp.max(hi_part, axis=0))
        stat_ref[1] = jnp.minimum(stat_ref[1], jnp.min(s_for_min.reshape(TK // SUBLANES, SUBLANES, TQ), axis=0))
        stat_ref[2] += jnp.sum(jnp.where(hi_part > 0.0, 1.0, 0.0), axis=0)
        stat_ref[3] += jnp.sum(jnp.where(hi_part >= 0.0, 1.0, 0.0), axis=0)

    def past_scores(kt):
        s = score_tile(kt)
        score_ref[kt] = s
        add_stats(s, s)

    _for_tiles(qb, past_scores)
    s_diag = score_tile(qb)
    visible = band_ref[0, 1] > 0.5 * NEG
    score_ref[qb] = jnp.where(visible, s_diag, -jnp.inf)
    add_stats(jnp.where(visible, s_diag, -jnp.inf), jnp.where(visible, s_diag, jnp.inf))

    k_sel = float(DSA_TOPK_MAX)

    def count(pred):
        def body(kt, c):
            xs = score_ref[kt].reshape(TK // SUBLANES, SUBLANES, TQ)
            accs = [c[a] for a in range(N_ACC)]
            for g in range(TK // SUBLANES):
                a = accs[g % N_ACC]
                accs[g % N_ACC] = jnp.where(pred(xs[g]), a + 1.0, a)
            return jnp.stack(accs)
        c = lax.fori_loop(0, n_tiles, body, jnp.zeros((N_ACC, SUBLANES, TQ), F32))
        return jnp.sum(jnp.sum(c, axis=0), axis=0, keepdims=True)

    col = lax.broadcasted_iota(jnp.int32, (1, TQ), 1)
    limit = qb * TQ + (col // CHUNK + 1) * CHUNK
    select_all = limit <= DSA_TOPK_MAX
    s_max = jnp.max(stat_ref[0], axis=0, keepdims=True)
    s_min = jnp.min(stat_ref[1], axis=0, keepdims=True)
    n_pos = jnp.sum(stat_ref[2], axis=0, keepdims=True)
    n_nonneg = jnp.sum(stat_ref[3], axis=0, keepdims=True)

    positive = n_pos > k_sel
    zero_thr = (n_pos <= k_sel) & (k_sel <= n_nonneg)
    lo0 = jnp.where(positive, 0.0, _key_to_f32(_f32_to_key(s_min) - 1))
    hi0 = jnp.where(positive, _key_to_f32(_f32_to_key(s_max) + 1), 0.0)
    flo0 = jnp.where(positive, n_pos, limit.astype(F32)) - k_sel
    fhi0 = jnp.where(positive, 0.0, n_nonneg) - k_sel
    done0 = jnp.where(select_all | zero_thr, 1.0, 0.0)
    thr0 = jnp.where(select_all, -jnp.inf, 0.0)

    def search_step(vec):
        lo, hi, flo, fhi, side, done, thr = vec
        w = flo / (flo - fhi)
        t = lo * (1.0 - w) + hi * w
        t = jnp.where((t > lo) & (t < hi), t, 0.5 * lo + 0.5 * hi)
        splits = (t > lo) & (t < hi)
        f = count(lambda s: s >= t) - k_sel
        active = done == 0.0
        hit = active & splits & (f == 0.0)
        adjacent = active & ~splits
        thr = jnp.where(hit, t, jnp.where(adjacent, lo, thr))
        done = jnp.where(hit | adjacent, 1.0, done)
        up = active & splits & (f > 0.0)
        down = active & splits & (f < 0.0)
        fhi_new = jnp.where(up & (side > 0.0), 0.5 * fhi, jnp.where(down, f, fhi))
        flo_new = jnp.where(down & (side < 0.0), 0.5 * flo, jnp.where(up, f, flo))
        lo = jnp.where(up, t, lo)
        hi = jnp.where(down, t, hi)
        side = jnp.where(up, 1.0, jnp.where(down, -1.0, side))
        return lo, hi, flo_new, fhi_new, side, done, thr

    def open_queries(done):
        return (jnp.min(done) == 0.0).astype(jnp.int32)

    def search_round(state):
        rounds, _, vec = state
        for _ in range(SEARCH_STEPS_PER_CHECK):
            vec = search_step(vec)
        return rounds + 1, open_queries(vec[5]), vec

    vec0 = (lo0, hi0, flo0, fhi0, jnp.zeros((1, TQ), F32), done0, thr0)
    state = lax.while_loop(lambda st: (st[0] < SEARCH_ROUNDS) & (st[1] > 0), search_round,
                           (jnp.int32(0), open_queries(done0), vec0))
    done, thr = state[2][5], state[2][6]

    def bisect(i, prefix):
        cand = prefix | lax.shift_left(jnp.int32(1), 31 - i)
        cand_f = _key_to_f32(cand)
        take = count(lambda s: s >= cand_f) >= k_sel
        return jnp.where(take, cand, prefix)

    def exact_thr():
        prefix = lax.fori_loop(0, 32, bisect, jnp.zeros((1, TQ), jnp.int32))
        return jnp.where(done == 0.0, _key_to_f32(prefix), thr)

    thr = lax.cond(state[1] > 0, exact_thr, lambda: thr)
    n_gt = count(lambda s: s > thr)
    need = jnp.where(select_all, 0.0, k_sel - n_gt)

    def tie_counts(kt, carry):
        tied = jnp.where(score_ref[kt] == thr, 1.0, 0.0).reshape(TK // SUBLANES, SUBLANES, TQ)
        tiecnt_ref[kt] = jnp.sum(jnp.sum(tied, axis=0), axis=0, keepdims=True)
        return carry

    lax.fori_loop(0, n_tiles, tie_counts, 0)
    tie_ref[...] = jnp.zeros(tie_ref.shape, F32)
    tril = tril_ref[...]

    def selection_mask(kt):
        s = score_ref[kt]
        tied = s == thr
        tied_before = tie_ref[...]
        tie_ref[...] = tied_before + tiecnt_ref[kt]
        tied_upto = tied_before + _dot(tril, jnp.where(tied, 1.0, 0.0).astype(BF16))
        return jnp.where((s > thr) | (tied & (tied_upto <= need)), 0.0, NEG)

    _load_queries(qt_ref, q_scr, _DSA_STREAMS)
    _softmax_passes(qb, k_ref, vt_ref, q_scr, band_ref, selection_mask, logit_ref, m_ref, acc_ref, _DSA_STREAMS)

    for p in range(N_PAIRS):
        halves = [_normalised(acc_ref, PAIR * p + j) if PAIR * p + j < DSA_HEADS
                  else jnp.zeros((HEAD_DIM, TQ), F32) for j in range(PAIR)]
        o_ref[0, :, pl.ds(p * LANES, LANES)] = jnp.concatenate(halves, axis=0).T.astype(o_ref.dtype)


def _dsa(aqT, ak, avT, iqT, ik, iwT, bias_t, batch, seq):
    bidx = jnp.asarray(_band_bucket_index_t())
    tril = jnp.asarray(_prefix_tril(), BF16)
    nq = seq // TQ
    n_streams = len(_DSA_STREAMS)
    qcols = lambda w: pl.BlockSpec((w, TQ), lambda b, n: (0, b * nq + n))
    ktok = lambda w: pl.BlockSpec((1, seq, w), lambda b, n: (b, 0, 0))
    const = lambda a: pl.BlockSpec(a.shape, lambda b, n: (0,) * a.ndim)
    out = pl.pallas_call(
        _dsa_kernel,
        grid=(batch, nq),
        in_specs=[pl.BlockSpec(memory_space=pltpu.SMEM),
                  qcols(D_HEADS_PAD), ktok(D_HEADS_PAD),
                  pl.BlockSpec((1, seq // TK, DSA_HEADS * VT_ROWS, TK), lambda b, n: (b, 0, 0, 0)),
                  qcols(IDX_HEADS * IDX_DIM), ktok(LANES), qcols(_IW_ROWS), const(bidx), const(tril)],
        out_specs=pl.BlockSpec((1, TQ, D_HEADS_PAD), lambda b, n: (b, n, 0)),
        out_shape=jax.ShapeDtypeStruct((batch, seq, D_HEADS_PAD), BF16),
        scratch_shapes=[pltpu.VMEM((DSA_HEADS, 2, TK, TQ), F32),
                        pltpu.VMEM((seq // TK, TK, TQ), F32),
                        pltpu.VMEM((n_streams, LANES, TQ), BF16),
                        pltpu.VMEM((seq // TK, n_streams, TK, TQ), F32),
                        pltpu.VMEM((n_streams, SUBLANES, TQ), F32),
                        pltpu.VMEM((n_streams, VT_ROWS, TQ), F32),
                        pltpu.VMEM((seq // TK, 1, TQ), F32),
                        pltpu.VMEM((1, TQ), F32),
                        pltpu.VMEM((4, SUBLANES, TQ), F32)],
        compiler_params=_cparams(2),
        name="dsa_attention",
    )(bias_t, aqT, ak.reshape(batch, seq, D_HEADS_PAD),
      avT.reshape(batch, seq // TK, DSA_HEADS * VT_ROWS, TK), iqT, ik.reshape(batch, seq, LANES), iwT, bidx, tril)
    return out.reshape(batch * seq, D_HEADS_PAD)


_DIFF_STREAMS = tuple(
    (h // PAIR, h, (h % PAIR) * HEAD_DIM + mm * DIFF_QK_DIM, (h % PAIR) * HEAD_DIM + (mm + 1) * DIFF_QK_DIM)
    for h in range(DIFF_HEADS) for mm in range(2))


def _diff_kernel(lambda_init, tab_ref, qt_ref, k_ref, vt_ref, lam_ref, g_ref, bidx_ref, avg_ref,
                 o_ref, band_ref, q_scr, logit_ref, m_ref, acc_ref):
    qb = pl.program_id(1)

    @pl.when((pl.program_id(0) == 0) & (qb == 0))
    def _():
        _build_band(bidx_ref, tab_ref, band_ref, DIFF_HEADS)

    lv = lam_ref[...]
    lam = (jnp.exp(jnp.sum(lv[0:1] * lv[1:2], axis=-1, keepdims=True))
           - jnp.exp(jnp.sum(lv[2:3] * lv[3:4], axis=-1, keepdims=True)) + lambda_init)

    _load_queries(qt_ref, q_scr, _DIFF_STREAMS)
    _softmax_passes(qb, k_ref, vt_ref, q_scr, band_ref, lambda kt: None, logit_ref, m_ref, acc_ref,
                    _DIFF_STREAMS)

    avg = avg_ref[...]
    for p in range(N_PAIRS):
        halves = []
        for j in range(PAIR):
            h = PAIR * p + j
            if h < DIFF_HEADS:
                halves.append(_normalised(acc_ref, 2 * h) - lam * _normalised(acc_ref, 2 * h + 1))
            else:
                halves.append(jnp.zeros((HEAD_DIM, TQ), F32))
        out = jnp.concatenate(halves, axis=0).T
        ms = _group_mean(out * out, avg)
        out = out * lax.rsqrt(ms + LN_EPS) * g_ref[...] * (1.0 - lambda_init)
        o_ref[0, :, pl.ds(p * LANES, LANES)] = out.astype(o_ref.dtype)


def _diff(dqT, dk, dvT, lam_vecs, subln_g, bias_t, lambda_init, batch, seq):
    bidx = jnp.asarray(_band_bucket_index_t())
    avg = jnp.asarray(_group_avg(), BF16)
    g2 = jnp.concatenate([subln_g, subln_g]).reshape(1, LANES)
    nq = seq // TQ
    n_streams = len(_DIFF_STREAMS)
    const = lambda a: pl.BlockSpec(a.shape, lambda b, n: (0,) * a.ndim)
    out = pl.pallas_call(
        functools.partial(_diff_kernel, lambda_init),
        grid=(batch, nq),
        in_specs=[pl.BlockSpec(memory_space=pltpu.SMEM),
                  pl.BlockSpec((D_HEADS_PAD, TQ), lambda b, n: (0, b * nq + n)),
                  pl.BlockSpec((1, seq, D_HEADS_PAD), lambda b, n: (b, 0, 0)),
                  pl.BlockSpec((1, seq // TK, DIFF_HEADS * VT_ROWS, TK), lambda b, n: (b, 0, 0, 0)),
                  const(lam_vecs), const(g2), const(bidx), const(avg)],
        out_specs=pl.BlockSpec((1, TQ, D_HEADS_PAD), lambda b, n: (b, n, 0)),
        out_shape=jax.ShapeDtypeStruct((batch, seq, D_HEADS_PAD), BF16),
        scratch_shapes=[pltpu.VMEM((DIFF_HEADS, 2, TK, TQ), F32),
                        pltpu.VMEM((n_streams, LANES, TQ), BF16),
                        pltpu.VMEM((seq // TK, n_streams, TK, TQ), F32),
                        pltpu.VMEM((n_streams, SUBLANES, TQ), F32),
                        pltpu.VMEM((n_streams, VT_ROWS, TQ), F32)],
        compiler_params=_cparams(2),
        name="diff_attention",
    )(bias_t, dqT, dk.reshape(batch, seq, D_HEADS_PAD),
      dvT.reshape(batch, seq // TK, DIFF_HEADS * VT_ROWS, TK), lam_vecs, g2, bidx, avg)
    return out.reshape(batch * seq, D_HEADS_PAD)


def _out_ffn_kernel(x_ref, ret_ref, dsa_ref, dif_ref, wo_ref, g1_ref, b1_ref, wg_ref, wu_ref, wd_ref,
                    g2_ref, b2_ref, o_ref):
    mixed = _dot(ret_ref[...], wo_ref[pl.ds(0, D_HEADS_PAD), :])
    mixed = mixed + _dot(dsa_ref[...], wo_ref[pl.ds(D_HEADS_PAD, D_HEADS_PAD), :])
    mixed = mixed + _dot(dif_ref[...], wo_ref[pl.ds(2 * D_HEADS_PAD, D_HEADS_PAD), :])
    x = _layer_norm(DEEPNORM_ALPHA * x_ref[...] + mixed, g1_ref[...], b1_ref[...])
    o_ref[...] = _ffn_ln_body(x, wg_ref, wu_ref, wd_ref, g2_ref[...], b2_ref[...])


def _pad_w_out(w_out):
    zeros = jnp.zeros((D_HEADS_PAD - D_DSA, w_out.shape[1]), w_out.dtype)
    return jnp.concatenate([w_out[:D_RET], w_out[D_RET:D_RET + D_DSA], zeros,
                            w_out[D_RET + D_DSA:], zeros], axis=0)


def _out_ffn(x, ret, dsa, dif, w_pad, g1, b1, wg, wu, wd, g2, b2, layer):
    t = x.shape[0]
    stacked = lambda shape: pl.BlockSpec((None,) + shape, lambda i: (layer, 0, 0), pipeline_mode=pl.Buffered(1))
    const = lambda shape: pl.BlockSpec(shape, lambda i: (0, 0), pipeline_mode=pl.Buffered(1))
    tok = lambda w: pl.BlockSpec((TM_FFN, w), lambda i: (i, 0))
    return pl.pallas_call(
        _out_ffn_kernel,
        grid=(t // TM_FFN,),
        in_specs=[tok(D_MODEL), tok(D_HEADS_PAD), tok(D_HEADS_PAD), tok(D_HEADS_PAD),
                  const((3 * D_HEADS_PAD, D_MODEL)), const((1, D_MODEL)), const((1, D_MODEL)),
                  stacked((D_MODEL, D_FF)), stacked((D_MODEL, D_FF)), stacked((D_FF, D_MODEL)),
                  const((1, D_MODEL)), const((1, D_MODEL))],
        out_specs=tok(D_MODEL),
        out_shape=jax.ShapeDtypeStruct((t, D_MODEL), F32),
        compiler_params=_cparams(1),
        name="out_proj_ffn_ln",
    )(x, ret, dsa, dif, w_pad, g1, b1, wg, wu, wd, g2, b2)


def kernel(x, w_in, w_out, ffn1_wg, ffn1_wu, ffn1_wd, ffn2_wg, ffn2_wu, ffn2_wd,
           ln_g, ln_b, diff_lambda, diff_subln_g, rel_bias):
    batch, seq, d = x.shape
    assert d == D_MODEL and seq % TQ == 0 and seq % (RET_BLOCKS * TR) == 0 and (batch * seq) % TM_FFN == 0
    assert min(DSA_TOPK_MAX, seq // 4) == DSA_TOPK_MAX
    h = x.reshape(batch * seq, d)
    bias_t = rel_bias.T
    row = lambda a: a.reshape(1, D_MODEL)
    for l in range(DEPTH):
        lambda_init = 0.8 - 0.6 * math.exp(-0.3 * l)
        h = _ffn_ln(h, ffn1_wg, ffn1_wu, ffn1_wd, row(ln_g[l, 0]), row(ln_b[l, 0]), l)
        (rq, rk, rv, rg, aqT, ak, avT, iqT, ik, iwT, dqT, dk, dvT) = _proj(h, w_in, l)
        ret = _retention(rq, rk, rv, rg, batch, seq)
        dsa = _dsa(aqT, ak, avT, iqT, ik, iwT, bias_t[:DSA_HEADS], batch, seq)
        dif = _diff(dqT, dk, dvT, diff_lambda[l], diff_subln_g[l], bias_t[DSA_HEADS:], lambda_init, batch, seq)
        h = _out_ffn(h, ret, dsa, dif, _pad_w_out(w_out[l]).astype(BF16), row(ln_g[l, 1]), row(ln_b[l, 1]),
                     ffn2_wg, ffn2_wu, ffn2_wd, row(ln_g[l, 2]), row(ln_b[l, 2]), l)
    return h.reshape(batch, seq, d)
```

```python
import functools
import math

import numpy as np
import jax
import jax.numpy as jnp
from jax import lax
from jax.experimental import pallas as pl
from jax.experimental.pallas import tpu as pltpu

D_MODEL = 1024
DEPTH = 2
CHUNK = 64
HEAD_DIM = 64
RET_HEADS = 6
DSA_HEADS = 5
IDX_HEADS = 4
IDX_DIM = 64
DSA_TOPK_MAX = 256
DIFF_HEADS = 5
DIFF_QK_DIM = HEAD_DIM // 2
D_RET = RET_HEADS * HEAD_DIM
D_DSA = DSA_HEADS * HEAD_DIM
D_DIFF = DIFF_HEADS * HEAD_DIM
D_FF = 2816
N_BUCKETS = 32
MAX_DISTANCE = 128
ROPE_BASE = 10000.0
LN_EPS = 1e-5
HEAD_NORM_EPS = 1e-6
DEEPNORM_ALPHA = (2 * DEPTH) ** 0.25

LANES = 128
SUBLANES = 8
BF16_ROWS = 16
PAIR = LANES // HEAD_DIM
VMEM_LIMIT_BYTES = 56 * 1024 * 1024

D_HEADS_PAD = 3 * LANES
N_PAIRS = D_HEADS_PAD // LANES
VT_ROWS = HEAD_DIM + BF16_ROWS
TQ = 256
TK = 256
TR = 256
RET_BLOCKS = 4
TM_FFN = 512
TF = 256
TM_PROJ = 512
TN_PROJ = 256
N_ACC = 4
SEARCH_STEPS_PER_CHECK = 4
SEARCH_ROUNDS = 6
NEG = -1e30
LOG2E = math.log2(math.e)

F32 = jnp.float32
BF16 = jnp.bfloat16


def _dot(a, b):
    return jnp.dot(a, b, preferred_element_type=F32)


def _dot_nt(a, b):
    return lax.dot_general(a, b, (((1,), (1,)), ((), ())), preferred_element_type=F32)


def _layer_norm(r, g, b):
    mu = jnp.mean(r, axis=-1, keepdims=True)
    d = r - mu
    var = jnp.mean(d * d, axis=-1, keepdims=True)
    return d * lax.rsqrt(var + LN_EPS) * g + b


def _group_mean(x, avg_bf16):
    hi = x.astype(BF16)
    lo = (x - hi.astype(F32)).astype(BF16)
    return _dot(hi, avg_bf16) + _dot(lo, avg_bf16)


def _cparams(n_axes):
    return pltpu.CompilerParams(
        dimension_semantics=("arbitrary",) * n_axes,
        vmem_limit_bytes=VMEM_LIMIT_BYTES,
    )


def _rel_bucket_static(rel):
    rel = np.asarray(rel, dtype=np.int64)
    half = N_BUCKETS // 2
    max_exact = half // 2
    n = np.abs(rel)
    large = np.full(n.shape, max_exact, dtype=np.int64)
    for k in range(1, 64):
        large = np.where(64 * (2 ** k) <= n * n, max_exact + k, large)
    large = np.minimum(large, half - 1)
    return (np.where(rel > 0, half, 0) + np.where(n < max_exact, n, large)).astype(np.int32)


FAR_BUCKET = int(_rel_bucket_static(-(TK + 1)))


@functools.lru_cache(maxsize=None)
def _band_bucket_index_t():
    j = np.arange(TK)[:, None]
    i = np.arange(TQ)[None, :]
    prev = _rel_bucket_static(j - TK - i)
    diag = _rel_bucket_static(j - i)
    visible = j < (i // CHUNK + 1) * CHUNK
    diag = np.where(visible, diag, -1)
    return np.stack([prev, diag]).astype(np.int32)


@functools.lru_cache(maxsize=None)
def _rotary_tables(seq):
    lane = np.arange(LANES)
    d = lane % HEAD_DIM
    f = d % (HEAD_DIM // 2)
    inv = ROPE_BASE ** (-(2.0 * f) / HEAD_DIM)
    ang = np.arange(seq, dtype=np.float64)[:, None] * inv[None, :]
    cos = np.cos(ang)
    sin = np.sin(ang) * np.where(d < HEAD_DIM // 2, -1.0, 1.0)[None, :]
    return cos.astype(np.float32), sin.astype(np.float32)


@functools.lru_cache(maxsize=None)
def _retention_tables():
    gamma = 1.0 - 2.0 ** (-5.0 - np.arange(RET_HEADS, dtype=np.float64))
    t = np.arange(TR)
    dist = np.abs(t[:, None] - t[None, :]).astype(np.float64)
    visible = t[None, :] < (t[:, None] // CHUNK + 1) * CHUNK
    intra = np.stack([np.where(visible, g ** dist, 0.0) for g in gamma])
    lane_head = np.arange(D_HEADS_PAD) // HEAD_DIM
    g_lane = gamma[lane_head]
    qdec = g_lane[None, :] ** (t[:, None] + 1.0)
    kdec = g_lane[None, :] ** (TR - 1.0 - t[:, None])
    a = np.arange(LANES)
    same = (a[:, None] // HEAD_DIM) == (a[None, :] // HEAD_DIM)
    sdec = np.stack([np.where(same, (gamma[PAIR * p + a // HEAD_DIM] ** TR)[:, None], 0.0)
                     for p in range(RET_HEADS // PAIR)])
    return (intra.astype(np.float32), qdec.astype(np.float32), kdec.astype(np.float32),
            sdec.astype(np.float32), same.astype(np.float32))


@functools.lru_cache(maxsize=None)
def _group_avg():
    a = np.arange(LANES)
    same = (a[:, None] // HEAD_DIM) == (a[None, :] // HEAD_DIM)
    return np.where(same, 1.0 / HEAD_DIM, 0.0).astype(np.float32)


@functools.lru_cache(maxsize=None)
def _prefix_tril():
    a = np.arange(TK)
    return (a[None, :] <= a[:, None]).astype(np.float32)


def _lane_group(shape):
    return lax.broadcasted_iota(jnp.int32, shape, len(shape) - 1) // HEAD_DIM


def _ffn_ln_body(x, wg_ref, wu_ref, wd_ref, g, b):
    xb = x.astype(BF16)
    acc = jnp.zeros(x.shape, F32)
    for c in range(D_FF // TF):
        cols = pl.ds(c * TF, TF)
        gate = _dot(xb, wg_ref[:, cols].astype(BF16))
        up = _dot(xb, wu_ref[:, cols].astype(BF16))
        h = gate * jax.nn.sigmoid(gate) * up
        acc = acc + _dot(h.astype(BF16), wd_ref[cols, :].astype(BF16))
    return _layer_norm(DEEPNORM_ALPHA * x + 0.5 * acc, g, b)


def _ffn_ln_kernel(x_ref, wg_ref, wu_ref, wd_ref, g_ref, b_ref, o_ref):
    o_ref[...] = _ffn_ln_body(x_ref[...], wg_ref, wu_ref, wd_ref, g_ref[...], b_ref[...])


def _ffn_ln(x, wg, wu, wd, g, b, layer):
    t = x.shape[0]
    stacked = lambda shape: pl.BlockSpec((None,) + shape, lambda i: (layer, 0, 0), pipeline_mode=pl.Buffered(1))
    full = lambda shape: pl.BlockSpec(shape, lambda i: (0, 0))
    return pl.pallas_call(
        _ffn_ln_kernel,
        grid=(t // TM_FFN,),
        in_specs=[pl.BlockSpec((TM_FFN, D_MODEL), lambda i: (i, 0)),
                  stacked((D_MODEL, D_FF)), stacked((D_MODEL, D_FF)), stacked((D_FF, D_MODEL)),
                  full((1, D_MODEL)), full((1, D_MODEL))],
        out_specs=pl.BlockSpec((TM_FFN, D_MODEL), lambda i: (i, 0)),
        out_shape=jax.ShapeDtypeStruct((t, D_MODEL), F32),
        compiler_params=_cparams(1),
        name="ffn_ln",
    )(x, wg, wu, wd, g, b)


IN_SPLITS = (D_RET, D_RET, D_RET, D_RET, D_DSA, D_DSA, D_DSA,
             IDX_HEADS * IDX_DIM, IDX_DIM, IDX_HEADS, D_DIFF, D_DIFF, D_DIFF)
N_IN = sum(IN_SPLITS)
N_IN_PAD = -(-N_IN // LANES) * LANES
_PROJ_OUT = (("rq", D_HEADS_PAD, F32, None, "tok"), ("rk", D_HEADS_PAD, F32, None, "tok"),
             ("rv", D_HEADS_PAD, BF16, None, "tok"), ("rg", D_HEADS_PAD, F32, None, "tok"),
             ("aq", D_HEADS_PAD, BF16, HEAD_DIM ** -0.5 * LOG2E, "T"), ("ak", D_HEADS_PAD, BF16, None, "tok"),
             ("av", DSA_HEADS, BF16, None, "VT"),
             ("iq", IDX_HEADS * IDX_DIM, BF16, None, "T"), ("ik", LANES, BF16, None, "tok"),
             ("iw", LANES, F32, None, "T"),
             ("dq", D_HEADS_PAD, BF16, DIFF_QK_DIM ** -0.5 * LOG2E, "T"), ("dk", D_HEADS_PAD, BF16, None, "tok"),
             ("dv", DIFF_HEADS, BF16, None, "VT"))
_IW_ROWS = SUBLANES


def _proj_kernel(x_ref, w_ref, *refs):
    o_refs, h_scr = refs[:-1], refs[-1]
    xb = x_ref[...].astype(BF16)
    for lo in range(0, N_IN, TN_PROJ):
        cols = pl.ds(lo, min(TN_PROJ, N_IN - lo))
        h_scr[:, cols] = _dot(xb, w_ref[:, cols].astype(BF16))
    ones = jnp.ones((BF16_ROWS, TK), BF16)
    off = 0
    for width, (name, out_width, dtype, scale, kind), o_ref in zip(IN_SPLITS, _PROJ_OUT, o_refs):
        base = off // LANES * LANES
        end = min(-(-(off + width) // LANES) * LANES, N_IN_PAD)
        seg = h_scr[:, pl.ds(base, end - base)][:, off - base:off - base + width]
        off += width
        if scale is not None:
            seg = seg * scale
        padded = -(-width // LANES) * LANES
        if name == "ik":
            seg = jnp.concatenate([seg, seg], axis=1)
        elif padded > width:
            seg = jnp.concatenate([seg, jnp.zeros((TM_PROJ, padded - width), F32)], axis=1)
        if kind == "tok":
            o_ref[...] = seg.astype(dtype)
        elif kind == "T":
            o_ref[...] = seg.T[:o_ref.shape[0]].astype(dtype)
        else:
            seg_t = seg.T.astype(dtype)
            for t in range(TM_PROJ // TK):
                for h in range(out_width):
                    o_ref[t, pl.ds(h * VT_ROWS, HEAD_DIM), :] = seg_t[h * HEAD_DIM:(h + 1) * HEAD_DIM,
                                                                      t * TK:(t + 1) * TK]
                    o_ref[t, pl.ds(h * VT_ROWS + HEAD_DIM, BF16_ROWS), :] = ones


def _proj(x, w_in, layer):
    t = x.shape[0]
    out_specs, out_shape = [], []
    for name, w, dt, _, kind in _PROJ_OUT:
        if kind == "tok":
            out_specs.append(pl.BlockSpec((TM_PROJ, w), lambda i: (i, 0)))
            out_shape.append(jax.ShapeDtypeStruct((t, w), dt))
        elif kind == "T":
            rows = _IW_ROWS if name == "iw" else w
            out_specs.append(pl.BlockSpec((rows, TM_PROJ), lambda i: (0, i)))
            out_shape.append(jax.ShapeDtypeStruct((rows, t), dt))
        else:
            out_specs.append(pl.BlockSpec((TM_PROJ // TK, w * VT_ROWS, TK), lambda i: (i, 0, 0)))
            out_shape.append(jax.ShapeDtypeStruct((t // TK, w * VT_ROWS, TK), dt))
    return pl.pallas_call(
        _proj_kernel,
        grid=(t // TM_PROJ,),
        in_specs=[pl.BlockSpec((TM_PROJ, D_MODEL), lambda i: (i, 0)),
                  pl.BlockSpec((None, D_MODEL, N_IN), lambda i: (layer, 0, 0), pipeline_mode=pl.Buffered(1))],
        out_specs=out_specs,
        out_shape=out_shape,
        scratch_shapes=[pltpu.VMEM((TM_PROJ, N_IN_PAD), F32)],
        compiler_params=_cparams(1),
        name="in_proj",
    )(x, w_in)


def _swap_halves(x):
    lane = lax.broadcasted_iota(jnp.int32, x.shape, 1)
    first_half = (lane % HEAD_DIM) < (HEAD_DIM // 2)
    from_right = pltpu.roll(x, LANES - HEAD_DIM // 2, 1)
    from_left = pltpu.roll(x, HEAD_DIM // 2, 1)
    return jnp.where(first_half, from_right, from_left)


def _retention_kernel(q_ref, k_ref, v_ref, g_ref, cos_ref, sin_ref, intra_ref, qdec_ref, kdec_ref,
                      sdec_ref, same_ref, avg_ref, o_ref, state_ref):
    @pl.when(pl.program_id(1) == 0)
    def _():
        state_ref[...] = jnp.zeros(state_ref.shape, F32)

    avg = avg_ref[...]
    group = _lane_group((TR, LANES))
    for p in range(RET_HEADS // PAIR):
        lanes = pl.ds(p * LANES, LANES)
        for blk in range(RET_BLOCKS):
            rows = pl.ds(blk * TR, TR)
            cos = cos_ref[rows, :]
            sin = sin_ref[rows, :]
            q = q_ref[0, rows, lanes]
            k = k_ref[0, rows, lanes]
            q = (q * cos + _swap_halves(q) * sin) * (HEAD_DIM ** -0.5)
            k = k * cos + _swap_halves(k) * sin
            v = v_ref[0, rows, lanes]
            kb = k.astype(BF16)
            intra = jnp.zeros((TR, LANES), F32)
            for j in range(PAIR):
                qh = jnp.where(group == j, q, 0.0).astype(BF16)
                scores = _dot_nt(qh, kb) * intra_ref[PAIR * p + j]
                oh = _dot(scores.astype(BF16), v)
                intra = jnp.where(group == j, oh, intra)
            state = state_ref[p]
            cross = _dot((q * qdec_ref[:, lanes]).astype(BF16), state.astype(BF16))
            o = intra + cross
            kd_t = (k * kdec_ref[:, lanes]).T.astype(BF16)
            state_ref[p] = state * sdec_ref[p] + _dot(kd_t, v) * same_ref[...]
            mu = _group_mean(o, avg)
            d = o - mu
            var = _group_mean(d * d, avg)
            normed = d * lax.rsqrt(var + HEAD_NORM_EPS)
            gate = g_ref[0, rows, lanes]
            o_ref[0, rows, lanes] = (gate * jax.nn.sigmoid(gate) * normed).astype(o_ref.dtype)


def _retention(rq, rk, rv, rg, batch, seq):
    cos, sin = _rotary_tables(seq)
    intra, qdec, kdec, sdec, same = _retention_tables()
    avg = jnp.asarray(_group_avg(), BF16)
    step = RET_BLOCKS * TR
    r3 = lambda a: a.reshape(batch, seq, D_HEADS_PAD)
    tok = pl.BlockSpec((1, step, D_HEADS_PAD), lambda b, n: (b, n, 0))
    const = lambda a: pl.BlockSpec(a.shape, lambda b, n: (0,) * a.ndim)
    pos = pl.BlockSpec((step, LANES), lambda b, n: (n, 0))
    out = pl.pallas_call(
        _retention_kernel,
        grid=(batch, seq // step),
        in_specs=[tok, tok, tok, tok, pos, pos, const(intra), const(qdec), const(kdec),
                  const(sdec), const(same), const(avg)],
        out_specs=tok,
        out_shape=jax.ShapeDtypeStruct((batch, seq, D_HEADS_PAD), BF16),
        scratch_shapes=[pltpu.VMEM((RET_HEADS // PAIR, LANES, LANES), F32)],
        compiler_params=_cparams(2),
        name="retention",
    )(r3(rq), r3(rk), r3(rv), r3(rg), cos, sin, intra, qdec, kdec, sdec, same, avg)
    return out.reshape(batch * seq, D_HEADS_PAD)


def _build_band(bidx_ref, tab_ref, band_ref, n_heads):
    for h in range(n_heads):
        far = tab_ref[h, FAR_BUCKET]
        for t in range(2):
            idx = bidx_ref[t]
            val = jnp.full((TK, TQ), NEG, F32)
            for bkt in range(N_BUCKETS):
                val = jnp.where(idx == bkt, (tab_ref[h, bkt] - far) * LOG2E, val)
            band_ref[h, t] = val


def _load_queries(qt_ref, q_scr, streams):
    row = lax.broadcasted_iota(jnp.int32, (LANES, TQ), 0)
    for i, (pair, _, lo, hi) in enumerate(streams):
        q_scr[i] = jnp.where((row >= lo) & (row < hi), qt_ref[pl.ds(pair * LANES, LANES), :], 0)


def _for_tiles(n, fn):
    def quad(j, carry):
        for u in range(4):
            fn(4 * j + u)
        return carry

    lax.fori_loop(0, lax.shift_right_logical(n, 2), quad, 0)
    rest = lax.rem(n, 4)

    @pl.when(rest >= 2)
    def _():
        fn(n - rest)
        fn(n - rest + 1)

    @pl.when(lax.rem(rest, 2) == 1)
    def _():
        fn(n - 1)


def _softmax_passes(qb, k_ref, vt_ref, q_scr, band_ref, mask_fn, logit_ref, m_ref, acc_ref, streams):
    m_ref[...] = jnp.full(m_ref.shape, NEG, F32)

    def logits_tile(kt, band_t):
        rows = pl.ds(pl.multiple_of(kt * TK, TK), TK)
        mask = mask_fn(kt)
        for i, (pair, head, _, _) in enumerate(streams):
            lg = _dot(k_ref[0, rows, pl.ds(pair * LANES, LANES)], q_scr[i])
            if band_t is not None:
                lg = lg + band_ref[head, band_t]
            if mask is not None:
                lg = lg + mask
            logit_ref[kt, i] = lg
            m_ref[i] = jnp.maximum(m_ref[i], jnp.max(lg.reshape(TK // SUBLANES, SUBLANES, TQ), axis=0))

    _for_tiles(jnp.maximum(qb - 1, 0), lambda kt: logits_tile(kt, None))

    @pl.when(qb >= 1)
    def _():
        logits_tile(qb - 1, 0)
        logits_tile(qb, 1)

    @pl.when(qb == 0)
    def _():
        logits_tile(qb, 1)

    m_fin = [jnp.max(m_ref[i], axis=0, keepdims=True) for i in range(len(streams))]
    acc_ref[...] = jnp.zeros(acc_ref.shape, F32)

    def values_tile(kt):
        for i, (_, head, _, _) in enumerate(streams):
            p = jnp.exp2(logit_ref[kt, i] - m_fin[i])
            acc_ref[i] += _dot(vt_ref[0, kt, pl.ds(head * VT_ROWS, VT_ROWS), :], p.astype(BF16))

    _for_tiles(qb + 1, values_tile)


def _normalised(acc_ref, i):
    a = acc_ref[i]
    return a[:HEAD_DIM] / a[HEAD_DIM:HEAD_DIM + 1]


_DSA_STREAMS = tuple((h // PAIR, h, (h % PAIR) * HEAD_DIM, (h % PAIR + 1) * HEAD_DIM) for h in range(DSA_HEADS))


def _f32_to_key(x):
    bits = lax.bitcast_convert_type(x, jnp.int32)
    return jnp.where(bits < 0, ~bits, bits ^ jnp.int32(-2 ** 31))


def _key_to_f32(key):
    bits = jnp.where(key < 0, key ^ jnp.int32(-2 ** 31), ~key)
    return lax.bitcast_convert_type(bits, F32)


def _dsa_kernel(tab_ref, qt_ref, k_ref, vt_ref, iqt_ref, ik_ref, iwt_ref, bidx_ref, tril_ref,
                o_ref, band_ref, score_ref, q_scr, logit_ref, m_ref, acc_ref, tiecnt_ref, tie_ref, stat_ref):
    qb = pl.program_id(1)
    n_tiles = qb + 1

    @pl.when((pl.program_id(0) == 0) & (qb == 0))
    def _():
        _build_band(bidx_ref, tab_ref, band_ref, DSA_HEADS)

    iw = iwt_ref[...] * (IDX_HEADS ** -0.5 * IDX_DIM ** -0.5)
    row = lax.broadcasted_iota(jnp.int32, (LANES, TQ), 0)
    iq = [jnp.where(row // HEAD_DIM == (h % PAIR), iqt_ref[pl.ds((h // PAIR) * LANES, LANES), :], 0)
          for h in range(IDX_HEADS)]

    def score_tile(kt):
        ki = ik_ref[0, pl.ds(pl.multiple_of(kt * TK, TK), TK), :]
        s = jnp.zeros((TK, TQ), F32)
        for h in range(IDX_HEADS):
            s = s + jnp.maximum(_dot(ki, iq[h]), 0.0) * iw[h:h + 1, :]
        return s

    stat_ref[0] = jnp.full((SUBLANES, TQ), -jnp.inf, F32)
    stat_ref[1] = jnp.full((SUBLANES, TQ), jnp.inf, F32)
    stat_ref[2] = jnp.zeros((SUBLANES, TQ), F32)
    stat_ref[3] = jnp.zeros((SUBLANES, TQ), F32)

    def add_stats(s_for_max, s_for_min):
        hi_part = s_for_max.reshape(TK // SUBLANES, SUBLANES, TQ)
        stat_ref[0] = jnp.maximum(stat_ref[0], jnp.max(hi_part, axis=0))
        stat_ref[1] = jnp.minimum(stat_ref[1], jnp.min(s_for_min.reshape(TK // SUBLANES, SUBLANES, TQ), axis=0))
        stat_ref[2] += jnp.sum(jnp.where(hi_part > 0.0, 1.0, 0.0), axis=0)
        stat_ref[3] += jnp.sum(jnp.where(hi_part >= 0.0, 1.0, 0.0), axis=0)

    def past_scores(kt):
        s = score_tile(kt)
        score_ref[kt] = s
        add_stats(s, s)

    _for_tiles(qb, past_scores)
    s_diag = score_tile(qb)
    visible = band_ref[0, 1] > 0.5 * NEG
    score_ref[qb] = jnp.where(visible, s_diag, -jnp.inf)
    add_stats(jnp.where(visible, s_diag, -jnp.inf), jnp.where(visible, s_diag, jnp.inf))

    k_sel = float(DSA_TOPK_MAX)

    def count(pred):
        def body(kt, c):
            xs = score_ref[kt].reshape(TK // SUBLANES, SUBLANES, TQ)
            accs = [c[a] for a in range(N_ACC)]
            for g in range(TK // SUBLANES):
                a = accs[g % N_ACC]
                accs[g % N_ACC] = jnp.where(pred(xs[g]), a + 1.0, a)
            return jnp.stack(accs)
        c = lax.fori_loop(0, n_tiles, body, jnp.zeros((N_ACC, SUBLANES, TQ), F32))
        return jnp.sum(jnp.sum(c, axis=0), axis=0, keepdims=True)

    col = lax.broadcasted_iota(jnp.int32, (1, TQ), 1)
    limit = qb * TQ + (col // CHUNK + 1) * CHUNK
    select_all = limit <= DSA_TOPK_MAX
    s_max = jnp.max(stat_ref[0], axis=0, keepdims=True)
    s_min = jnp.min(stat_ref[1], axis=0, keepdims=True)
    n_pos = jnp.sum(stat_ref[2], axis=0, keepdims=True)
    n_nonneg = jnp.sum(stat_ref[3], axis=0, keepdims=True)

    positive = n_pos > k_sel
    zero_thr = (n_pos <= k_sel) & (k_sel <= n_nonneg)
    lo0 = jnp.where(positive, 0.0, _key_to_f32(_f32_to_key(s_min) - 1))
    hi0 = jnp.where(positive, _key_to_f32(_f32_to_key(s_max) + 1), 0.0)
    flo0 = jnp.where(positive, n_pos, limit.astype(F32)) - k_sel
    fhi0 = jnp.where(positive, 0.0, n_nonneg) - k_sel
    done0 = jnp.where(select_all | zero_thr, 1.0, 0.0)
    thr0 = jnp.where(select_all, -jnp.inf, 0.0)

    def search_step(vec):
        lo, hi, flo, fhi, side, done, thr = vec
        w = flo / (flo - fhi)
        t = lo * (1.0 - w) + hi * w
        t = jnp.where((t > lo) & (t < hi), t, 0.5 * lo + 0.5 * hi)
        splits = (t > lo) & (t < hi)
        f = count(lambda s: s >= t) - k_sel
        active = done == 0.0
        hit = active & splits & (f == 0.0)
        adjacent = active & ~splits
        thr = jnp.where(hit, t, jnp.where(adjacent, lo, thr))
        done = jnp.where(hit | adjacent, 1.0, done)
        up = active & splits & (f > 0.0)
        down = active & splits & (f < 0.0)
        fhi_new = jnp.where(up & (side > 0.0), 0.5 * fhi, jnp.where(down, f, fhi))
        flo_new = jnp.where(down & (side < 0.0), 0.5 * flo, jnp.where(up, f, flo))
        lo = jnp.where(up, t, lo)
        hi = jnp.where(down, t, hi)
        side = jnp.where(up, 1.0, jnp.where(down, -1.0, side))
        return lo, hi, flo_new, fhi_new, side, done, thr

    def open_queries(done):
        return (jnp.min(done) == 0.0).astype(jnp.int32)

    def search_round(state):
        rounds, _, vec = state
        for _ in range(SEARCH_STEPS_PER_CHECK):
            vec = search_step(vec)
        return rounds + 1, open_queries(vec[5]), vec

    vec0 = (lo0, hi0, flo0, fhi0, jnp.zeros((1, TQ), F32), done0, thr0)
    state = lax.while_loop(lambda st: (st[0] < SEARCH_ROUNDS) & (st[1] > 0), search_round,
                           (jnp.int32(0), open_queries(done0), vec0))
    done, thr = state[2][5], state[2][6]

    def bisect(i, prefix):
        cand = prefix | lax.shift_left(jnp.int32(1), 31 - i)
        cand_f = _key_to_f32(cand)
        take = count(lambda s: s >= cand_f) >= k_sel
        return jnp.where(take, cand, prefix)

    def exact_thr():
        prefix = lax.fori_loop(0, 32, bisect, jnp.zeros((1, TQ), jnp.int32))
        return jnp.where(done == 0.0, _key_to_f32(prefix), thr)

    thr = lax.cond(state[1] > 0, exact_thr, lambda: thr)
    n_gt = count(lambda s: s > thr)
    need = jnp.where(select_all, 0.0, k_sel - n_gt)

    def tie_counts(kt, carry):
        tied = jnp.where(score_ref[kt] == thr, 1.0, 0.0).reshape(TK // SUBLANES, SUBLANES, TQ)
        tiecnt_ref[kt] = jnp.sum(jnp.sum(tied, axis=0), axis=0, keepdims=True)
        return carry

    lax.fori_loop(0, n_tiles, tie_counts, 0)
    tie_ref[...] = jnp.zeros(tie_ref.shape, F32)
    tril = tril_ref[...]

    def selection_mask(kt):
        s = score_ref[kt]
        tied = s == thr
        tied_before = tie_ref[...]
        tie_ref[...] = tied_before + tiecnt_ref[kt]
        tied_upto = tied_before + _dot(tril, jnp.where(tied, 1.0, 0.0).astype(BF16))
        return jnp.where((s > thr) | (tied & (tied_upto <= need)), 0.0, NEG)

    _load_queries(qt_ref, q_scr, _DSA_STREAMS)
    _softmax_passes(qb, k_ref, vt_ref, q_scr, band_ref, selection_mask, logit_ref, m_ref, acc_ref, _DSA_STREAMS)

    for p in range(N_PAIRS):
        halves = [_normalised(acc_ref, PAIR * p + j) if PAIR * p + j < DSA_HEADS
                  else jnp.zeros((HEAD_DIM, TQ), F32) for j in range(PAIR)]
        o_ref[0, :, pl.ds(p * LANES, LANES)] = jnp.concatenate(halves, axis=0).T.astype(o_ref.dtype)


def _dsa(aqT, ak, avT, iqT, ik, iwT, bias_t, batch, seq):
    bidx = jnp.asarray(_band_bucket_index_t())
    tril = jnp.asarray(_prefix_tril(), BF16)
    nq = seq // TQ
    n_streams = len(_DSA_STREAMS)
    qcols = lambda w: pl.BlockSpec((w, TQ), lambda b, n: (0, b * nq + n))
    ktok = lambda w: pl.BlockSpec((1, seq, w), lambda b, n: (b, 0, 0))
    const = lambda a: pl.BlockSpec(a.shape, lambda b, n: (0,) * a.ndim)
    out = pl.pallas_call(
        _dsa_kernel,
        grid=(batch, nq),
        in_specs=[pl.BlockSpec(memory_space=pltpu.SMEM),
                  qcols(D_HEADS_PAD), ktok(D_HEADS_PAD),
                  pl.BlockSpec((1, seq // TK, DSA_HEADS * VT_ROWS, TK), lambda b, n: (b, 0, 0, 0)),
                  qcols(IDX_HEADS * IDX_DIM), ktok(LANES), qcols(_IW_ROWS), const(bidx), const(tril)],
        out_specs=pl.BlockSpec((1, TQ, D_HEADS_PAD), lambda b, n: (b, n, 0)),
        out_shape=jax.ShapeDtypeStruct((batch, seq, D_HEADS_PAD), BF16),
        scratch_shapes=[pltpu.VMEM((DSA_HEADS, 2, TK, TQ), F32),
                        pltpu.VMEM((seq // TK, TK, TQ), F32),
                        pltpu.VMEM((n_streams, LANES, TQ), BF16),
                        pltpu.VMEM((seq // TK, n_streams, TK, TQ), F32),
                        pltpu.VMEM((n_streams, SUBLANES, TQ), F32),
                        pltpu.VMEM((n_streams, VT_ROWS, TQ), F32),
                        pltpu.VMEM((seq // TK, 1, TQ), F32),
                        pltpu.VMEM((1, TQ), F32),
                        pltpu.VMEM((4, SUBLANES, TQ), F32)],
        compiler_params=_cparams(2),
        name="dsa_attention",
    )(bias_t, aqT, ak.reshape(batch, seq, D_HEADS_PAD),
      avT.reshape(batch, seq // TK, DSA_HEADS * VT_ROWS, TK), iqT, ik.reshape(batch, seq, LANES), iwT, bidx, tril)
    return out.reshape(batch * seq, D_HEADS_PAD)


_DIFF_STREAMS = tuple(
    (h // PAIR, h, (h % PAIR) * HEAD_DIM + mm * DIFF_QK_DIM, (h % PAIR) * HEAD_DIM + (mm + 1) * DIFF_QK_DIM)
    for h in range(DIFF_HEADS) for mm in range(2))


def _diff_kernel(lambda_init, tab_ref, qt_ref, k_ref, vt_ref, lam_ref, g_ref, bidx_ref, avg_ref,
                 o_ref, band_ref, q_scr, logit_ref, m_ref, acc_ref):
    qb = pl.program_id(1)

    @pl.when((pl.program_id(0) == 0) & (qb == 0))
    def _():
        _build_band(bidx_ref, tab_ref, band_ref, DIFF_HEADS)

    lv = lam_ref[...]
    lam = (jnp.exp(jnp.sum(lv[0:1] * lv[1:2], axis=-1, keepdims=True))
           - jnp.exp(jnp.sum(lv[2:3] * lv[3:4], axis=-1, keepdims=True)) + lambda_init)

    _load_queries(qt_ref, q_scr, _DIFF_STREAMS)
    _softmax_passes(qb, k_ref, vt_ref, q_scr, band_ref, lambda kt: None, logit_ref, m_ref, acc_ref,
                    _DIFF_STREAMS)

    avg = avg_ref[...]
    for p in range(N_PAIRS):
        halves = []
        for j in range(PAIR):
            h = PAIR * p + j
            if h < DIFF_HEADS:
                halves.append(_normalised(acc_ref, 2 * h) - lam * _normalised(acc_ref, 2 * h + 1))
            else:
                halves.append(jnp.zeros((HEAD_DIM, TQ), F32))
        out = jnp.concatenate(halves, axis=0).T
        ms = _group_mean(out * out, avg)
        out = out * lax.rsqrt(ms + LN_EPS) * g_ref[...] * (1.0 - lambda_init)
        o_ref[0, :, pl.ds(p * LANES, LANES)] = out.astype(o_ref.dtype)


def _diff(dqT, dk, dvT, lam_vecs, subln_g, bias_t, lambda_init, batch, seq):
    bidx = jnp.asarray(_band_bucket_index_t())
    avg = jnp.asarray(_group_avg(), BF16)
    g2 = jnp.concatenate([subln_g, subln_g]).reshape(1, LANES)
    nq = seq // TQ
    n_streams = len(_DIFF_STREAMS)
    const = lambda a: pl.BlockSpec(a.shape, lambda b, n: (0,) * a.ndim)
    out = pl.pallas_call(
        functools.partial(_diff_kernel, lambda_init),
        grid=(batch, nq),
        in_specs=[pl.BlockSpec(memory_space=pltpu.SMEM),
                  pl.BlockSpec((D_HEADS_PAD, TQ), lambda b, n: (0, b * nq + n)),
                  pl.BlockSpec((1, seq, D_HEADS_PAD), lambda b, n: (b, 0, 0)),
                  pl.BlockSpec((1, seq // TK, DIFF_HEADS * VT_ROWS, TK), lambda b, n: (b, 0, 0, 0)),
                  const(lam_vecs), const(g2), const(bidx), const(avg)],
        out_specs=pl.BlockSpec((1, TQ, D_HEADS_PAD), lambda b, n: (b, n, 0)),
        out_shape=jax.ShapeDtypeStruct((batch, seq, D_HEADS_PAD), BF16),
        scratch_shapes=[pltpu.VMEM((DIFF_HEADS, 2, TK, TQ), F32),
                        pltpu.VMEM((n_streams, LANES, TQ), BF16),
                        pltpu.VMEM((seq // TK, n_streams, TK, TQ), F32),
                        pltpu.VMEM((n_streams, SUBLANES, TQ), F32),
                        pltpu.VMEM((n_streams, VT_ROWS, TQ), F32)],
        compiler_params=_cparams(2),
        name="diff_attention",
    )(bias_t, dqT, dk.reshape(batch, seq, D_HEADS_PAD),
      dvT.reshape(batch, seq // TK, DIFF_HEADS * VT_ROWS, TK), lam_vecs, g2, bidx, avg)
    return out.reshape(batch * seq, D_HEADS_PAD)


def _out_ffn_kernel(x_ref, ret_ref, dsa_ref, dif_ref, wo_ref, g1_ref, b1_ref, wg_ref, wu_ref, wd_ref,
                    g2_ref, b2_ref, o_ref):
    mixed = _dot(jnp.concatenate([ret_ref[...], dsa_ref[...], dif_ref[...]], axis=1), wo_ref[...])
    x = _layer_norm(DEEPNORM_ALPHA * x_ref[...] + mixed, g1_ref[...], b1_ref[...])
    o_ref[...] = _ffn_ln_body(x, wg_ref, wu_ref, wd_ref, g2_ref[...], b2_ref[...])


def _pad_w_out(w_out):
    zeros = jnp.zeros((D_HEADS_PAD - D_DSA, w_out.shape[1]), w_out.dtype)
    return jnp.concatenate([w_out[:D_RET], w_out[D_RET:D_RET + D_DSA], zeros,
                            w_out[D_RET + D_DSA:], zeros], axis=0)


def _out_ffn(x, ret, dsa, dif, w_pad, g1, b1, wg, wu, wd, g2, b2, layer):
    t = x.shape[0]
    stacked = lambda shape: pl.BlockSpec((None,) + shape, lambda i: (layer, 0, 0), pipeline_mode=pl.Buffered(1))
    const = lambda shape: pl.BlockSpec(shape, lambda i: (0, 0), pipeline_mode=pl.Buffered(1))
    tok = lambda w: pl.BlockSpec((TM_FFN, w), lambda i: (i, 0))
    return pl.pallas_call(
        _out_ffn_kernel,
        grid=(t // TM_FFN,),
        in_specs=[tok(D_MODEL), tok(D_HEADS_PAD), tok(D_HEADS_PAD), tok(D_HEADS_PAD),
                  const((3 * D_HEADS_PAD, D_MODEL)), const((1, D_MODEL)), const((1, D_MODEL)),
                  stacked((D_MODEL, D_FF)), stacked((D_MODEL, D_FF)), stacked((D_FF, D_MODEL)),
                  const((1, D_MODEL)), const((1, D_MODEL))],
        out_specs=tok(D_MODEL),
        out_shape=jax.ShapeDtypeStruct((t, D_MODEL), F32),
        compiler_params=_cparams(1),
        name="out_proj_ffn_ln",
    )(x, ret, dsa, dif, w_pad, g1, b1, wg, wu, wd, g2, b2)


def kernel(x, w_in, w_out, ffn1_wg, ffn1_wu, ffn1_wd, ffn2_wg, ffn2_wu, ffn2_wd,
           ln_g, ln_b, diff_lambda, diff_subln_g, rel_bias):
    batch, seq, d = x.shape
    assert d == D_MODEL and seq % TQ == 0 and seq % (RET_BLOCKS * TR) == 0 and (batch * seq) % TM_FFN == 0
    assert min(DSA_TOPK_MAX, seq // 4) == DSA_TOPK_MAX
    h = x.reshape(batch * seq, d)
    bias_t = rel_bias.T
    row = lambda a: a.reshape(1, D_MODEL)
    for l in range(DEPTH):
        lambda_init = 0.8 - 0.6 * math.exp(-0.3 * l)
        h = _ffn_ln(h, ffn1_wg, ffn1_wu, ffn1_wd, row(ln_g[l, 0]), row(ln_b[l, 0]), l)
        (rq, rk, rv, rg, aqT, ak, avT, iqT, ik, iwT, dqT, dk, dvT) = _proj(h, w_in, l)
        ret = _retention(rq, rk, rv, rg, batch, seq)
        dsa = _dsa(aqT, ak, avT, iqT, ik, iwT, bias_t[:DSA_HEADS], batch, seq)
        dif = _diff(dqT, dk, dvT, diff_lambda[l], diff_subln_g[l], bias_t[DSA_HEADS:], lambda_init, batch, seq)
        h = _out_ffn(h, ret, dsa, dif, _pad_w_out(w_out[l]).astype(BF16), row(ln_g[l, 1]), row(ln_b[l, 1]),
                     ffn2_wg, ffn2_wu, ffn2_wd, row(ln_g[l, 2]), row(ln_b[l, 2]), l)
    return h.reshape(batch, seq, d)
```

```python
import functools
import math

import numpy as np
import jax
import jax.numpy as jnp
from jax import lax
from jax.experimental import pallas as pl
from jax.experimental.pallas import tpu as pltpu

D_MODEL = 1024
DEPTH = 2
CHUNK = 64
HEAD_DIM = 64
RET_HEADS = 6
DSA_HEADS = 5
IDX_HEADS = 4
IDX_DIM = 64
DSA_TOPK_MAX = 256
DIFF_HEADS = 5
DIFF_QK_DIM = HEAD_DIM // 2
D_RET = RET_HEADS * HEAD_DIM
D_DSA = DSA_HEADS * HEAD_DIM
D_DIFF = DIFF_HEADS * HEAD_DIM
D_FF = 2816
N_BUCKETS = 32
MAX_DISTANCE = 128
ROPE_BASE = 10000.0
LN_EPS = 1e-5
HEAD_NORM_EPS = 1e-6
DEEPNORM_ALPHA = (2 * DEPTH) ** 0.25

LANES = 128
SUBLANES = 8
BF16_ROWS = 16
PAIR = LANES // HEAD_DIM
VMEM_LIMIT_BYTES = 56 * 1024 * 1024

D_HEADS_PAD = 3 * LANES
N_PAIRS = D_HEADS_PAD // LANES
VT_ROWS = HEAD_DIM + BF16_ROWS
TQ = 256
TK = 256
TR = 256
RET_BLOCKS = 4
TM_FFN = 512
TF = 256
TM_PROJ = 512
TN_PROJ = 256
N_ACC = 4
SEARCH_STEPS_PER_CHECK = 4
SEARCH_ROUNDS = 6
NEG = -1e30
LOG2E = math.log2(math.e)

F32 = jnp.float32
BF16 = jnp.bfloat16


def _dot(a, b):
    return jnp.dot(a, b, preferred_element_type=F32)


def _dot_nt(a, b):
    return lax.dot_general(a, b, (((1,), (1,)), ((), ())), preferred_element_type=F32)


def _layer_norm(r, g, b):
    mu = jnp.mean(r, axis=-1, keepdims=True)
    d = r - mu
    var = jnp.mean(d * d, axis=-1, keepdims=True)
    return d * lax.rsqrt(var + LN_EPS) * g + b


def _group_mean(x, avg_bf16):
    hi = x.astype(BF16)
    lo = (x - hi.astype(F32)).astype(BF16)
    return _dot(hi, avg_bf16) + _dot(lo, avg_bf16)


def _cparams(n_axes):
    return pltpu.CompilerParams(
        dimension_semantics=("arbitrary",) * n_axes,
        vmem_limit_bytes=VMEM_LIMIT_BYTES,
    )


def _rel_bucket_static(rel):
    rel = np.asarray(rel, dtype=np.int64)
    half = N_BUCKETS // 2
    max_exact = half // 2
    n = np.abs(rel)
    large = np.full(n.shape, max_exact, dtype=np.int64)
    for k in range(1, 64):
        large = np.where(64 * (2 ** k) <= n * n, max_exact + k, large)
    large = np.minimum(large, half - 1)
    return (np.where(rel > 0, half, 0) + np.where(n < max_exact, n, large)).astype(np.int32)


FAR_BUCKET = int(_rel_bucket_static(-(TK + 1)))


@functools.lru_cache(maxsize=None)
def _band_bucket_index_t():
    j = np.arange(TK)[:, None]
    i = np.arange(TQ)[None, :]
    prev = _rel_bucket_static(j - TK - i)
    diag = _rel_bucket_static(j - i)
    visible = j < (i // CHUNK + 1) * CHUNK
    diag = np.where(visible, diag, -1)
    return np.stack([prev, diag]).astype(np.int32)


@functools.lru_cache(maxsize=None)
def _rotary_tables(seq):
    lane = np.arange(LANES)
    d = lane % HEAD_DIM
    f = d % (HEAD_DIM // 2)
    inv = ROPE_BASE ** (-(2.0 * f) / HEAD_DIM)
    ang = np.arange(seq, dtype=np.float64)[:, None] * inv[None, :]
    cos = np.cos(ang)
    sin = np.sin(ang) * np.where(d < HEAD_DIM // 2, -1.0, 1.0)[None, :]
    return cos.astype(np.float32), sin.astype(np.float32)


@functools.lru_cache(maxsize=None)
def _retention_tables():
    gamma = 1.0 - 2.0 ** (-5.0 - np.arange(RET_HEADS, dtype=np.float64))
    t = np.arange(TR)
    dist = np.abs(t[:, None] - t[None, :]).astype(np.float64)
    visible = t[None, :] < (t[:, None] // CHUNK + 1) * CHUNK
    intra = np.stack([np.where(visible, g ** dist, 0.0) for g in gamma])
    lane_head = np.arange(D_HEADS_PAD) // HEAD_DIM
    g_lane = gamma[lane_head]
    qdec = g_lane[None, :] ** (t[:, None] + 1.0)
    kdec = g_lane[None, :] ** (TR - 1.0 - t[:, None])
    a = np.arange(LANES)
    same = (a[:, None] // HEAD_DIM) == (a[None, :] // HEAD_DIM)
    sdec = np.stack([np.where(same, (gamma[PAIR * p + a // HEAD_DIM] ** TR)[:, None], 0.0)
                     for p in range(RET_HEADS // PAIR)])
    return (intra.astype(np.float32), qdec.astype(np.float32), kdec.astype(np.float32),
            sdec.astype(np.float32), same.astype(np.float32))


@functools.lru_cache(maxsize=None)
def _group_avg():
    a = np.arange(LANES)
    same = (a[:, None] // HEAD_DIM) == (a[None, :] // HEAD_DIM)
    return np.where(same, 1.0 / HEAD_DIM, 0.0).astype(np.float32)


@functools.lru_cache(maxsize=None)
def _prefix_tril():
    a = np.arange(TK)
    return (a[None, :] <= a[:, None]).astype(np.float32)


def _lane_group(shape):
    return lax.broadcasted_iota(jnp.int32, shape, len(shape) - 1) // HEAD_DIM


def _ffn_ln_body(x, wg_ref, wu_ref, wd_ref, g, b, before_chunk=None):
    xb = x.astype(BF16)
    acc = jnp.zeros(x.shape, F32)
    for c in range(D_FF // TF):
        if before_chunk is not None:
            before_chunk(c)
        cols = pl.ds(c * TF, TF)
        gate = _dot(xb, wg_ref[:, cols].astype(BF16))
        up = _dot(xb, wu_ref[:, cols].astype(BF16))
        h = gate * jax.nn.sigmoid(gate) * up
        acc = acc + _dot(h.astype(BF16), wd_ref[cols, :].astype(BF16))
    return _layer_norm(DEEPNORM_ALPHA * x + 0.5 * acc, g, b)


def _with_resident_ffn_weights(layer, w_hbm, w_vmem, sem, compute):
    wg_hbm, wu_hbm, wd_hbm = w_hbm
    wg_v, wu_v, wd_v = w_vmem
    copies = []
    for c in range(D_FF // TF):
        cols = pl.ds(c * TF, TF)
        copies.append((pltpu.make_async_copy(wg_hbm.at[layer, :, cols], wg_v.at[:, cols], sem.at[0, c]),
                       pltpu.make_async_copy(wu_hbm.at[layer, :, cols], wu_v.at[:, cols], sem.at[1, c]),
                       pltpu.make_async_copy(wd_hbm.at[layer, cols, :], wd_v.at[cols, :], sem.at[2, c])))
    first = pl.program_id(0) == 0

    @pl.when(first)
    def _():
        for chunk in copies:
            for copy in chunk:
                copy.start()

        def wait_chunk(c):
            for copy in copies[c]:
                copy.wait()

        compute(wait_chunk)

    @pl.when(jnp.logical_not(first))
    def _():
        compute(None)


_FFN_WEIGHT_SCRATCH = [pltpu.VMEM((D_MODEL, D_FF), F32), pltpu.VMEM((D_MODEL, D_FF), F32),
                       pltpu.VMEM((D_FF, D_MODEL), F32), pltpu.SemaphoreType.DMA((3, D_FF // TF))]


def _ffn_ln_kernel(layer, x_ref, wg_hbm, wu_hbm, wd_hbm, g_ref, b_ref, o_ref, wg_v, wu_v, wd_v, sem):
    def compute(before_chunk):
        o_ref[...] = _ffn_ln_body(x_ref[...], wg_v, wu_v, wd_v, g_ref[...], b_ref[...], before_chunk)

    _with_resident_ffn_weights(layer, (wg_hbm, wu_hbm, wd_hbm), (wg_v, wu_v, wd_v), sem, compute)


def _ffn_ln(x, wg, wu, wd, g, b, layer):
    t = x.shape[0]
    full = lambda shape: pl.BlockSpec(shape, lambda i: (0, 0))
    hbm = pl.BlockSpec(memory_space=pl.ANY)
    return pl.pallas_call(
        functools.partial(_ffn_ln_kernel, layer),
        grid=(t // TM_FFN,),
        in_specs=[pl.BlockSpec((TM_FFN, D_MODEL), lambda i: (i, 0)), hbm, hbm, hbm,
                  full((1, D_MODEL)), full((1, D_MODEL))],
        out_specs=pl.BlockSpec((TM_FFN, D_MODEL), lambda i: (i, 0)),
        out_shape=jax.ShapeDtypeStruct((t, D_MODEL), F32),
        scratch_shapes=list(_FFN_WEIGHT_SCRATCH),
        compiler_params=_cparams(1),
        name="ffn_ln",
    )(x, wg, wu, wd, g, b)


IN_SPLITS = (D_RET, D_RET, D_RET, D_RET, D_DSA, D_DSA, D_DSA,
             IDX_HEADS * IDX_DIM, IDX_DIM, IDX_HEADS, D_DIFF, D_DIFF, D_DIFF)
N_IN = sum(IN_SPLITS)
N_IN_PAD = -(-N_IN // LANES) * LANES
_PROJ_OUT = (("rq", D_HEADS_PAD, F32, None, "tok"), ("rk", D_HEADS_PAD, F32, None, "tok"),
             ("rv", D_HEADS_PAD, BF16, None, "tok"), ("rg", D_HEADS_PAD, F32, None, "tok"),
             ("aq", D_HEADS_PAD, BF16, HEAD_DIM ** -0.5 * LOG2E, "T"), ("ak", D_HEADS_PAD, BF16, None, "tok"),
             ("av", DSA_HEADS, BF16, None, "VT"),
             ("iq", IDX_HEADS * IDX_DIM, BF16, None, "T"), ("ik", LANES, BF16, None, "tok"),
             ("iw", LANES, F32, None, "T"),
             ("dq", D_HEADS_PAD, BF16, DIFF_QK_DIM ** -0.5 * LOG2E, "T"), ("dk", D_HEADS_PAD, BF16, None, "tok"),
             ("dv", DIFF_HEADS, BF16, None, "VT"))
_IW_ROWS = SUBLANES


def _proj_kernel(x_ref, w_ref, *refs):
    o_refs, h_scr = refs[:-1], refs[-1]
    xb = x_ref[...].astype(BF16)
    for lo in range(0, N_IN, TN_PROJ):
        cols = pl.ds(lo, min(TN_PROJ, N_IN - lo))
        h_scr[:, cols] = _dot(xb, w_ref[:, cols].astype(BF16))
    ones = jnp.ones((BF16_ROWS, TK), BF16)
    off = 0
    for width, (name, out_width, dtype, scale, kind), o_ref in zip(IN_SPLITS, _PROJ_OUT, o_refs):
        base = off // LANES * LANES
        end = min(-(-(off + width) // LANES) * LANES, N_IN_PAD)
        seg = h_scr[:, pl.ds(base, end - base)][:, off - base:off - base + width]
        off += width
        if scale is not None:
            seg = seg * scale
        padded = -(-width // LANES) * LANES
        if name == "ik":
            seg = jnp.concatenate([seg, seg], axis=1)
        elif padded > width:
            seg = jnp.concatenate([seg, jnp.zeros((TM_PROJ, padded - width), F32)], axis=1)
        if kind == "tok":
            o_ref[...] = seg.astype(dtype)
        elif kind == "T":
            o_ref[...] = seg.T[:o_ref.shape[0]].astype(dtype)
        else:
            seg_t = seg.T.astype(dtype)
            for t in range(TM_PROJ // TK):
                for h in range(out_width):
                    o_ref[t, pl.ds(h * VT_ROWS, HEAD_DIM), :] = seg_t[h * HEAD_DIM:(h + 1) * HEAD_DIM,
                                                                      t * TK:(t + 1) * TK]
                    o_ref[t, pl.ds(h * VT_ROWS + HEAD_DIM, BF16_ROWS), :] = ones


def _proj(x, w_in, layer):
    t = x.shape[0]
    out_specs, out_shape = [], []
    for name, w, dt, _, kind in _PROJ_OUT:
        if kind == "tok":
            out_specs.append(pl.BlockSpec((TM_PROJ, w), lambda i: (i, 0)))
            out_shape.append(jax.ShapeDtypeStruct((t, w), dt))
        elif kind == "T":
            rows = _IW_ROWS if name == "iw" else w
            out_specs.append(pl.BlockSpec((rows, TM_PROJ), lambda i: (0, i)))
            out_shape.append(jax.ShapeDtypeStruct((rows, t), dt))
        else:
            out_specs.append(pl.BlockSpec((TM_PROJ // TK, w * VT_ROWS, TK), lambda i: (i, 0, 0)))
            out_shape.append(jax.ShapeDtypeStruct((t // TK, w * VT_ROWS, TK), dt))
    return pl.pallas_call(
        _proj_kernel,
        grid=(t // TM_PROJ,),
        in_specs=[pl.BlockSpec((TM_PROJ, D_MODEL), lambda i: (i, 0)),
                  pl.BlockSpec((None, D_MODEL, N_IN), lambda i: (layer, 0, 0), pipeline_mode=pl.Buffered(1))],
        out_specs=out_specs,
        out_shape=out_shape,
        scratch_shapes=[pltpu.VMEM((TM_PROJ, N_IN_PAD), F32)],
        compiler_params=_cparams(1),
        name="in_proj",
    )(x, w_in)


def _swap_halves(x):
    lane = lax.broadcasted_iota(jnp.int32, x.shape, 1)
    first_half = (lane % HEAD_DIM) < (HEAD_DIM // 2)
    from_right = pltpu.roll(x, LANES - HEAD_DIM // 2, 1)
    from_left = pltpu.roll(x, HEAD_DIM // 2, 1)
    return jnp.where(first_half, from_right, from_left)


def _retention_kernel(q_ref, k_ref, v_ref, g_ref, cos_ref, sin_ref, intra_ref, qdec_ref, kdec_ref,
                      sdec_ref, same_ref, avg_ref, o_ref, state_ref):
    @pl.when(pl.program_id(1) == 0)
    def _():
        state_ref[...] = jnp.zeros(state_ref.shape, F32)

    avg = avg_ref[...]
    group = _lane_group((TR, LANES))
    for p in range(RET_HEADS // PAIR):
        lanes = pl.ds(p * LANES, LANES)
        for blk in range(RET_BLOCKS):
            rows = pl.ds(blk * TR, TR)
            cos = cos_ref[rows, :]
            sin = sin_ref[rows, :]
            q = q_ref[0, rows, lanes]
            k = k_ref[0, rows, lanes]
            q = (q * cos + _swap_halves(q) * sin) * (HEAD_DIM ** -0.5)
            k = k * cos + _swap_halves(k) * sin
            v = v_ref[0, rows, lanes]
            kb = k.astype(BF16)
            intra = jnp.zeros((TR, LANES), F32)
            for j in range(PAIR):
                qh = jnp.where(group == j, q, 0.0).astype(BF16)
                scores = _dot_nt(qh, kb) * intra_ref[PAIR * p + j]
                oh = _dot(scores.astype(BF16), v)
                intra = jnp.where(group == j, oh, intra)
            state = state_ref[p]
            cross = _dot((q * qdec_ref[:, lanes]).astype(BF16), state.astype(BF16))
            o = intra + cross
            kd_t = (k * kdec_ref[:, lanes]).T.astype(BF16)
            state_ref[p] = state * sdec_ref[p] + _dot(kd_t, v) * same_ref[...]
            mu = _group_mean(o, avg)
            d = o - mu
            var = _group_mean(d * d, avg)
            normed = d * lax.rsqrt(var + HEAD_NORM_EPS)
            gate = g_ref[0, rows, lanes]
            o_ref[0, rows, lanes] = (gate * jax.nn.sigmoid(gate) * normed).astype(o_ref.dtype)


def _retention(rq, rk, rv, rg, batch, seq):
    cos, sin = _rotary_tables(seq)
    intra, qdec, kdec, sdec, same = _retention_tables()
    avg = jnp.asarray(_group_avg(), BF16)
    step = RET_BLOCKS * TR
    r3 = lambda a: a.reshape(batch, seq, D_HEADS_PAD)
    tok = pl.BlockSpec((1, step, D_HEADS_PAD), lambda b, n: (b, n, 0))
    const = lambda a: pl.BlockSpec(a.shape, lambda b, n: (0,) * a.ndim)
    pos = pl.BlockSpec((step, LANES), lambda b, n: (n, 0))
    out = pl.pallas_call(
        _retention_kernel,
        grid=(batch, seq // step),
        in_specs=[tok, tok, tok, tok, pos, pos, const(intra), const(qdec), const(kdec),
                  const(sdec), const(same), const(avg)],
        out_specs=tok,
        out_shape=jax.ShapeDtypeStruct((batch, seq, D_HEADS_PAD), BF16),
        scratch_shapes=[pltpu.VMEM((RET_HEADS // PAIR, LANES, LANES), F32)],
        compiler_params=_cparams(2),
        name="retention",
    )(r3(rq), r3(rk), r3(rv), r3(rg), cos, sin, intra, qdec, kdec, sdec, same, avg)
    return out.reshape(batch * seq, D_HEADS_PAD)


def _build_band(bidx_ref, tab_ref, band_ref, n_heads):
    for h in range(n_heads):
        far = tab_ref[h, FAR_BUCKET]
        for t in range(2):
            idx = bidx_ref[t]
            val = jnp.full((TK, TQ), NEG, F32)
            for bkt in range(N_BUCKETS):
                val = jnp.where(idx == bkt, (tab_ref[h, bkt] - far) * LOG2E, val)
            band_ref[h, t] = val


def _load_queries(qt_ref, q_scr, streams):
    row = lax.broadcasted_iota(jnp.int32, (LANES, TQ), 0)
    for i, (pair, _, lo, hi) in enumerate(streams):
        q_scr[i] = jnp.where((row >= lo) & (row < hi), qt_ref[pl.ds(pair * LANES, LANES), :], 0)


def _for_tiles(n, fn):
    def quad(j, carry):
        for u in range(4):
            fn(4 * j + u)
        return carry

    lax.fori_loop(0, lax.shift_right_logical(n, 2), quad, 0)
    rest = lax.rem(n, 4)

    @pl.when(rest >= 2)
    def _():
        fn(n - rest)
        fn(n - rest + 1)

    @pl.when(lax.rem(rest, 2) == 1)
    def _():
        fn(n - 1)


def _softmax_passes(qb, k_ref, vt_ref, q_scr, band_ref, mask_fn, logit_ref, m_ref, acc_ref, streams):
    m_ref[...] = jnp.full(m_ref.shape, NEG, F32)

    def logits_tile(kt, band_t):
        rows = pl.ds(pl.multiple_of(kt * TK, TK), TK)
        mask = mask_fn(kt)
        for i, (pair, head, _, _) in enumerate(streams):
            lg = _dot(k_ref[0, rows, pl.ds(pair * LANES, LANES)], q_scr[i])
            if band_t is not None:
                lg = lg + band_ref[head, band_t]
            if mask is not None:
                lg = lg + mask
            logit_ref[kt, i] = lg
            m_ref[i] = jnp.maximum(m_ref[i], jnp.max(lg.reshape(TK // SUBLANES, SUBLANES, TQ), axis=0))

    _for_tiles(jnp.maximum(qb - 1, 0), lambda kt: logits_tile(kt, None))

    @pl.when(qb >= 1)
    def _():
        logits_tile(qb - 1, 0)
        logits_tile(qb, 1)

    @pl.when(qb == 0)
    def _():
        logits_tile(qb, 1)

    m_fin = [jnp.max(m_ref[i], axis=0, keepdims=True) for i in range(len(streams))]
    acc_ref[...] = jnp.zeros(acc_ref.shape, F32)

    def values_tile(kt):
        for i, (_, head, _, _) in enumerate(streams):
            p = jnp.exp2(logit_ref[kt, i] - m_fin[i])
            acc_ref[i] += _dot(vt_ref[0, kt, pl.ds(head * VT_ROWS, VT_ROWS), :], p.astype(BF16))

    _for_tiles(qb + 1, values_tile)


def _normalised(acc_ref, i):
    a = acc_ref[i]
    return a[:HEAD_DIM] / a[HEAD_DIM:HEAD_DIM + 1]


_DSA_STREAMS = tuple((h // PAIR, h, (h % PAIR) * HEAD_DIM, (h % PAIR + 1) * HEAD_DIM) for h in range(DSA_HEADS))


def _f32_to_key(x):
    bits = lax.bitcast_convert_type(x, jnp.int32)
    return jnp.where(bits < 0, ~bits, bits ^ jnp.int32(-2 ** 31))


def _key_to_f32(key):
    bits = jnp.where(key < 0, key ^ jnp.int32(-2 ** 31), ~key)
    return lax.bitcast_convert_type(bits, F32)


def _dsa_kernel(tab_ref, qt_ref, k_ref, vt_ref, iqt_ref, ik_ref, iwt_ref, bidx_ref, tril_ref,
                o_ref, band_ref, score_ref, q_scr, logit_ref, m_ref, acc_ref, tiecnt_ref, tie_ref, stat_ref):
    qb = pl.program_id(1)
    n_tiles = qb + 1

    @pl.when((pl.program_id(0) == 0) & (qb == 0))
    def _():
        _build_band(bidx_ref, tab_ref, band_ref, DSA_HEADS)

    iw = iwt_ref[...] * (IDX_HEADS ** -0.5 * IDX_DIM ** -0.5)
    row = lax.broadcasted_iota(jnp.int32, (LANES, TQ), 0)
    iq = [jnp.where(row // HEAD_DIM == (h % PAIR), iqt_ref[pl.ds((h // PAIR) * LANES, LANES), :], 0)
          for h in range(IDX_HEADS)]

    def score_tile(kt):
        ki = ik_ref[0, pl.ds(pl.multiple_of(kt * TK, TK), TK), :]
        s = jnp.zeros((TK, TQ), F32)
        for h in range(IDX_HEADS):
            s = s + jnp.maximum(_dot(ki, iq[h]), 0.0) * iw[h:h + 1, :]
        return s

    stat_ref[0] = jnp.full((SUBLANES, TQ), -jnp.inf, F32)
    stat_ref[1] = jnp.full((SUBLANES, TQ), jnp.inf, F32)
    stat_ref[2] = jnp.zeros((SUBLANES, TQ), F32)
    stat_ref[3] = jnp.zeros((SUBLANES, TQ), F32)

    def add_stats(s_for_max, s_for_min):
        hi_part = s_for_max.reshape(TK // SUBLANES, SUBLANES, TQ)
        stat_ref[0] = jnp.maximum(stat_ref[0], jnp.max(hi_part, axis=0))
        stat_ref[1] = jnp.minimum(stat_ref[1], jnp.min(s_for_min.reshape(TK // SUBLANES, SUBLANES, TQ), axis=0))
        stat_ref[2] += jnp.sum(jnp.where(hi_part > 0.0, 1.0, 0.0), axis=0)
        stat_ref[3] += jnp.sum(jnp.where(hi_part >= 0.0, 1.0, 0.0), axis=0)

    def past_scores(kt):
        s = score_tile(kt)
        score_ref[kt] = s
        add_stats(s, s)

    _for_tiles(qb, past_scores)
    s_diag = score_tile(qb)
    visible = band_ref[0, 1] > 0.5 * NEG
    score_ref[qb] = jnp.where(visible, s_diag, -jnp.inf)
    add_stats(jnp.where(visible, s_diag, -jnp.inf), jnp.where(visible, s_diag, jnp.inf))

    k_sel = float(DSA_TOPK_MAX)

    def count(pred):
        def body(kt, c):
            xs = score_ref[kt].reshape(TK // SUBLANES, SUBLANES, TQ)
            accs = [c[a] for a in range(N_ACC)]
            for g in range(TK // SUBLANES):
                a = accs[g % N_ACC]
                accs[g % N_ACC] = jnp.where(pred(xs[g]), a + 1.0, a)
            return jnp.stack(accs)
        c = lax.fori_loop(0, n_tiles, body, jnp.zeros((N_ACC, SUBLANES, TQ), F32))
        return jnp.sum(jnp.sum(c, axis=0), axis=0, keepdims=True)

    col = lax.broadcasted_iota(jnp.int32, (1, TQ), 1)
    limit = qb * TQ + (col // CHUNK + 1) * CHUNK
    select_all = limit <= DSA_TOPK_MAX
    s_max = jnp.max(stat_ref[0], axis=0, keepdims=True)
    s_min = jnp.min(stat_ref[1], axis=0, keepdims=True)
    n_pos = jnp.sum(stat_ref[2], axis=0, keepdims=True)
    n_nonneg = jnp.sum(stat_ref[3], axis=0, keepdims=True)

    positive = n_pos > k_sel
    zero_thr = (n_pos <= k_sel) & (k_sel <= n_nonneg)
    lo0 = jnp.where(positive, 0.0, _key_to_f32(_f32_to_key(s_min) - 1))
    hi0 = jnp.where(positive, _key_to_f32(_f32_to_key(s_max) + 1), 0.0)
    flo0 = jnp.where(positive, n_pos, limit.astype(F32)) - k_sel
    fhi0 = jnp.where(positive, 0.0, n_nonneg) - k_sel
    done0 = jnp.where(select_all | zero_thr, 1.0, 0.0)
    thr0 = jnp.where(select_all, -jnp.inf, 0.0)

    def search_step(vec):
        lo, hi, flo, fhi, side, done, thr = vec
        w = flo / (flo - fhi)
        t = lo * (1.0 - w) + hi * w
        t = jnp.where((t > lo) & (t < hi), t, 0.5 * lo + 0.5 * hi)
        splits = (t > lo) & (t < hi)
        f = count(lambda s: s >= t) - k_sel
        active = done == 0.0
        hit = active & splits & (f == 0.0)
        adjacent = active & ~splits
        thr = jnp.where(hit, t, jnp.where(adjacent, lo, thr))
        done = jnp.where(hit | adjacent, 1.0, done)
        up = active & splits & (f > 0.0)
        down = active & splits & (f < 0.0)
        fhi_new = jnp.where(up & (side > 0.0), 0.5 * fhi, jnp.where(down, f, fhi))
        flo_new = jnp.where(down & (side < 0.0), 0.5 * flo, jnp.where(up, f, flo))
        lo = jnp.where(up, t, lo)
        hi = jnp.where(down, t, hi)
        side = jnp.where(up, 1.0, jnp.where(down, -1.0, side))
        return lo, hi, flo_new, fhi_new, side, done, thr

    def open_queries(done):
        return (jnp.min(done) == 0.0).astype(jnp.int32)

    def search_round(state):
        rounds, _, vec = state
        for _ in range(SEARCH_STEPS_PER_CHECK):
            vec = search_step(vec)
        return rounds + 1, open_queries(vec[5]), vec

    vec0 = (lo0, hi0, flo0, fhi0, jnp.zeros((1, TQ), F32), done0, thr0)
    state = lax.while_loop(lambda st: (st[0] < SEARCH_ROUNDS) & (st[1] > 0), search_round,
                           (jnp.int32(0), open_queries(done0), vec0))
    done, thr = state[2][5], state[2][6]

    def bisect(i, prefix):
        cand = prefix | lax.shift_left(jnp.int32(1), 31 - i)
        cand_f = _key_to_f32(cand)
        take = count(lambda s: s >= cand_f) >= k_sel
        return jnp.where(take, cand, prefix)

    def exact_thr():
        prefix = lax.fori_loop(0, 32, bisect, jnp.zeros((1, TQ), jnp.int32))
        return jnp.where(done == 0.0, _key_to_f32(prefix), thr)

    thr = lax.cond(state[1] > 0, exact_thr, lambda: thr)
    n_gt = count(lambda s: s > thr)
    need = jnp.where(select_all, 0.0, k_sel - n_gt)

    def tie_counts(kt, carry):
        tied = jnp.where(score_ref[kt] == thr, 1.0, 0.0).reshape(TK // SUBLANES, SUBLANES, TQ)
        tiecnt_ref[kt] = jnp.sum(jnp.sum(tied, axis=0), axis=0, keepdims=True)
        return carry

    lax.fori_loop(0, n_tiles, tie_counts, 0)
    tie_ref[...] = jnp.zeros(tie_ref.shape, F32)
    tril = tril_ref[...]

    def selection_mask(kt):
        s = score_ref[kt]
        tied = s == thr
        tied_before = tie_ref[...]
        tie_ref[...] = tied_before + tiecnt_ref[kt]
        tied_upto = tied_before + _dot(tril, jnp.where(tied, 1.0, 0.0).astype(BF16))
        return jnp.where((s > thr) | (tied & (tied_upto <= need)), 0.0, NEG)

    _load_queries(qt_ref, q_scr, _DSA_STREAMS)
    _softmax_passes(qb, k_ref, vt_ref, q_scr, band_ref, selection_mask, logit_ref, m_ref, acc_ref, _DSA_STREAMS)

    for p in range(N_PAIRS):
        halves = [_normalised(acc_ref, PAIR * p + j) if PAIR * p + j < DSA_HEADS
                  else jnp.zeros((HEAD_DIM, TQ), F32) for j in range(PAIR)]
        o_ref[0, :, pl.ds(p * LANES, LANES)] = jnp.concatenate(halves, axis=0).T.astype(o_ref.dtype)


def _dsa(aqT, ak, avT, iqT, ik, iwT, bias_t, batch, seq):
    bidx = jnp.asarray(_band_bucket_index_t())
    tril = jnp.asarray(_prefix_tril(), BF16)
    nq = seq // TQ
    n_streams = len(_DSA_STREAMS)
    qcols = lambda w: pl.BlockSpec((w, TQ), lambda b, n: (0, b * nq + n))
    ktok = lambda w: pl.BlockSpec((1, seq, w), lambda b, n: (b, 0, 0))
    const = lambda a: pl.BlockSpec(a.shape, lambda b, n: (0,) * a.ndim)
    out = pl.pallas_call(
        _dsa_kernel,
        grid=(batch, nq),
        in_specs=[pl.BlockSpec(memory_space=pltpu.SMEM),
                  qcols(D_HEADS_PAD), ktok(D_HEADS_PAD),
                  pl.BlockSpec((1, seq // TK, DSA_HEADS * VT_ROWS, TK), lambda b, n: (b, 0, 0, 0)),
                  qcols(IDX_HEADS * IDX_DIM), ktok(LANES), qcols(_IW_ROWS), const(bidx), const(tril)],
        out_specs=pl.BlockSpec((1, TQ, D_HEADS_PAD), lambda b, n: (b, n, 0)),
        out_shape=jax.ShapeDtypeStruct((batch, seq, D_HEADS_PAD), BF16),
        scratch_shapes=[pltpu.VMEM((DSA_HEADS, 2, TK, TQ), F32),
                        pltpu.VMEM((seq // TK, TK, TQ), F32),
                        pltpu.VMEM((n_streams, LANES, TQ), BF16),
                        pltpu.VMEM((seq // TK, n_streams, TK, TQ), F32),
                        pltpu.VMEM((n_streams, SUBLANES, TQ), F32),
                        pltpu.VMEM((n_streams, VT_ROWS, TQ), F32),
                        pltpu.VMEM((seq // TK, 1, TQ), F32),
                        pltpu.VMEM((1, TQ), F32),
                        pltpu.VMEM((4, SUBLANES, TQ), F32)],
        compiler_params=_cparams(2),
        name="dsa_attention",
    )(bias_t, aqT, ak.reshape(batch, seq, D_HEADS_PAD),
      avT.reshape(batch, seq // TK, DSA_HEADS * VT_ROWS, TK), iqT, ik.reshape(batch, seq, LANES), iwT, bidx, tril)
    return out.reshape(batch * seq, D_HEADS_PAD)


_DIFF_STREAMS = tuple(
    (h // PAIR, h, (h % PAIR) * HEAD_DIM + mm * DIFF_QK_DIM, (h % PAIR) * HEAD_DIM + (mm + 1) * DIFF_QK_DIM)
    for h in range(DIFF_HEADS) for mm in range(2))


def _diff_kernel(lambda_init, tab_ref, qt_ref, k_ref, vt_ref, lam_ref, g_ref, bidx_ref, avg_ref,
                 o_ref, band_ref, q_scr, logit_ref, m_ref, acc_ref):
    qb = pl.program_id(1)

    @pl.when((pl.program_id(0) == 0) & (qb == 0))
    def _():
        _build_band(bidx_ref, tab_ref, band_ref, DIFF_HEADS)

    lv = lam_ref[...]
    lam = (jnp.exp(jnp.sum(lv[0:1] * lv[1:2], axis=-1, keepdims=True))
           - jnp.exp(jnp.sum(lv[2:3] * lv[3:4], axis=-1, keepdims=True)) + lambda_init)

    _load_queries(qt_ref, q_scr, _DIFF_STREAMS)
    _softmax_passes(qb, k_ref, vt_ref, q_scr, band_ref, lambda kt: None, logit_ref, m_ref, acc_ref,
                    _DIFF_STREAMS)

    avg = avg_ref[...]
    for p in range(N_PAIRS):
        halves = []
        for j in range(PAIR):
            h = PAIR * p + j
            if h < DIFF_HEADS:
                halves.append(_normalised(acc_ref, 2 * h) - lam * _normalised(acc_ref, 2 * h + 1))
            else:
                halves.append(jnp.zeros((HEAD_DIM, TQ), F32))
        out = jnp.concatenate(halves, axis=0).T
        ms = _group_mean(out * out, avg)
        out = out * lax.rsqrt(ms + LN_EPS) * g_ref[...] * (1.0 - lambda_init)
        o_ref[0, :, pl.ds(p * LANES, LANES)] = out.astype(o_ref.dtype)


def _diff(dqT, dk, dvT, lam_vecs, subln_g, bias_t, lambda_init, batch, seq):
    bidx = jnp.asarray(_band_bucket_index_t())
    avg = jnp.asarray(_group_avg(), BF16)
    g2 = jnp.concatenate([subln_g, subln_g]).reshape(1, LANES)
    nq = seq // TQ
    n_streams = len(_DIFF_STREAMS)
    const = lambda a: pl.BlockSpec(a.shape, lambda b, n: (0,) * a.ndim)
    out = pl.pallas_call(
        functools.partial(_diff_kernel, lambda_init),
        grid=(batch, nq),
        in_specs=[pl.BlockSpec(memory_space=pltpu.SMEM),
                  pl.BlockSpec((D_HEADS_PAD, TQ), lambda b, n: (0, b * nq + n)),
                  pl.BlockSpec((1, seq, D_HEADS_PAD), lambda b, n: (b, 0, 0)),
                  pl.BlockSpec((1, seq // TK, DIFF_HEADS * VT_ROWS, TK), lambda b, n: (b, 0, 0, 0)),
                  const(lam_vecs), const(g2), const(bidx), const(avg)],
        out_specs=pl.BlockSpec((1, TQ, D_HEADS_PAD), lambda b, n: (b, n, 0)),
        out_shape=jax.ShapeDtypeStruct((batch, seq, D_HEADS_PAD), BF16),
        scratch_shapes=[pltpu.VMEM((DIFF_HEADS, 2, TK, TQ), F32),
                        pltpu.VMEM((n_streams, LANES, TQ), BF16),
                        pltpu.VMEM((seq // TK, n_streams, TK, TQ), F32),
                        pltpu.VMEM((n_streams, SUBLANES, TQ), F32),
                        pltpu.VMEM((n_streams, VT_ROWS, TQ), F32)],
        compiler_params=_cparams(2),
        name="diff_attention",
    )(bias_t, dqT, dk.reshape(batch, seq, D_HEADS_PAD),
      dvT.reshape(batch, seq // TK, DIFF_HEADS * VT_ROWS, TK), lam_vecs, g2, bidx, avg)
    return out.reshape(batch * seq, D_HEADS_PAD)


def _out_ffn_kernel(x_ref, ret_ref, dsa_ref, dif_ref, wo_ref, g1_ref, b1_ref, wg_ref, wu_ref, wd_ref,
                    g2_ref, b2_ref, o_ref):
    mixed = _dot(jnp.concatenate([ret_ref[...], dsa_ref[...], dif_ref[...]], axis=1), wo_ref[...])
    x = _layer_norm(DEEPNORM_ALPHA * x_ref[...] + mixed, g1_ref[...], b1_ref[...])
    o_ref[...] = _ffn_ln_body(x, wg_ref, wu_ref, wd_ref, g2_ref[...], b2_ref[...])


def _pad_w_out(w_out):
    zeros = jnp.zeros((D_HEADS_PAD - D_DSA, w_out.shape[1]), w_out.dtype)
    return jnp.concatenate([w_out[:D_RET], w_out[D_RET:D_RET + D_DSA], zeros,
                            w_out[D_RET + D_DSA:], zeros], axis=0)


def _out_ffn(x, ret, dsa, dif, w_pad, g1, b1, wg, wu, wd, g2, b2, layer):
    t = x.shape[0]
    stacked = lambda shape: pl.BlockSpec((None,) + shape, lambda i: (layer, 0, 0), pipeline_mode=pl.Buffered(1))
    const = lambda shape: pl.BlockSpec(shape, lambda i: (0, 0), pipeline_mode=pl.Buffered(1))
    tok = lambda w: pl.BlockSpec((TM_FFN, w), lambda i: (i, 0))
    return pl.pallas_call(
        _out_ffn_kernel,
        grid=(t // TM_FFN,),
        in_specs=[tok(D_MODEL), tok(D_HEADS_PAD), tok(D_HEADS_PAD), tok(D_HEADS_PAD),
                  const((3 * D_HEADS_PAD, D_MODEL)), const((1, D_MODEL)), const((1, D_MODEL)),
                  stacked((D_MODEL, D_FF)), stacked((D_MODEL, D_FF)), stacked((D_FF, D_MODEL)),
                  const((1, D_MODEL)), const((1, D_MODEL))],
        out_specs=tok(D_MODEL),
        out_shape=jax.ShapeDtypeStruct((t, D_MODEL), F32),
        compiler_params=_cparams(1),
        name="out_proj_ffn_ln",
    )(x, ret, dsa, dif, w_pad, g1, b1, wg, wu, wd, g2, b2)


def kernel(x, w_in, w_out, ffn1_wg, ffn1_wu, ffn1_wd, ffn2_wg, ffn2_wu, ffn2_wd,
           ln_g, ln_b, diff_lambda, diff_subln_g, rel_bias):
    batch, seq, d = x.shape
    assert d == D_MODEL and seq % TQ == 0 and seq % (RET_BLOCKS * TR) == 0 and (batch * seq) % TM_FFN == 0
    assert min(DSA_TOPK_MAX, seq // 4) == DSA_TOPK_MAX
    h = x.reshape(batch * seq, d)
    bias_t = rel_bias.T
    row = lambda a: a.reshape(1, D_MODEL)
    for l in range(DEPTH):
        lambda_init = 0.8 - 0.6 * math.exp(-0.3 * l)
        h = _ffn_ln(h, ffn1_wg, ffn1_wu, ffn1_wd, row(ln_g[l, 0]), row(ln_b[l, 0]), l)
        (rq, rk, rv, rg, aqT, ak, avT, iqT, ik, iwT, dqT, dk, dvT) = _proj(h, w_in, l)
        ret = _retention(rq, rk, rv, rg, batch, seq)
        dsa = _dsa(aqT, ak, avT, iqT, ik, iwT, bias_t[:DSA_HEADS], batch, seq)
        dif = _diff(dqT, dk, dvT, diff_lambda[l], diff_subln_g[l], bias_t[DSA_HEADS:], lambda_init, batch, seq)
        h = _out_ffn(h, ret, dsa, dif, _pad_w_out(w_out[l]).astype(BF16), row(ln_g[l, 1]), row(ln_b[l, 1]),
                     ffn2_wg, ffn2_wu, ffn2_wd, row(ln_g[l, 2]), row(ln_b[l, 2]), l)
    return h.reshape(batch, seq, d)
```

```python
import functools
import math

import numpy as np
import jax
import jax.numpy as jnp
from jax import lax
from jax.experimental import pallas as pl
from jax.experimental.pallas import tpu as pltpu

D_MODEL = 1024
DEPTH = 2
CHUNK = 64
HEAD_DIM = 64
RET_HEADS = 6
DSA_HEADS = 5
IDX_HEADS = 4
IDX_DIM = 64
DSA_TOPK_MAX = 256
DIFF_HEADS = 5
DIFF_QK_DIM = HEAD_DIM // 2
D_RET = RET_HEADS * HEAD_DIM
D_DSA = DSA_HEADS * HEAD_DIM
D_DIFF = DIFF_HEADS * HEAD_DIM
D_FF = 2816
N_BUCKETS = 32
MAX_DISTANCE = 128
ROPE_BASE = 10000.0
LN_EPS = 1e-5
HEAD_NORM_EPS = 1e-6
DEEPNORM_ALPHA = (2 * DEPTH) ** 0.25

LANES = 128
SUBLANES = 8
BF16_ROWS = 16
PAIR = LANES // HEAD_DIM
VMEM_LIMIT_BYTES = 56 * 1024 * 1024
VMEM_LIMIT_FUSED_FFN_BYTES = 58 * 1024 * 1024

D_HEADS_PAD = 3 * LANES
N_PAIRS = D_HEADS_PAD // LANES
VT_ROWS = HEAD_DIM + BF16_ROWS
TQ = 256
TK = 256
TR = 256
RET_BLOCKS = 4
TM_FFN = 512
TF = 256
TM_PROJ = 512
TN_PROJ = 256
N_ACC = 4
SEARCH_STEPS_PER_CHECK = 4
SEARCH_ROUNDS = 6
NEG = -1e30
LOG2E = math.log2(math.e)

F32 = jnp.float32
BF16 = jnp.bfloat16


def _dot(a, b):
    return jnp.dot(a, b, preferred_element_type=F32)


def _dot_nt(a, b):
    return lax.dot_general(a, b, (((1,), (1,)), ((), ())), preferred_element_type=F32)


def _layer_norm(r, g, b):
    mu = jnp.mean(r, axis=-1, keepdims=True)
    d = r - mu
    var = jnp.mean(d * d, axis=-1, keepdims=True)
    return d * lax.rsqrt(var + LN_EPS) * g + b


def _group_mean(x, avg_bf16):
    hi = x.astype(BF16)
    lo = (x - hi.astype(F32)).astype(BF16)
    return _dot(hi, avg_bf16) + _dot(lo, avg_bf16)


def _cparams(n_axes, vmem_limit_bytes=VMEM_LIMIT_BYTES):
    return pltpu.CompilerParams(
        dimension_semantics=("arbitrary",) * n_axes,
        vmem_limit_bytes=vmem_limit_bytes,
    )


def _rel_bucket_static(rel):
    rel = np.asarray(rel, dtype=np.int64)
    half = N_BUCKETS // 2
    max_exact = half // 2
    n = np.abs(rel)
    large = np.full(n.shape, max_exact, dtype=np.int64)
    for k in range(1, 64):
        large = np.where(64 * (2 ** k) <= n * n, max_exact + k, large)
    large = np.minimum(large, half - 1)
    return (np.where(rel > 0, half, 0) + np.where(n < max_exact, n, large)).astype(np.int32)


FAR_BUCKET = int(_rel_bucket_static(-(TK + 1)))


@functools.lru_cache(maxsize=None)
def _band_bucket_index_t():
    j = np.arange(TK)[:, None]
    i = np.arange(TQ)[None, :]
    prev = _rel_bucket_static(j - TK - i)
    diag = _rel_bucket_static(j - i)
    visible = j < (i // CHUNK + 1) * CHUNK
    diag = np.where(visible, diag, -1)
    return np.stack([prev, diag]).astype(np.int32)


@functools.lru_cache(maxsize=None)
def _rotary_tables(seq):
    lane = np.arange(LANES)
    d = lane % HEAD_DIM
    f = d % (HEAD_DIM // 2)
    inv = ROPE_BASE ** (-(2.0 * f) / HEAD_DIM)
    ang = np.arange(seq, dtype=np.float64)[:, None] * inv[None, :]
    cos = np.cos(ang)
    sin = np.sin(ang) * np.where(d < HEAD_DIM // 2, -1.0, 1.0)[None, :]
    return cos.astype(np.float32), sin.astype(np.float32)


@functools.lru_cache(maxsize=None)
def _retention_tables():
    gamma = 1.0 - 2.0 ** (-5.0 - np.arange(RET_HEADS, dtype=np.float64))
    t = np.arange(TR)
    dist = np.abs(t[:, None] - t[None, :]).astype(np.float64)
    visible = t[None, :] < (t[:, None] // CHUNK + 1) * CHUNK
    intra = np.stack([np.where(visible, g ** dist, 0.0) for g in gamma])
    lane_head = np.arange(D_HEADS_PAD) // HEAD_DIM
    g_lane = gamma[lane_head]
    qdec = g_lane[None, :] ** (t[:, None] + 1.0)
    kdec = g_lane[None, :] ** (TR - 1.0 - t[:, None])
    a = np.arange(LANES)
    same = (a[:, None] // HEAD_DIM) == (a[None, :] // HEAD_DIM)
    sdec = np.stack([np.where(same, (gamma[PAIR * p + a // HEAD_DIM] ** TR)[:, None], 0.0)
                     for p in range(RET_HEADS // PAIR)])
    return (intra.astype(np.float32), qdec.astype(np.float32), kdec.astype(np.float32),
            sdec.astype(np.float32), same.astype(np.float32))


@functools.lru_cache(maxsize=None)
def _group_avg():
    a = np.arange(LANES)
    same = (a[:, None] // HEAD_DIM) == (a[None, :] // HEAD_DIM)
    return np.where(same, 1.0 / HEAD_DIM, 0.0).astype(np.float32)


@functools.lru_cache(maxsize=None)
def _prefix_tril():
    a = np.arange(TK)
    return (a[None, :] <= a[:, None]).astype(np.float32)


def _lane_group(shape):
    return lax.broadcasted_iota(jnp.int32, shape, len(shape) - 1) // HEAD_DIM


def _ffn_ln_body(x, wg_ref, wu_ref, wd_ref, g, b, before_chunk=None):
    xb = x.astype(BF16)
    acc = jnp.zeros(x.shape, F32)
    for c in range(D_FF // TF):
        if before_chunk is not None:
            before_chunk(c)
        cols = pl.ds(c * TF, TF)
        gate = _dot(xb, wg_ref[:, cols].astype(BF16))
        up = _dot(xb, wu_ref[:, cols].astype(BF16))
        h = gate * jax.nn.sigmoid(gate) * up
        acc = acc + _dot(h.astype(BF16), wd_ref[cols, :].astype(BF16))
    return _layer_norm(DEEPNORM_ALPHA * x + 0.5 * acc, g, b)


def _with_resident_ffn_weights(layer, w_hbm, w_vmem, sem, compute):
    wg_hbm, wu_hbm, wd_hbm = w_hbm
    wg_v, wu_v, wd_v = w_vmem
    copies = []
    for c in range(D_FF // TF):
        cols = pl.ds(c * TF, TF)
        copies.append((pltpu.make_async_copy(wg_hbm.at[layer, :, cols], wg_v.at[:, cols], sem.at[0, c]),
                       pltpu.make_async_copy(wu_hbm.at[layer, :, cols], wu_v.at[:, cols], sem.at[1, c]),
                       pltpu.make_async_copy(wd_hbm.at[layer, cols, :], wd_v.at[cols, :], sem.at[2, c])))
    first = pl.program_id(0) == 0

    @pl.when(first)
    def _():
        for chunk in copies:
            for copy in chunk:
                copy.start()

        def wait_chunk(c):
            for copy in copies[c]:
                copy.wait()

        compute(wait_chunk)

    @pl.when(jnp.logical_not(first))
    def _():
        compute(None)


_FFN_WEIGHT_SCRATCH = [pltpu.VMEM((D_MODEL, D_FF), F32), pltpu.VMEM((D_MODEL, D_FF), F32),
                       pltpu.VMEM((D_FF, D_MODEL), F32), pltpu.SemaphoreType.DMA((3, D_FF // TF))]


def _ffn_ln_kernel(layer, x_ref, wg_hbm, wu_hbm, wd_hbm, g_ref, b_ref, o_ref, wg_v, wu_v, wd_v, sem):
    def compute(before_chunk):
        o_ref[...] = _ffn_ln_body(x_ref[...], wg_v, wu_v, wd_v, g_ref[...], b_ref[...], before_chunk)

    _with_resident_ffn_weights(layer, (wg_hbm, wu_hbm, wd_hbm), (wg_v, wu_v, wd_v), sem, compute)


def _ffn_ln(x, wg, wu, wd, g, b, layer):
    t = x.shape[0]
    full = lambda shape: pl.BlockSpec(shape, lambda i: (0, 0))
    hbm = pl.BlockSpec(memory_space=pl.ANY)
    return pl.pallas_call(
        functools.partial(_ffn_ln_kernel, layer),
        grid=(t // TM_FFN,),
        in_specs=[pl.BlockSpec((TM_FFN, D_MODEL), lambda i: (i, 0)), hbm, hbm, hbm,
                  full((1, D_MODEL)), full((1, D_MODEL))],
        out_specs=pl.BlockSpec((TM_FFN, D_MODEL), lambda i: (i, 0)),
        out_shape=jax.ShapeDtypeStruct((t, D_MODEL), F32),
        scratch_shapes=list(_FFN_WEIGHT_SCRATCH),
        compiler_params=_cparams(1),
        name="ffn_ln",
    )(x, wg, wu, wd, g, b)


IN_SPLITS = (D_RET, D_RET, D_RET, D_RET, D_DSA, D_DSA, D_DSA,
             IDX_HEADS * IDX_DIM, IDX_DIM, IDX_HEADS, D_DIFF, D_DIFF, D_DIFF)
N_IN = sum(IN_SPLITS)
N_IN_PAD = -(-N_IN // LANES) * LANES
_PROJ_OUT = (("rq", D_HEADS_PAD, F32, None, "tok"), ("rk", D_HEADS_PAD, F32, None, "tok"),
             ("rv", D_HEADS_PAD, BF16, None, "tok"), ("rg", D_HEADS_PAD, F32, None, "tok"),
             ("aq", D_HEADS_PAD, BF16, HEAD_DIM ** -0.5 * LOG2E, "T"), ("ak", D_HEADS_PAD, BF16, None, "tok"),
             ("av", DSA_HEADS, BF16, None, "VT"),
             ("iq", IDX_HEADS * IDX_DIM, BF16, None, "T"), ("ik", LANES, BF16, None, "tok"),
             ("iw", LANES, F32, None, "T"),
             ("dq", D_HEADS_PAD, BF16, DIFF_QK_DIM ** -0.5 * LOG2E, "T"), ("dk", D_HEADS_PAD, BF16, None, "tok"),
             ("dv", DIFF_HEADS, BF16, None, "VT"))
_IW_ROWS = SUBLANES


def _proj_kernel(x_ref, w_ref, *refs):
    o_refs, h_scr = refs[:-1], refs[-1]
    xb = x_ref[...].astype(BF16)
    for lo in range(0, N_IN, TN_PROJ):
        cols = pl.ds(lo, min(TN_PROJ, N_IN - lo))
        h_scr[:, cols] = _dot(xb, w_ref[:, cols].astype(BF16))
    ones = jnp.ones((BF16_ROWS, TK), BF16)
    off = 0
    for width, (name, out_width, dtype, scale, kind), o_ref in zip(IN_SPLITS, _PROJ_OUT, o_refs):
        base = off // LANES * LANES
        end = min(-(-(off + width) // LANES) * LANES, N_IN_PAD)
        seg = h_scr[:, pl.ds(base, end - base)][:, off - base:off - base + width]
        off += width
        if scale is not None:
            seg = seg * scale
        padded = -(-width // LANES) * LANES
        if name == "ik":
            seg = jnp.concatenate([seg, seg], axis=1)
        elif padded > width:
            seg = jnp.concatenate([seg, jnp.zeros((TM_PROJ, padded - width), F32)], axis=1)
        if kind == "tok":
            o_ref[...] = seg.astype(dtype)
        elif kind == "T":
            o_ref[...] = seg.T[:o_ref.shape[0]].astype(dtype)
        else:
            seg_t = seg.T.astype(dtype)
            for t in range(TM_PROJ // TK):
                for h in range(out_width):
                    o_ref[t, pl.ds(h * VT_ROWS, HEAD_DIM), :] = seg_t[h * HEAD_DIM:(h + 1) * HEAD_DIM,
                                                                      t * TK:(t + 1) * TK]
                    o_ref[t, pl.ds(h * VT_ROWS + HEAD_DIM, BF16_ROWS), :] = ones


def _proj(x, w_in, layer):
    t = x.shape[0]
    out_specs, out_shape = [], []
    for name, w, dt, _, kind in _PROJ_OUT:
        if kind == "tok":
            out_specs.append(pl.BlockSpec((TM_PROJ, w), lambda i: (i, 0)))
            out_shape.append(jax.ShapeDtypeStruct((t, w), dt))
        elif kind == "T":
            rows = _IW_ROWS if name == "iw" else w
            out_specs.append(pl.BlockSpec((rows, TM_PROJ), lambda i: (0, i)))
            out_shape.append(jax.ShapeDtypeStruct((rows, t), dt))
        else:
            out_specs.append(pl.BlockSpec((TM_PROJ // TK, w * VT_ROWS, TK), lambda i: (i, 0, 0)))
            out_shape.append(jax.ShapeDtypeStruct((t // TK, w * VT_ROWS, TK), dt))
    return pl.pallas_call(
        _proj_kernel,
        grid=(t // TM_PROJ,),
        in_specs=[pl.BlockSpec((TM_PROJ, D_MODEL), lambda i: (i, 0)),
                  pl.BlockSpec((None, D_MODEL, N_IN), lambda i: (layer, 0, 0), pipeline_mode=pl.Buffered(1))],
        out_specs=out_specs,
        out_shape=out_shape,
        scratch_shapes=[pltpu.VMEM((TM_PROJ, N_IN_PAD), F32)],
        compiler_params=_cparams(1),
        name="in_proj",
    )(x, w_in)


def _swap_halves(x):
    lane = lax.broadcasted_iota(jnp.int32, x.shape, 1)
    first_half = (lane % HEAD_DIM) < (HEAD_DIM // 2)
    from_right = pltpu.roll(x, LANES - HEAD_DIM // 2, 1)
    from_left = pltpu.roll(x, HEAD_DIM // 2, 1)
    return jnp.where(first_half, from_right, from_left)


def _retention_kernel(q_ref, k_ref, v_ref, g_ref, cos_ref, sin_ref, intra_ref, qdec_ref, kdec_ref,
                      sdec_ref, same_ref, avg_ref, o_ref, state_ref):
    @pl.when(pl.program_id(1) == 0)
    def _():
        state_ref[...] = jnp.zeros(state_ref.shape, F32)

    avg = avg_ref[...]
    group = _lane_group((TR, LANES))
    for p in range(RET_HEADS // PAIR):
        lanes = pl.ds(p * LANES, LANES)
        for blk in range(RET_BLOCKS):
            rows = pl.ds(blk * TR, TR)
            cos = cos_ref[rows, :]
            sin = sin_ref[rows, :]
            q = q_ref[0, rows, lanes]
            k = k_ref[0, rows, lanes]
            q = (q * cos + _swap_halves(q) * sin) * (HEAD_DIM ** -0.5)
            k = k * cos + _swap_halves(k) * sin
            v = v_ref[0, rows, lanes]
            kb = k.astype(BF16)
            intra = jnp.zeros((TR, LANES), F32)
            for j in range(PAIR):
                qh = jnp.where(group == j, q, 0.0).astype(BF16)
                scores = _dot_nt(qh, kb) * intra_ref[PAIR * p + j]
                oh = _dot(scores.astype(BF16), v)
                intra = jnp.where(group == j, oh, intra)
            state = state_ref[p]
            cross = _dot((q * qdec_ref[:, lanes]).astype(BF16), state.astype(BF16))
            o = intra + cross
            kd_t = (k * kdec_ref[:, lanes]).T.astype(BF16)
            state_ref[p] = state * sdec_ref[p] + _dot(kd_t, v) * same_ref[...]
            mu = _group_mean(o, avg)
            d = o - mu
            var = _group_mean(d * d, avg)
            normed = d * lax.rsqrt(var + HEAD_NORM_EPS)
            gate = g_ref[0, rows, lanes]
            o_ref[0, rows, lanes] = (gate * jax.nn.sigmoid(gate) * normed).astype(o_ref.dtype)


def _retention(rq, rk, rv, rg, batch, seq):
    cos, sin = _rotary_tables(seq)
    intra, qdec, kdec, sdec, same = _retention_tables()
    avg = jnp.asarray(_group_avg(), BF16)
    step = RET_BLOCKS * TR
    r3 = lambda a: a.reshape(batch, seq, D_HEADS_PAD)
    tok = pl.BlockSpec((1, step, D_HEADS_PAD), lambda b, n: (b, n, 0))
    const = lambda a: pl.BlockSpec(a.shape, lambda b, n: (0,) * a.ndim)
    pos = pl.BlockSpec((step, LANES), lambda b, n: (n, 0))
    out = pl.pallas_call(
        _retention_kernel,
        grid=(batch, seq // step),
        in_specs=[tok, tok, tok, tok, pos, pos, const(intra), const(qdec), const(kdec),
                  const(sdec), const(same), const(avg)],
        out_specs=tok,
        out_shape=jax.ShapeDtypeStruct((batch, seq, D_HEADS_PAD), BF16),
        scratch_shapes=[pltpu.VMEM((RET_HEADS // PAIR, LANES, LANES), F32)],
        compiler_params=_cparams(2),
        name="retention",
    )(r3(rq), r3(rk), r3(rv), r3(rg), cos, sin, intra, qdec, kdec, sdec, same, avg)
    return out.reshape(batch * seq, D_HEADS_PAD)


def _build_band(bidx_ref, tab_ref, band_ref, n_heads):
    for h in range(n_heads):
        far = tab_ref[h, FAR_BUCKET]
        for t in range(2):
            idx = bidx_ref[t]
            val = jnp.full((TK, TQ), NEG, F32)
            for bkt in range(N_BUCKETS):
                val = jnp.where(idx == bkt, (tab_ref[h, bkt] - far) * LOG2E, val)
            band_ref[h, t] = val


def _load_queries(qt_ref, q_scr, streams):
    row = lax.broadcasted_iota(jnp.int32, (LANES, TQ), 0)
    for i, (pair, _, lo, hi) in enumerate(streams):
        q_scr[i] = jnp.where((row >= lo) & (row < hi), qt_ref[pl.ds(pair * LANES, LANES), :], 0)


def _for_tiles(n, fn):
    def quad(j, carry):
        for u in range(4):
            fn(4 * j + u)
        return carry

    lax.fori_loop(0, lax.shift_right_logical(n, 2), quad, 0)
    rest = lax.rem(n, 4)

    @pl.when(rest >= 2)
    def _():
        fn(n - rest)
        fn(n - rest + 1)

    @pl.when(lax.rem(rest, 2) == 1)
    def _():
        fn(n - 1)


def _softmax_passes(qb, k_ref, vt_ref, q_scr, band_ref, mask_fn, logit_ref, m_ref, acc_ref, streams):
    m_ref[...] = jnp.full(m_ref.shape, NEG, F32)

    def logits_tile(kt, band_t):
        rows = pl.ds(pl.multiple_of(kt * TK, TK), TK)
        mask = mask_fn(kt)
        for i, (pair, head, _, _) in enumerate(streams):
            lg = _dot(k_ref[0, rows, pl.ds(pair * LANES, LANES)], q_scr[i])
            if band_t is not None:
                lg = lg + band_ref[head, band_t]
            if mask is not None:
                lg = lg + mask
            logit_ref[kt, i] = lg
            m_ref[i] = jnp.maximum(m_ref[i], jnp.max(lg.reshape(TK // SUBLANES, SUBLANES, TQ), axis=0))

    _for_tiles(jnp.maximum(qb - 1, 0), lambda kt: logits_tile(kt, None))

    @pl.when(qb >= 1)
    def _():
        logits_tile(qb - 1, 0)
        logits_tile(qb, 1)

    @pl.when(qb == 0)
    def _():
        logits_tile(qb, 1)

    m_fin = [jnp.max(m_ref[i], axis=0, keepdims=True) for i in range(len(streams))]
    acc_ref[...] = jnp.zeros(acc_ref.shape, F32)

    def values_tile(kt):
        for i, (_, head, _, _) in enumerate(streams):
            p = jnp.exp2(logit_ref[kt, i] - m_fin[i])
            acc_ref[i] += _dot(vt_ref[0, kt, pl.ds(head * VT_ROWS, VT_ROWS), :], p.astype(BF16))

    _for_tiles(qb + 1, values_tile)


def _normalised(acc_ref, i):
    a = acc_ref[i]
    return a[:HEAD_DIM] / a[HEAD_DIM:HEAD_DIM + 1]


_DSA_STREAMS = tuple((h // PAIR, h, (h % PAIR) * HEAD_DIM, (h % PAIR + 1) * HEAD_DIM) for h in range(DSA_HEADS))


def _f32_to_key(x):
    bits = lax.bitcast_convert_type(x, jnp.int32)
    return jnp.where(bits < 0, ~bits, bits ^ jnp.int32(-2 ** 31))


def _key_to_f32(key):
    bits = jnp.where(key < 0, key ^ jnp.int32(-2 ** 31), ~key)
    return lax.bitcast_convert_type(bits, F32)


def _dsa_kernel(tab_ref, qt_ref, k_ref, vt_ref, iqt_ref, ik_ref, iwt_ref, bidx_ref, tril_ref,
                o_ref, band_ref, score_ref, q_scr, logit_ref, m_ref, acc_ref, tiecnt_ref, tie_ref, stat_ref):
    qb = pl.program_id(1)
    n_tiles = qb + 1

    @pl.when((pl.program_id(0) == 0) & (qb == 0))
    def _():
        _build_band(bidx_ref, tab_ref, band_ref, DSA_HEADS)

    iw = iwt_ref[...] * (IDX_HEADS ** -0.5 * IDX_DIM ** -0.5)
    row = lax.broadcasted_iota(jnp.int32, (LANES, TQ), 0)
    iq = [jnp.where(row // HEAD_DIM == (h % PAIR), iqt_ref[pl.ds((h // PAIR) * LANES, LANES), :], 0)
          for h in range(IDX_HEADS)]

    def score_tile(kt):
        ki = ik_ref[0, pl.ds(pl.multiple_of(kt * TK, TK), TK), :]
        s = jnp.zeros((TK, TQ), F32)
        for h in range(IDX_HEADS):
            s = s + jnp.maximum(_dot(ki, iq[h]), 0.0) * iw[h:h + 1, :]
        return s

    stat_ref[0] = jnp.full((SUBLANES, TQ), -jnp.inf, F32)
    stat_ref[1] = jnp.full((SUBLANES, TQ), jnp.inf, F32)
    stat_ref[2] = jnp.zeros((SUBLANES, TQ), F32)
    stat_ref[3] = jnp.zeros((SUBLANES, TQ), F32)

    def add_stats(s_for_max, s_for_min):
        hi_part = s_for_max.reshape(TK // SUBLANES, SUBLANES, TQ)
        stat_ref[0] = jnp.maximum(stat_ref[0], jnp.max(hi_part, axis=0))
        stat_ref[1] = jnp.minimum(stat_ref[1], jnp.min(s_for_min.reshape(TK // SUBLANES, SUBLANES, TQ), axis=0))
        stat_ref[2] += jnp.sum(jnp.where(hi_part > 0.0, 1.0, 0.0), axis=0)
        stat_ref[3] += jnp.sum(jnp.where(hi_part >= 0.0, 1.0, 0.0), axis=0)

    def past_scores(kt):
        s = score_tile(kt)
        score_ref[kt] = s
        add_stats(s, s)

    _for_tiles(qb, past_scores)
    s_diag = score_tile(qb)
    visible = band_ref[0, 1] > 0.5 * NEG
    score_ref[qb] = jnp.where(visible, s_diag, -jnp.inf)
    add_stats(jnp.where(visible, s_diag, -jnp.inf), jnp.where(visible, s_diag, jnp.inf))

    k_sel = float(DSA_TOPK_MAX)

    def count(pred):
        def body(kt, c):
            xs = score_ref[kt].reshape(TK // SUBLANES, SUBLANES, TQ)
            accs = [c[a] for a in range(N_ACC)]
            for g in range(TK // SUBLANES):
                a = accs[g % N_ACC]
                accs[g % N_ACC] = jnp.where(pred(xs[g]), a + 1.0, a)
            return jnp.stack(accs)
        c = lax.fori_loop(0, n_tiles, body, jnp.zeros((N_ACC, SUBLANES, TQ), F32))
        return jnp.sum(jnp.sum(c, axis=0), axis=0, keepdims=True)

    col = lax.broadcasted_iota(jnp.int32, (1, TQ), 1)
    limit = qb * TQ + (col // CHUNK + 1) * CHUNK
    select_all = limit <= DSA_TOPK_MAX
    s_max = jnp.max(stat_ref[0], axis=0, keepdims=True)
    s_min = jnp.min(stat_ref[1], axis=0, keepdims=True)
    n_pos = jnp.sum(stat_ref[2], axis=0, keepdims=True)
    n_nonneg = jnp.sum(stat_ref[3], axis=0, keepdims=True)

    positive = n_pos > k_sel
    zero_thr = (n_pos <= k_sel) & (k_sel <= n_nonneg)
    lo0 = jnp.where(positive, 0.0, _key_to_f32(_f32_to_key(s_min) - 1))
    hi0 = jnp.where(positive, _key_to_f32(_f32_to_key(s_max) + 1), 0.0)
    flo0 = jnp.where(positive, n_pos, limit.astype(F32)) - k_sel
    fhi0 = jnp.where(positive, 0.0, n_nonneg) - k_sel
    done0 = jnp.where(select_all | zero_thr, 1.0, 0.0)
    thr0 = jnp.where(select_all, -jnp.inf, 0.0)

    def search_step(vec):
        lo, hi, flo, fhi, side, done, thr = vec
        w = flo / (flo - fhi)
        t = lo * (1.0 - w) + hi * w
        t = jnp.where((t > lo) & (t < hi), t, 0.5 * lo + 0.5 * hi)
        splits = (t > lo) & (t < hi)
        f = count(lambda s: s >= t) - k_sel
        active = done == 0.0
        hit = active & splits & (f == 0.0)
        adjacent = active & ~splits
        thr = jnp.where(hit, t, jnp.where(adjacent, lo, thr))
        done = jnp.where(hit | adjacent, 1.0, done)
        up = active & splits & (f > 0.0)
        down = active & splits & (f < 0.0)
        fhi_new = jnp.where(up & (side > 0.0), 0.5 * fhi, jnp.where(down, f, fhi))
        flo_new = jnp.where(down & (side < 0.0), 0.5 * flo, jnp.where(up, f, flo))
        lo = jnp.where(up, t, lo)
        hi = jnp.where(down, t, hi)
        side = jnp.where(up, 1.0, jnp.where(down, -1.0, side))
        return lo, hi, flo_new, fhi_new, side, done, thr

    def open_queries(done):
        return (jnp.min(done) == 0.0).astype(jnp.int32)

    def search_round(state):
        rounds, _, vec = state
        for _ in range(SEARCH_STEPS_PER_CHECK):
            vec = search_step(vec)
        return rounds + 1, open_queries(vec[5]), vec

    vec0 = (lo0, hi0, flo0, fhi0, jnp.zeros((1, TQ), F32), done0, thr0)
    state = lax.while_loop(lambda st: (st[0] < SEARCH_ROUNDS) & (st[1] > 0), search_round,
                           (jnp.int32(0), open_queries(done0), vec0))
    done, thr = state[2][5], state[2][6]

    def bisect(i, prefix):
        cand = prefix | lax.shift_left(jnp.int32(1), 31 - i)
        cand_f = _key_to_f32(cand)
        take = count(lambda s: s >= cand_f) >= k_sel
        return jnp.where(take, cand, prefix)

    def exact_thr():
        prefix = lax.fori_loop(0, 32, bisect, jnp.zeros((1, TQ), jnp.int32))
        return jnp.where(done == 0.0, _key_to_f32(prefix), thr)

    thr = lax.cond(state[1] > 0, exact_thr, lambda: thr)
    n_gt = count(lambda s: s > thr)
    need = jnp.where(select_all, 0.0, k_sel - n_gt)

    def tie_counts(kt, carry):
        tied = jnp.where(score_ref[kt] == thr, 1.0, 0.0).reshape(TK // SUBLANES, SUBLANES, TQ)
        tiecnt_ref[kt] = jnp.sum(jnp.sum(tied, axis=0), axis=0, keepdims=True)
        return carry

    lax.fori_loop(0, n_tiles, tie_counts, 0)
    tie_ref[...] = jnp.zeros(tie_ref.shape, F32)
    tril = tril_ref[...]

    def selection_mask(kt):
        s = score_ref[kt]
        tied = s == thr
        tied_before = tie_ref[...]
        tie_ref[...] = tied_before + tiecnt_ref[kt]
        tied_upto = tied_before + _dot(tril, jnp.where(tied, 1.0, 0.0).astype(BF16))
        return jnp.where((s > thr) | (tied & (tied_upto <= need)), 0.0, NEG)

    _load_queries(qt_ref, q_scr, _DSA_STREAMS)
    _softmax_passes(qb, k_ref, vt_ref, q_scr, band_ref, selection_mask, logit_ref, m_ref, acc_ref, _DSA_STREAMS)

    for p in range(N_PAIRS):
        halves = [_normalised(acc_ref, PAIR * p + j) if PAIR * p + j < DSA_HEADS
                  else jnp.zeros((HEAD_DIM, TQ), F32) for j in range(PAIR)]
        o_ref[0, :, pl.ds(p * LANES, LANES)] = jnp.concatenate(halves, axis=0).T.astype(o_ref.dtype)


def _dsa(aqT, ak, avT, iqT, ik, iwT, bias_t, batch, seq):
    bidx = jnp.asarray(_band_bucket_index_t())
    tril = jnp.asarray(_prefix_tril(), BF16)
    nq = seq // TQ
    n_streams = len(_DSA_STREAMS)
    qcols = lambda w: pl.BlockSpec((w, TQ), lambda b, n: (0, b * nq + n))
    ktok = lambda w: pl.BlockSpec((1, seq, w), lambda b, n: (b, 0, 0))
    const = lambda a: pl.BlockSpec(a.shape, lambda b, n: (0,) * a.ndim)
    out = pl.pallas_call(
        _dsa_kernel,
        grid=(batch, nq),
        in_specs=[pl.BlockSpec(memory_space=pltpu.SMEM),
                  qcols(D_HEADS_PAD), ktok(D_HEADS_PAD),
                  pl.BlockSpec((1, seq // TK, DSA_HEADS * VT_ROWS, TK), lambda b, n: (b, 0, 0, 0)),
                  qcols(IDX_HEADS * IDX_DIM), ktok(LANES), qcols(_IW_ROWS), const(bidx), const(tril)],
        out_specs=pl.BlockSpec((1, TQ, D_HEADS_PAD), lambda b, n: (b, n, 0)),
        out_shape=jax.ShapeDtypeStruct((batch, seq, D_HEADS_PAD), BF16),
        scratch_shapes=[pltpu.VMEM((DSA_HEADS, 2, TK, TQ), F32),
                        pltpu.VMEM((seq // TK, TK, TQ), F32),
                        pltpu.VMEM((n_streams, LANES, TQ), BF16),
                        pltpu.VMEM((seq // TK, n_streams, TK, TQ), F32),
                        pltpu.VMEM((n_streams, SUBLANES, TQ), F32),
                        pltpu.VMEM((n_streams, VT_ROWS, TQ), F32),
                        pltpu.VMEM((seq // TK, 1, TQ), F32),
                        pltpu.VMEM((1, TQ), F32),
                        pltpu.VMEM((4, SUBLANES, TQ), F32)],
        compiler_params=_cparams(2),
        name="dsa_attention",
    )(bias_t, aqT, ak.reshape(batch, seq, D_HEADS_PAD),
      avT.reshape(batch, seq // TK, DSA_HEADS * VT_ROWS, TK), iqT, ik.reshape(batch, seq, LANES), iwT, bidx, tril)
    return out.reshape(batch * seq, D_HEADS_PAD)


_DIFF_STREAMS = tuple(
    (h // PAIR, h, (h % PAIR) * HEAD_DIM + mm * DIFF_QK_DIM, (h % PAIR) * HEAD_DIM + (mm + 1) * DIFF_QK_DIM)
    for h in range(DIFF_HEADS) for mm in range(2))


def _diff_kernel(lambda_init, tab_ref, qt_ref, k_ref, vt_ref, lam_ref, g_ref, bidx_ref, avg_ref,
                 o_ref, band_ref, q_scr, logit_ref, m_ref, acc_ref):
    qb = pl.program_id(1)

    @pl.when((pl.program_id(0) == 0) & (qb == 0))
    def _():
        _build_band(bidx_ref, tab_ref, band_ref, DIFF_HEADS)

    lv = lam_ref[...]
    lam = (jnp.exp(jnp.sum(lv[0:1] * lv[1:2], axis=-1, keepdims=True))
           - jnp.exp(jnp.sum(lv[2:3] * lv[3:4], axis=-1, keepdims=True)) + lambda_init)

    _load_queries(qt_ref, q_scr, _DIFF_STREAMS)
    _softmax_passes(qb, k_ref, vt_ref, q_scr, band_ref, lambda kt: None, logit_ref, m_ref, acc_ref,
                    _DIFF_STREAMS)

    avg = avg_ref[...]
    for p in range(N_PAIRS):
        halves = []
        for j in range(PAIR):
            h = PAIR * p + j
            if h < DIFF_HEADS:
                halves.append(_normalised(acc_ref, 2 * h) - lam * _normalised(acc_ref, 2 * h + 1))
            else:
                halves.append(jnp.zeros((HEAD_DIM, TQ), F32))
        out = jnp.concatenate(halves, axis=0).T
        ms = _group_mean(out * out, avg)
        out = out * lax.rsqrt(ms + LN_EPS) * g_ref[...] * (1.0 - lambda_init)
        o_ref[0, :, pl.ds(p * LANES, LANES)] = out.astype(o_ref.dtype)


def _diff(dqT, dk, dvT, lam_vecs, subln_g, bias_t, lambda_init, batch, seq):
    bidx = jnp.asarray(_band_bucket_index_t())
    avg = jnp.asarray(_group_avg(), BF16)
    g2 = jnp.concatenate([subln_g, subln_g]).reshape(1, LANES)
    nq = seq // TQ
    n_streams = len(_DIFF_STREAMS)
    const = lambda a: pl.BlockSpec(a.shape, lambda b, n: (0,) * a.ndim)
    out = pl.pallas_call(
        functools.partial(_diff_kernel, lambda_init),
        grid=(batch, nq),
        in_specs=[pl.BlockSpec(memory_space=pltpu.SMEM),
                  pl.BlockSpec((D_HEADS_PAD, TQ), lambda b, n: (0, b * nq + n)),
                  pl.BlockSpec((1, seq, D_HEADS_PAD), lambda b, n: (b, 0, 0)),
                  pl.BlockSpec((1, seq // TK, DIFF_HEADS * VT_ROWS, TK), lambda b, n: (b, 0, 0, 0)),
                  const(lam_vecs), const(g2), const(bidx), const(avg)],
        out_specs=pl.BlockSpec((1, TQ, D_HEADS_PAD), lambda b, n: (b, n, 0)),
        out_shape=jax.ShapeDtypeStruct((batch, seq, D_HEADS_PAD), BF16),
        scratch_shapes=[pltpu.VMEM((DIFF_HEADS, 2, TK, TQ), F32),
                        pltpu.VMEM((n_streams, LANES, TQ), BF16),
                        pltpu.VMEM((seq // TK, n_streams, TK, TQ), F32),
                        pltpu.VMEM((n_streams, SUBLANES, TQ), F32),
                        pltpu.VMEM((n_streams, VT_ROWS, TQ), F32)],
        compiler_params=_cparams(2),
        name="diff_attention",
    )(bias_t, dqT, dk.reshape(batch, seq, D_HEADS_PAD),
      dvT.reshape(batch, seq // TK, DIFF_HEADS * VT_ROWS, TK), lam_vecs, g2, bidx, avg)
    return out.reshape(batch * seq, D_HEADS_PAD)


def _out_ffn_kernel(layer, x_ref, ret_ref, dsa_ref, dif_ref, wo_ref, g1_ref, b1_ref, wg_hbm, wu_hbm, wd_hbm,
                    g2_ref, b2_ref, o_ref, wg_v, wu_v, wd_v, sem):
    def compute(before_chunk):
        mixed = _dot(jnp.concatenate([ret_ref[...], dsa_ref[...], dif_ref[...]], axis=1), wo_ref[...])
        x = _layer_norm(DEEPNORM_ALPHA * x_ref[...] + mixed, g1_ref[...], b1_ref[...])
        o_ref[...] = _ffn_ln_body(x, wg_v, wu_v, wd_v, g2_ref[...], b2_ref[...], before_chunk)

    _with_resident_ffn_weights(layer, (wg_hbm, wu_hbm, wd_hbm), (wg_v, wu_v, wd_v), sem, compute)


def _pad_w_out(w_out):
    zeros = jnp.zeros((D_HEADS_PAD - D_DSA, w_out.shape[1]), w_out.dtype)
    return jnp.concatenate([w_out[:D_RET], w_out[D_RET:D_RET + D_DSA], zeros,
                            w_out[D_RET + D_DSA:], zeros], axis=0)


def _out_ffn(x, ret, dsa, dif, w_pad, g1, b1, wg, wu, wd, g2, b2, layer):
    t = x.shape[0]
    const = lambda shape: pl.BlockSpec(shape, lambda i: (0, 0), pipeline_mode=pl.Buffered(1))
    tok = lambda w: pl.BlockSpec((TM_FFN, w), lambda i: (i, 0))
    hbm = pl.BlockSpec(memory_space=pl.ANY)
    return pl.pallas_call(
        functools.partial(_out_ffn_kernel, layer),
        grid=(t // TM_FFN,),
        in_specs=[tok(D_MODEL), tok(D_HEADS_PAD), tok(D_HEADS_PAD), tok(D_HEADS_PAD),
                  const((3 * D_HEADS_PAD, D_MODEL)), const((1, D_MODEL)), const((1, D_MODEL)),
                  hbm, hbm, hbm, const((1, D_MODEL)), const((1, D_MODEL))],
        out_specs=tok(D_MODEL),
        out_shape=jax.ShapeDtypeStruct((t, D_MODEL), F32),
        scratch_shapes=list(_FFN_WEIGHT_SCRATCH),
        compiler_params=_cparams(1, VMEM_LIMIT_FUSED_FFN_BYTES),
        name="out_proj_ffn_ln",
    )(x, ret, dsa, dif, w_pad, g1, b1, wg, wu, wd, g2, b2)


def kernel(x, w_in, w_out, ffn1_wg, ffn1_wu, ffn1_wd, ffn2_wg, ffn2_wu, ffn2_wd,
           ln_g, ln_b, diff_lambda, diff_subln_g, rel_bias):
    batch, seq, d = x.shape
    assert d == D_MODEL and seq % TQ == 0 and seq % (RET_BLOCKS * TR) == 0 and (batch * seq) % TM_FFN == 0
    assert min(DSA_TOPK_MAX, seq // 4) == DSA_TOPK_MAX
    h = x.reshape(batch * seq, d)
    bias_t = rel_bias.T
    row = lambda a: a.reshape(1, D_MODEL)
    for l in range(DEPTH):
        lambda_init = 0.8 - 0.6 * math.exp(-0.3 * l)
        h = _ffn_ln(h, ffn1_wg, ffn1_wu, ffn1_wd, row(ln_g[l, 0]), row(ln_b[l, 0]), l)
        (rq, rk, rv, rg, aqT, ak, avT, iqT, ik, iwT, dqT, dk, dvT) = _proj(h, w_in, l)
        ret = _retention(rq, rk, rv, rg, batch, seq)
        dsa = _dsa(aqT, ak, avT, iqT, ik, iwT, bias_t[:DSA_HEADS], batch, seq)
        dif = _diff(dqT, dk, dvT, diff_lambda[l], diff_subln_g[l], bias_t[DSA_HEADS:], lambda_init, batch, seq)
        h = _out_ffn(h, ret, dsa, dif, _pad_w_out(w_out[l]).astype(BF16), row(ln_g[l, 1]), row(ln_b[l, 1]),
                     ffn2_wg, ffn2_wu, ffn2_wd, row(ln_g[l, 2]), row(ln_b[l, 2]), l)
    return h.reshape(batch, seq, d)
```
